```python
import math
import jax, jax.numpy as jnp
from jax import lax
import numpy as np

D_MODEL = 2048
BATCH = 4
SEQ = 2048
DEPTH = 4

ATTN_WIDTH = D_MODEL // 2
CONV_WIDTH = D_MODEL - ATTN_WIDTH
N_HEADS = 8
HEAD_DV = ATTN_WIDTH // N_HEADS
HEAD_DK = HEAD_DV // 2
N_CONV_GROUPS = 8
CONV_K = 3
D_FF = 4 * D_MODEL
NUM_BUCKETS = 32
MAX_DISTANCE = 128
MAX_EXACT = NUM_BUCKETS // 2
Q_BLOCK = 128
EPS = 1e-6
NEG = -1e30
PROJ_WIDTH = 3 * ATTN_WIDTH + 3 * CONV_WIDTH

kernel_name = "hybrid_diffattn_shortconv_sqrelu"


def _rmsnorm(x, g):
    x32 = x.astype(jnp.float32)
    y = x32 * lax.rsqrt(jnp.mean(x32 * x32, axis=-1, keepdims=True) + EPS)
    return (y * g.astype(jnp.float32)).astype(x.dtype)


def _relative_bucket(dist):
    n = jnp.maximum(dist, 0)
    is_small = n < MAX_EXACT
    nf = jnp.maximum(n, MAX_EXACT).astype(jnp.float32)
    large = MAX_EXACT + (jnp.log(nf / MAX_EXACT) / math.log(MAX_DISTANCE / MAX_EXACT)
                         * (NUM_BUCKETS - MAX_EXACT)).astype(jnp.int32)
    large = jnp.minimum(large, NUM_BUCKETS - 1)
    return jnp.where(is_small, n, large)


def _diff_attention(q, k, v, lam, rel_bias):
    B, S, H, _, dk = q.shape
    dv = v.shape[-1]
    nblk = S // Q_BLOCK
    k_pos = jnp.arange(S, dtype=jnp.int32)
    scale = dk ** -0.5

    def block(args):
        qb, start = args
        q_pos = start + jnp.arange(Q_BLOCK, dtype=jnp.int32)
        dist = q_pos[:, None] - k_pos[None, :]
        bias = jnp.take(rel_bias, _relative_bucket(dist), axis=0)
        bias = bias.reshape(Q_BLOCK, S, H, 2).transpose(2, 3, 0, 1).astype(jnp.float32)
        s = jnp.einsum('bqhmd,bkhmd->bhmqk', qb, k).astype(jnp.float32) * scale + bias
        s = jnp.where(dist >= 0, s, NEG)
        p = jax.nn.softmax(s, axis=-1)
        p = p[:, :, 0] - lam * p[:, :, 1]
        return jnp.einsum('bhqk,bkhd->bqhd', p.astype(v.dtype), v)

    qb = q.reshape(B, nblk, Q_BLOCK, H, 2, dk).transpose(1, 0, 2, 3, 4, 5)
    starts = jnp.arange(nblk, dtype=jnp.int32) * Q_BLOCK
    out = lax.map(block, (qb, starts))
    return out.transpose(1, 0, 2, 3, 4).reshape(B, S, H, dv)


def _short_conv(u, w):
    S = u.shape[1]
    up = jnp.pad(u, ((0, 0), (CONV_K - 1, 0), (0, 0)))
    return sum(w[i] * up[:, i:i + S] for i in range(CONV_K))


def setup_inputs(seed: int = 0) -> dict:
    key = jax.random.key(seed)
    ks = jax.random.split(key, 16)
    f32 = jnp.float32

    def nrm(k, shape, scale):
        return jax.random.normal(k, shape, f32) * scale

    def gain(k, shape):
        return 1.0 + 0.02 * jax.random.normal(k, shape, f32)

    return {
        "x": nrm(ks[0], (BATCH, SEQ, D_MODEL), 1.0),
        "w_in": nrm(ks[1], (DEPTH, D_MODEL, PROJ_WIDTH), D_MODEL ** -0.5),
        "w_out": nrm(ks[2], (DEPTH, ATTN_WIDTH + CONV_WIDTH, D_MODEL), (ATTN_WIDTH + CONV_WIDTH) ** -0.5),
        "conv_w": nrm(ks[3], (DEPTH, CONV_K, CONV_WIDTH), CONV_K ** -0.5),
        "q_norm_g": gain(ks[4], (DEPTH, HEAD_DK)),
        "k_norm_g": gain(ks[5], (DEPTH, HEAD_DK)),
        "lambda_q1": nrm(ks[6], (DEPTH, HEAD_DK), 0.1),
        "lambda_k1": nrm(ks[7], (DEPTH, HEAD_DK), 0.1),
        "lambda_q2": nrm(ks[8], (DEPTH, HEAD_DK), 0.1),
        "lambda_k2": nrm(ks[9], (DEPTH, HEAD_DK), 0.1),
        "subln_g": gain(ks[10], (DEPTH, HEAD_DV)),
        "attn_norm_g": gain(ks[11], (DEPTH, D_MODEL)),
        "mlp_norm_g": gain(ks[12], (DEPTH, D_MODEL)),
        "w_up": nrm(ks[13], (DEPTH, D_MODEL, D_FF), D_MODEL ** -0.5),
        "w_down": nrm(ks[14], (DEPTH, D_FF, D_MODEL), D_FF ** -0.5),
        "rel_bias": nrm(ks[15], (NUM_BUCKETS, 2 * N_HEADS), 0.5),
    }


def reference(x, w_in, w_out, conv_w, q_norm_g, k_norm_g, lambda_q1, lambda_k1,
              lambda_q2, lambda_k2, subln_g, attn_norm_g, mlp_norm_g, w_up, w_down,
              rel_bias):
    B, S, _ = x.shape
    splits = [ATTN_WIDTH, 2 * ATTN_WIDTH, 3 * ATTN_WIDTH,
              3 * ATTN_WIDTH + CONV_WIDTH, 3 * ATTN_WIDTH + 2 * CONV_WIDTH]
    for l in range(DEPTH):
        h = _rmsnorm(x, attn_norm_g[l])
        proj = h @ w_in[l]
        q, k, v, gate_b, gate_c, conv_in = jnp.split(proj, splits, axis=-1)

        q = _rmsnorm(q.reshape(B, S, N_HEADS, 2, HEAD_DK), q_norm_g[l])
        k = _rmsnorm(k.reshape(B, S, N_HEADS, 2, HEAD_DK), k_norm_g[l])
        v = v.reshape(B, S, N_HEADS, HEAD_DV)
        lam_init = 0.8 - 0.6 * math.exp(-0.3 * l)
        lam = (jnp.exp(jnp.sum(lambda_q1[l].astype(jnp.float32) * lambda_k1[l].astype(jnp.float32)))
               - jnp.exp(jnp.sum(lambda_q2[l].astype(jnp.float32) * lambda_k2[l].astype(jnp.float32)))
               + lam_init)
        attn = _diff_attention(q, k, v, lam, rel_bias)
        attn = (_rmsnorm(attn, subln_g[l]) * (1.0 - lam_init)).reshape(B, S, ATTN_WIDTH)

        conv = gate_b * _short_conv(gate_c * conv_in, conv_w[l])

        x = x + jnp.concatenate([attn, conv.astype(attn.dtype)], axis=-1) @ w_out[l]

        hm = _rmsnorm(x, mlp_norm_g[l]) @ w_up[l]
        x = x + jnp.square(jax.nn.relu(hm)) @ w_down[l]
    return x
```

```python
import functools
import math

import numpy as np
import jax
import jax.numpy as jnp
from jax import lax
from jax.experimental import pallas as pl
from jax.experimental.pallas import tpu as pltpu

HEAD_DV = 128
HEAD_DK = HEAD_DV // 2
CONV_K = 3
NUM_BUCKETS = 32
MAX_DISTANCE = 128
MAX_EXACT = NUM_BUCKETS // 2
EPS = 1e-6
NEG = -1e30

LANES = 128
SUBLANES = 8
VMEM_LIMIT_BYTES = 56 * 1024 * 1024

F32 = jnp.float32
BF16 = jnp.bfloat16


def _tile(dim, pref):
    if dim <= pref:
        return dim
    t = (pref // LANES) * LANES
    while dim % t:
        t -= LANES
    return t


def _params(n_axes):
    return pltpu.CompilerParams(
        dimension_semantics=("arbitrary",) * n_axes,
        vmem_limit_bytes=VMEM_LIMIT_BYTES,
    )


def _rmsnorm_kernel(x_ref, g_ref, o_ref):
    x = x_ref[...]
    ms = jnp.mean(x * x, axis=-1, keepdims=True)
    o_ref[...] = (x * lax.rsqrt(ms + EPS) * g_ref[...]).astype(o_ref.dtype)


def _rmsnorm(x, g):
    m, d = x.shape
    tm = _tile(m, 512)
    return pl.pallas_call(
        _rmsnorm_kernel,
        grid=(m // tm,),
        in_specs=[pl.BlockSpec((tm, d), lambda i: (i, 0)),
                  pl.BlockSpec((1, d), lambda i: (0, 0))],
        out_specs=pl.BlockSpec((tm, d), lambda i: (i, 0)),
        out_shape=jax.ShapeDtypeStruct((m, d), BF16),
        compiler_params=_params(1),
        name="rmsnorm",
    )(x, g.reshape(1, d))


def _cast_weight(w_ref, wbf_ref):
    k = w_ref.shape[0]
    ck = _tile(k, 256)
    for r in range(0, k, ck):
        wbf_ref[r:r + ck, :] = w_ref[r:r + ck, :].astype(BF16)


def _qkv_kernel(h_ref, w_ref, gq_ref, gk_ref, o_ref, wbf_ref):
    j = pl.program_id(0)

    @pl.when(pl.program_id(1) == 0)
    def _():
        _cast_weight(w_ref, wbf_ref)

    acc = jnp.dot(h_ref[...], wbf_ref[...], preferred_element_type=F32)

    @pl.when(j == 2)
    def _():
        o_ref[...] = acc.astype(o_ref.dtype)

    @pl.when(j < 2)
    def _():
        g = jnp.where(j == 0, gq_ref[...], gk_ref[...])
        lo = lax.broadcasted_iota(jnp.int32, (1, LANES), 1) < HEAD_DK
        for c in range(acc.shape[1] // LANES):
            a = acc[:, c * LANES:(c + 1) * LANES]
            sq = a * a
            s_lo = jnp.sum(jnp.where(lo, sq, 0.0), axis=-1, keepdims=True)
            s_hi = jnp.sum(jnp.where(lo, 0.0, sq), axis=-1, keepdims=True)
            ms = jnp.where(lo, s_lo, s_hi) * (1.0 / HEAD_DK)
            o_ref[:, c * LANES:(c + 1) * LANES] = (a * lax.rsqrt(ms + EPS) * g).astype(o_ref.dtype)


def _qkv_proj(h, w_in, layer, gq, gk, aw):
    m, d = h.shape
    bm = _tile(m, 1024)
    return pl.pallas_call(
        _qkv_kernel,
        grid=(3, m // bm),
        in_specs=[pl.BlockSpec((bm, d), lambda j, i: (i, 0)),
                  pl.BlockSpec((None, d, aw), lambda j, i: (layer, 0, j)),
                  pl.BlockSpec((1, LANES), lambda j, i: (0, 0)),
                  pl.BlockSpec((1, LANES), lambda j, i: (0, 0))],
        out_specs=pl.BlockSpec((bm, aw), lambda j, i: (i, j)),
        out_shape=jax.ShapeDtypeStruct((m, 3 * aw), BF16),
        scratch_shapes=[pltpu.VMEM((d, aw), BF16)],
        compiler_params=_params(2),
        name="qkv_proj",
    )(h, w_in, gq, gk)


def _conv_kernel(h_ref, wb_ref, wc_ref, wi_ref, cw_ref, o_ref, wbf_ref, ubuf_ref, *, tiles_per_seq):
    i = pl.program_id(1)
    bm = h_ref.shape[0]

    @pl.when(i == 0)
    def _():
        _cast_weight(wb_ref, wbf_ref.at[0])
        _cast_weight(wc_ref, wbf_ref.at[1])
        _cast_weight(wi_ref, wbf_ref.at[2])

    @pl.when(i % tiles_per_seq == 0)
    def _():
        ubuf_ref[0:SUBLANES, :] = jnp.zeros((SUBLANES, ubuf_ref.shape[1]), F32)

    h = h_ref[...]
    gate_c = jnp.dot(h, wbf_ref[1], preferred_element_type=F32)
    conv_in = jnp.dot(h, wbf_ref[2], preferred_element_type=F32)
    ubuf_ref[SUBLANES:SUBLANES + bm, :] = gate_c * conv_in
    gate_b = jnp.dot(h, wbf_ref[0], preferred_element_type=F32)
    cw = cw_ref[...]
    conv = (cw[0:1] * ubuf_ref[SUBLANES - 2:SUBLANES - 2 + bm, :]
            + cw[1:2] * ubuf_ref[SUBLANES - 1:SUBLANES - 1 + bm, :]
            + cw[2:3] * ubuf_ref[SUBLANES:SUBLANES + bm, :])
    o_ref[...] = (gate_b * conv).astype(o_ref.dtype)
    ubuf_ref[0:SUBLANES, :] = ubuf_ref[bm:bm + SUBLANES, :]


def _conv_proj(h, w_in, conv_w, layer, aw, cwid, seq):
    m, d = h.shape
    bm = _tile(seq, 1024)
    bn = _tile(cwid, 512)
    off = 3 * aw // bn
    nb = cwid // bn
    return pl.pallas_call(
        functools.partial(_conv_kernel, tiles_per_seq=seq // bm),
        grid=(nb, m // bm),
        in_specs=[pl.BlockSpec((bm, d), lambda j, i: (i, 0)),
                  pl.BlockSpec((None, d, bn), lambda j, i: (layer, 0, off + j)),
                  pl.BlockSpec((None, d, bn), lambda j, i: (layer, 0, off + nb + j)),
                  pl.BlockSpec((None, d, bn), lambda j, i: (layer, 0, off + 2 * nb + j)),
                  pl.BlockSpec((None, CONV_K, bn), lambda j, i: (layer, 0, j))],
        out_specs=pl.BlockSpec((bm, bn), lambda j, i: (i, j)),
        out_shape=jax.ShapeDtypeStruct((m, cwid), BF16),
        scratch_shapes=[pltpu.VMEM((3, d, bn), BF16),
                        pltpu.VMEM((bm + SUBLANES, bn), F32)],
        compiler_params=_params(2),
        name="conv_proj",
    )(h, w_in, w_in, w_in, conv_w)


def _bucket(n):
    nf = np.maximum(n, MAX_EXACT).astype(np.float64)
    val = np.log(nf / MAX_EXACT) / math.log(MAX_DISTANCE / MAX_EXACT) * (NUM_BUCKETS - MAX_EXACT)
    large = np.minimum(MAX_EXACT + val.astype(np.int64), NUM_BUCKETS - 1)
    return np.where(n < MAX_EXACT, n, large)


def _bucket_tables(t):
    assert _bucket(np.array([t + 1]))[0] == NUM_BUCKETS - 1
    r = np.arange(t)[:, None]
    c = np.arange(t)[None, :]
    d0 = r - c
    diag = np.where(d0 >= 0, _bucket(np.maximum(d0, 0)), -1)
    prev = _bucket(t + r - c)
    return np.stack([diag, prev]).astype(np.int32)


def _bias_table_kernel(rb_ref, bk_ref, o_ref):
    h = pl.program_id(0)
    for mode in range(2):
        bk = bk_ref[mode]
        for mp in range(2):
            col = 2 * h + mp
            far = rb_ref[NUM_BUCKETS - 1, col]
            acc = jnp.where(bk < 0, NEG, 0.0).astype(F32)
            for b in range(NUM_BUCKETS - 1):
                acc = jnp.where(bk == b, rb_ref[b, col] - far, acc)
            o_ref[0, mode, mp] = acc


def _bias_tables(rel_bias, n_heads, t):
    buckets = jnp.asarray(_bucket_tables(t))
    return pl.pallas_call(
        _bias_table_kernel,
        grid=(n_heads,),
        in_specs=[pl.BlockSpec(memory_space=pltpu.SMEM),
                  pl.BlockSpec((2, t, t), lambda h: (0, 0, 0))],
        out_specs=pl.BlockSpec((1, 2, 2, t, t), lambda h: (h, 0, 0, 0, 0)),
        out_shape=jax.ShapeDtypeStruct((n_heads, 2, 2, t, t), F32),
        compiler_params=_params(1),
        name="bias_tables",
    )(rel_bias, buckets)


def _attn_kernel(q_ref, k_ref, v_ref, tb_ref, lam_ref, g_ref, o_ref, m_ref, l_ref, acc_ref,
                 *, lam_init):
    i = pl.program_id(2)
    t = q_ref.shape[0]

    q = q_ref[...]
    lo = lax.broadcasted_iota(jnp.int32, (1, LANES), 1) < HEAD_DK
    zero = jnp.zeros_like(q)
    q2 = jnp.concatenate([jnp.where(lo, q, zero), jnp.where(lo, zero, q)], axis=0)

    m_ref[...] = jnp.full(m_ref.shape, NEG, F32)
    l_ref[...] = jnp.zeros(l_ref.shape, F32)
    acc_ref[...] = jnp.zeros(acc_ref.shape, F32)

    def step(j, bias):
        start = pl.multiple_of(j * t, t)
        kc = k_ref[pl.ds(start, t), :]
        vc = v_ref[pl.ds(start, t), :]
        s = lax.dot_general(q2, kc, (((1,), (1,)), ((), ())), preferred_element_type=F32)
        if bias is not None:
            s = s + bias
        m_prev = m_ref[...]
        m_new = jnp.maximum(m_prev, jnp.max(s, axis=-1, keepdims=True))
        alpha = jnp.exp(m_prev - m_new)
        p = jnp.exp(s - m_new)
        l_ref[...] = alpha * l_ref[...] + jnp.sum(p, axis=-1, keepdims=True)
        acc_ref[...] = alpha * acc_ref[...] + jnp.dot(p.astype(BF16), vc, preferred_element_type=F32)
        m_ref[...] = m_new

    def far_step(j, carry):
        step(j, None)
        return carry

    lax.fori_loop(0, i - 1, far_step, 0)

    @pl.when(i >= 1)
    def _():
        step(i - 1, tb_ref[0, 1].reshape(2 * t, t))

    step(i, tb_ref[0, 0].reshape(2 * t, t))

    lv = lam_ref[...]
    lam = (jnp.exp(jnp.sum(lv[0:1] * lv[1:2], axis=-1, keepdims=True))
           - jnp.exp(jnp.sum(lv[2:3] * lv[3:4], axis=-1, keepdims=True)) + lam_init)
    out = acc_ref[...] / l_ref[...]
    o = out[0:t] - lam * out[t:2 * t]
    ms = jnp.mean(o * o, axis=-1, keepdims=True)
    o_ref[...] = (o * lax.rsqrt(ms + EPS) * g_ref[...] * (1.0 - lam_init)).astype(o_ref.dtype)


def _attention(qkv, tables, lam_vecs, subln_g, lam_init, batch, seq, n_heads, t):
    nq = seq // t
    return pl.pallas_call(
        functools.partial(_attn_kernel, lam_init=lam_init),
        grid=(batch, n_heads, nq),
        in_specs=[pl.BlockSpec((t, HEAD_DV), lambda b, h, i: (b * nq + i, h)),
                  pl.BlockSpec((seq, HEAD_DV), lambda b, h, i: (b, n_heads + h)),
                  pl.BlockSpec((seq, HEAD_DV), lambda b, h, i: (b, 2 * n_heads + h)),
                  pl.BlockSpec((1, 2, 2, t, t), lambda b, h, i: (h, 0, 0, 0, 0)),
                  pl.BlockSpec((4, HEAD_DK), lambda b, h, i: (0, 0)),
                  pl.BlockSpec((1, HEAD_DV), lambda b, h, i: (0, 0))],
        out_specs=pl.BlockSpec((t, HEAD_DV), lambda b, h, i: (b * nq + i, h)),
        out_shape=jax.ShapeDtypeStruct((batch * seq, n_heads * HEAD_DV), BF16),
        scratch_shapes=[pltpu.VMEM((2 * t, 1), F32),
                        pltpu.VMEM((2 * t, 1), F32),
                        pltpu.VMEM((2 * t, HEAD_DV), F32)],
        compiler_params=_params(3),
        name="diff_attention",
    )(qkv, qkv, qkv, tables, lam_vecs, subln_g)


def _out_kernel(a_ref, c_ref, w_ref, x_ref, o_ref, wbf_ref):
    @pl.when(pl.program_id(1) == 0)
    def _():
        _cast_weight(w_ref, wbf_ref)

    aw = a_ref.shape[1]
    acc = jnp.dot(a_ref[...], wbf_ref[0:aw, :], preferred_element_type=F32)
    acc = acc + jnp.dot(c_ref[...], wbf_ref[aw:, :], preferred_element_type=F32)
    o_ref[...] = x_ref[...] + acc


def _out_proj(attn, conv, w_out, layer, x):
    m, d = x.shape
    aw, cwid = attn.shape[1], conv.shape[1]
    bm = _tile(m, 512)
    bn = _tile(d, 1024)
    return pl.pallas_call(
        _out_kernel,
        grid=(d // bn, m // bm),
        in_specs=[pl.BlockSpec((bm, aw), lambda j, i: (i, 0)),
                  pl.BlockSpec((bm, cwid), lambda j, i: (i, 0)),
                  pl.BlockSpec((None, aw + cwid, bn), lambda j, i: (layer, 0, j)),
                  pl.BlockSpec((bm, bn), lambda j, i: (i, j))],
        out_specs=pl.BlockSpec((bm, bn), lambda j, i: (i, j)),
        out_shape=jax.ShapeDtypeStruct((m, d), F32),
        scratch_shapes=[pltpu.VMEM((aw + cwid, bn), BF16)],
        compiler_params=_params(2),
        name="out_proj",
    )(attn, conv, w_out, x)


def _up_kernel(h_ref, w_ref, o_ref, wbf_ref):
    @pl.when(pl.program_id(1) == 0)
    def _():
        _cast_weight(w_ref, wbf_ref)

    acc = jnp.dot(h_ref[...], wbf_ref[...], preferred_element_type=F32)
    o_ref[...] = jnp.square(jnp.maximum(acc, 0.0)).astype(o_ref.dtype)


def _up_proj(h, w_up, layer):
    m, d = h.shape
    f = w_up.shape[2]
    bm = _tile(m, 1024)
    bn = _tile(f, 1024)
    return pl.pallas_call(
        _up_kernel,
        grid=(f // bn, m // bm),
        in_specs=[pl.BlockSpec((bm, d), lambda j, i: (i, 0)),
                  pl.BlockSpec((None, d, bn), lambda j, i: (layer, 0, j))],
        out_specs=pl.BlockSpec((bm, bn), lambda j, i: (i, j)),
        out_shape=jax.ShapeDtypeStruct((m, f), BF16),
        scratch_shapes=[pltpu.VMEM((d, bn), BF16)],
        compiler_params=_params(2),
        name="up_proj",
    )(h, w_up)


def _down_kernel(a_ref, w_ref, x_ref, o_ref, acc_ref):
    kk = pl.program_id(2)
    part = jnp.dot(a_ref[...], w_ref[...].astype(BF16), preferred_element_type=F32)

    @pl.when(kk == 0)
    def _():
        acc_ref[...] = part

    @pl.when(kk > 0)
    def _():
        acc_ref[...] += part

    @pl.when(kk == pl.num_programs(2) - 1)
    def _():
        o_ref[...] = x_ref[...] + acc_ref[...]


def _down_proj(a, w_down, layer, x):
    m, d = x.shape
    f = a.shape[1]
    bm = _tile(m, 1024)
    bn = _tile(d, 1024)
    bk = _tile(f, 2048)
    return pl.pallas_call(
        _down_kernel,
        grid=(d // bn, m // bm, f // bk),
        in_specs=[pl.BlockSpec((bm, bk), lambda j, i, k: (i, k)),
                  pl.BlockSpec((None, bk, bn), lambda j, i, k: (layer, k, j)),
                  pl.BlockSpec((bm, bn), lambda j, i, k: (i, j))],
        out_specs=pl.BlockSpec((bm, bn), lambda j, i, k: (i, j)),
        out_shape=jax.ShapeDtypeStruct((m, d), F32),
        scratch_shapes=[pltpu.VMEM((bm, bn), F32)],
        compiler_params=_params(3),
        name="down_proj",
    )(a, w_down, x)


def kernel(x, w_in, w_out, conv_w, q_norm_g, k_norm_g, lambda_q1, lambda_k1, lambda_q2, lambda_k2,
           subln_g, attn_norm_g, mlp_norm_g, w_up, w_down, rel_bias):
    batch, seq, d = x.shape
    depth = w_in.shape[0]
    aw = d // 2
    cwid = d - aw
    n_heads = aw // HEAD_DV
    assert w_in.shape[2] == 3 * aw + 3 * cwid and rel_bias.shape == (NUM_BUCKETS, 2 * n_heads)
    t = _tile(seq, 256)

    tables = _bias_tables(rel_bias, n_heads, t)
    scale = HEAD_DK ** -0.5
    xf = x.reshape(batch * seq, d)
    for l in range(depth):
        lam_init = 0.8 - 0.6 * math.exp(-0.3 * l)
        gq = jnp.tile(q_norm_g[l] * scale, 2).reshape(1, LANES)
        gk = jnp.tile(k_norm_g[l], 2).reshape(1, LANES)
        lam_vecs = jnp.stack([lambda_q1[l], lambda_k1[l], lambda_q2[l], lambda_k2[l]])

        h = _rmsnorm(xf, attn_norm_g[l])
        qkv = _qkv_proj(h, w_in, l, gq, gk, aw)
        conv = _conv_proj(h, w_in, conv_w, l, aw, cwid, seq)
        attn = _attention(qkv, tables, lam_vecs, subln_g[l].reshape(1, HEAD_DV), lam_init,
                          batch, seq, n_heads, t)
        xf = _out_proj(attn, conv, w_out, l, xf)

        hm = _rmsnorm(xf, mlp_norm_g[l])
        xf = _down_proj(_up_proj(hm, w_up, l), w_down, l, xf)
    return xf.reshape(batch, seq, d)
```

```python
import functools
import math

import numpy as np
import jax
import jax.numpy as jnp
from jax import lax
from jax.experimental import pallas as pl
from jax.experimental.pallas import tpu as pltpu

HEAD_DV = 128
HEAD_DK = HEAD_DV // 2
CONV_K = 3
NUM_BUCKETS = 32
MAX_DISTANCE = 128
MAX_EXACT = NUM_BUCKETS // 2
EPS = 1e-6
NEG = -1e30

LANES = 128
SUBLANES = 8
VMEM_LIMIT_BYTES = 56 * 1024 * 1024

F32 = jnp.float32
BF16 = jnp.bfloat16


def _tile(dim, pref):
    if dim <= pref:
        return dim
    t = (pref // LANES) * LANES
    while dim % t:
        t -= LANES
    return t


def _params(n_axes):
    return pltpu.CompilerParams(
        dimension_semantics=("arbitrary",) * n_axes,
        vmem_limit_bytes=VMEM_LIMIT_BYTES,
    )


def _rmsnorm_kernel(x_ref, g_ref, o_ref):
    x = x_ref[...]
    ms = jnp.mean(x * x, axis=-1, keepdims=True)
    o_ref[...] = (x * lax.rsqrt(ms + EPS) * g_ref[...]).astype(o_ref.dtype)


def _rmsnorm(x, g):
    m, d = x.shape
    tm = _tile(m, 512)
    return pl.pallas_call(
        _rmsnorm_kernel,
        grid=(m // tm,),
        in_specs=[pl.BlockSpec((tm, d), lambda i: (i, 0)),
                  pl.BlockSpec((1, d), lambda i: (0, 0))],
        out_specs=pl.BlockSpec((tm, d), lambda i: (i, 0)),
        out_shape=jax.ShapeDtypeStruct((m, d), BF16),
        compiler_params=_params(1),
        name="rmsnorm",
    )(x, g.reshape(1, d))


def _cast_weight(w_ref, wbf_ref):
    k = w_ref.shape[0]
    ck = _tile(k, 256)
    for r in range(0, k, ck):
        wbf_ref[r:r + ck, :] = w_ref[r:r + ck, :].astype(BF16)


def _qkv_kernel(h_ref, w_ref, gq_ref, gk_ref, o_ref, wbf_ref):
    j = pl.program_id(0)

    @pl.when(pl.program_id(1) == 0)
    def _():
        _cast_weight(w_ref, wbf_ref)

    acc = jnp.dot(h_ref[...], wbf_ref[...], preferred_element_type=F32)

    @pl.when(j == 2)
    def _():
        o_ref[...] = acc.astype(o_ref.dtype)

    @pl.when(j < 2)
    def _():
        g = jnp.where(j == 0, gq_ref[...], gk_ref[...])
        lo = lax.broadcasted_iota(jnp.int32, (1, LANES), 1) < HEAD_DK
        for c in range(acc.shape[1] // LANES):
            a = acc[:, c * LANES:(c + 1) * LANES]
            sq = a * a
            s_lo = jnp.sum(jnp.where(lo, sq, 0.0), axis=-1, keepdims=True)
            s_hi = jnp.sum(jnp.where(lo, 0.0, sq), axis=-1, keepdims=True)
            ms = jnp.where(lo, s_lo, s_hi) * (1.0 / HEAD_DK)
            o_ref[:, c * LANES:(c + 1) * LANES] = (a * lax.rsqrt(ms + EPS) * g).astype(o_ref.dtype)


def _qkv_proj(h, w_in, layer, gq, gk, aw):
    m, d = h.shape
    bm = _tile(m, 1024)
    return pl.pallas_call(
        _qkv_kernel,
        grid=(3, m // bm),
        in_specs=[pl.BlockSpec((bm, d), lambda j, i: (i, 0)),
                  pl.BlockSpec((None, d, aw), lambda j, i: (layer, 0, j)),
                  pl.BlockSpec((1, LANES), lambda j, i: (0, 0)),
                  pl.BlockSpec((1, LANES), lambda j, i: (0, 0))],
        out_specs=pl.BlockSpec((bm, aw), lambda j, i: (i, j)),
        out_shape=jax.ShapeDtypeStruct((m, 3 * aw), BF16),
        scratch_shapes=[pltpu.VMEM((d, aw), BF16)],
        compiler_params=_params(2),
        name="qkv_proj",
    )(h, w_in, gq, gk)


def _conv_kernel(h_ref, wb_ref, wc_ref, wi_ref, cw_ref, o_ref, wbf_ref, ubuf_ref, *, tiles_per_seq):
    i = pl.program_id(1)
    bm = h_ref.shape[0]

    @pl.when(i == 0)
    def _():
        _cast_weight(wb_ref, wbf_ref.at[0])
        _cast_weight(wc_ref, wbf_ref.at[1])
        _cast_weight(wi_ref, wbf_ref.at[2])

    @pl.when(i % tiles_per_seq == 0)
    def _():
        ubuf_ref[0:SUBLANES, :] = jnp.zeros((SUBLANES, ubuf_ref.shape[1]), F32)

    h = h_ref[...]
    gate_c = jnp.dot(h, wbf_ref[1], preferred_element_type=F32)
    conv_in = jnp.dot(h, wbf_ref[2], preferred_element_type=F32)
    ubuf_ref[SUBLANES:SUBLANES + bm, :] = gate_c * conv_in
    gate_b = jnp.dot(h, wbf_ref[0], preferred_element_type=F32)
    cw = cw_ref[...]
    conv = (cw[0:1] * ubuf_ref[SUBLANES - 2:SUBLANES - 2 + bm, :]
            + cw[1:2] * ubuf_ref[SUBLANES - 1:SUBLANES - 1 + bm, :]
            + cw[2:3] * ubuf_ref[SUBLANES:SUBLANES + bm, :])
    o_ref[...] = (gate_b * conv).astype(o_ref.dtype)
    ubuf_ref[0:SUBLANES, :] = ubuf_ref[bm:bm + SUBLANES, :]


def _conv_proj(h, w_in, conv_w, layer, aw, cwid, seq):
    m, d = h.shape
    bm = _tile(seq, 1024)
    bn = _tile(cwid, 512)
    off = 3 * aw // bn
    nb = cwid // bn
    return pl.pallas_call(
        functools.partial(_conv_kernel, tiles_per_seq=seq // bm),
        grid=(nb, m // bm),
        in_specs=[pl.BlockSpec((bm, d), lambda j, i: (i, 0)),
                  pl.BlockSpec((None, d, bn), lambda j, i: (layer, 0, off + j)),
                  pl.BlockSpec((None, d, bn), lambda j, i: (layer, 0, off + nb + j)),
                  pl.BlockSpec((None, d, bn), lambda j, i: (layer, 0, off + 2 * nb + j)),
                  pl.BlockSpec((None, CONV_K, bn), lambda j, i: (layer, 0, j))],
        out_specs=pl.BlockSpec((bm, bn), lambda j, i: (i, j)),
        out_shape=jax.ShapeDtypeStruct((m, cwid), BF16),
        scratch_shapes=[pltpu.VMEM((3, d, bn), BF16),
                        pltpu.VMEM((bm + SUBLANES, bn), F32)],
        compiler_params=_params(2),
        name="conv_proj",
    )(h, w_in, w_in, w_in, conv_w)


def _bucket(n):
    nf = np.maximum(n, MAX_EXACT).astype(np.float64)
    val = np.log(nf / MAX_EXACT) / math.log(MAX_DISTANCE / MAX_EXACT) * (NUM_BUCKETS - MAX_EXACT)
    large = np.minimum(MAX_EXACT + val.astype(np.int64), NUM_BUCKETS - 1)
    return np.where(n < MAX_EXACT, n, large)


def _bucket_tables(t):
    assert _bucket(np.array([t + 1]))[0] == NUM_BUCKETS - 1
    r = np.arange(t)[:, None]
    c = np.arange(t)[None, :]
    d0 = r - c
    diag = np.where(d0 >= 0, _bucket(np.maximum(d0, 0)), -1)
    prev = _bucket(t + r - c)
    return np.stack([diag, prev]).astype(np.int32)


def _bias_table_kernel(rb_ref, bk_ref, o_ref):
    h = pl.program_id(0)
    for mode in range(2):
        bk = bk_ref[mode]
        for mp in range(2):
            col = 2 * h + mp
            far = rb_ref[NUM_BUCKETS - 1, col]
            acc = jnp.where(bk < 0, NEG, 0.0).astype(F32)
            for b in range(NUM_BUCKETS - 1):
                acc = jnp.where(bk == b, rb_ref[b, col] - far, acc)
            o_ref[0, mode, mp] = acc


def _bias_tables(rel_bias, n_heads, t):
    buckets = jnp.asarray(_bucket_tables(t))
    return pl.pallas_call(
        _bias_table_kernel,
        grid=(n_heads,),
        in_specs=[pl.BlockSpec(memory_space=pltpu.SMEM),
                  pl.BlockSpec((2, t, t), lambda h: (0, 0, 0))],
        out_specs=pl.BlockSpec((1, 2, 2, t, t), lambda h: (h, 0, 0, 0, 0)),
        out_shape=jax.ShapeDtypeStruct((n_heads, 2, 2, t, t), F32),
        compiler_params=_params(1),
        name="bias_tables",
    )(rel_bias, buckets)


def _attn_kernel(q_ref, k_ref, v_ref, tb_ref, lam_ref, g_ref, o_ref, *, lam_init, t):
    seq = q_ref.shape[0]
    lo = lax.broadcasted_iota(jnp.int32, (1, LANES), 1) < HEAD_DK
    lv = lam_ref[...]
    lam = (jnp.exp(jnp.sum(lv[0:1] * lv[1:2], axis=-1, keepdims=True))
           - jnp.exp(jnp.sum(lv[2:3] * lv[3:4], axis=-1, keepdims=True)) + lam_init)
    g = g_ref[...]
    contract_last = (((1,), (1,)), ((), ()))

    for i in range(seq // t):
        q = q_ref[i * t:(i + 1) * t, :]
        zero = jnp.zeros_like(q)
        q2 = jnp.concatenate([jnp.where(lo, q, zero), jnp.where(lo, zero, q)], axis=0)

        pieces = []
        if i >= 2:
            pieces.append((0, (i - 1) * t, None))
        if i >= 1:
            pieces.append(((i - 1) * t, t, 1))
        pieces.append((i * t, t, 0))

        scores = []
        for start, width, table in pieces:
            s = lax.dot_general(q2, k_ref[start:start + width, :], contract_last,
                                preferred_element_type=F32)
            if table is not None:
                s = s + tb_ref[0, table].reshape(2 * t, t)
            scores.append(s)
        m = functools.reduce(jnp.maximum, [jnp.max(s, axis=-1, keepdims=True) for s in scores])
        l = None
        acc = None
        for s, (start, width, _) in zip(scores, pieces):
            p = jnp.exp(s - m)
            lsum = jnp.sum(p, axis=-1, keepdims=True)
            pv = jnp.dot(p.astype(BF16), v_ref[start:start + width, :], preferred_element_type=F32)
            l = lsum if l is None else l + lsum
            acc = pv if acc is None else acc + pv
        out = acc / l
        o = out[0:t] - lam * out[t:2 * t]
        ms = jnp.mean(o * o, axis=-1, keepdims=True)
        o_ref[i * t:(i + 1) * t, :] = (o * lax.rsqrt(ms + EPS) * g * (1.0 - lam_init)).astype(o_ref.dtype)


def _attention(qkv, tables, lam_vecs, subln_g, lam_init, batch, seq, n_heads, t):
    return pl.pallas_call(
        functools.partial(_attn_kernel, lam_init=lam_init, t=t),
        grid=(batch, n_heads),
        in_specs=[pl.BlockSpec((seq, HEAD_DV), lambda b, h: (b, h)),
                  pl.BlockSpec((seq, HEAD_DV), lambda b, h: (b, n_heads + h)),
                  pl.BlockSpec((seq, HEAD_DV), lambda b, h: (b, 2 * n_heads + h)),
                  pl.BlockSpec((1, 2, 2, t, t), lambda b, h: (h, 0, 0, 0, 0)),
                  pl.BlockSpec((4, HEAD_DK), lambda b, h: (0, 0)),
                  pl.BlockSpec((1, HEAD_DV), lambda b, h: (0, 0))],
        out_specs=pl.BlockSpec((seq, HEAD_DV), lambda b, h: (b, h)),
        out_shape=jax.ShapeDtypeStruct((batch * seq, n_heads * HEAD_DV), BF16),
        compiler_params=_params(2),
        name="diff_attention",
    )(qkv, qkv, qkv, tables, lam_vecs, subln_g)


def _out_kernel(a_ref, c_ref, w_ref, x_ref, o_ref, wbf_ref):
    @pl.when(pl.program_id(1) == 0)
    def _():
        _cast_weight(w_ref, wbf_ref)

    aw = a_ref.shape[1]
    acc = jnp.dot(a_ref[...], wbf_ref[0:aw, :], preferred_element_type=F32)
    acc = acc + jnp.dot(c_ref[...], wbf_ref[aw:, :], preferred_element_type=F32)
    o_ref[...] = x_ref[...] + acc


def _out_proj(attn, conv, w_out, layer, x):
    m, d = x.shape
    aw, cwid = attn.shape[1], conv.shape[1]
    bm = _tile(m, 512)
    bn = _tile(d, 1024)
    return pl.pallas_call(
        _out_kernel,
        grid=(d // bn, m // bm),
        in_specs=[pl.BlockSpec((bm, aw), lambda j, i: (i, 0)),
                  pl.BlockSpec((bm, cwid), lambda j, i: (i, 0)),
                  pl.BlockSpec((None, aw + cwid, bn), lambda j, i: (layer, 0, j)),
                  pl.BlockSpec((bm, bn), lambda j, i: (i, j))],
        out_specs=pl.BlockSpec((bm, bn), lambda j, i: (i, j)),
        out_shape=jax.ShapeDtypeStruct((m, d), F32),
        scratch_shapes=[pltpu.VMEM((aw + cwid, bn), BF16)],
        compiler_params=_params(2),
        name="out_proj",
    )(attn, conv, w_out, x)


def _up_kernel(h_ref, w_ref, o_ref, wbf_ref):
    @pl.when(pl.program_id(1) == 0)
    def _():
        _cast_weight(w_ref, wbf_ref)

    acc = jnp.dot(h_ref[...], wbf_ref[...], preferred_element_type=F32)
    o_ref[...] = jnp.square(jnp.maximum(acc, 0.0)).astype(o_ref.dtype)


def _up_proj(h, w_up, layer):
    m, d = h.shape
    f = w_up.shape[2]
    bm = _tile(m, 1024)
    bn = _tile(f, 1024)
    return pl.pallas_call(
        _up_kernel,
        grid=(f // bn, m // bm),
        in_specs=[pl.BlockSpec((bm, d), lambda j, i: (i, 0)),
                  pl.BlockSpec((None, d, bn), lambda j, i: (layer, 0, j))],
        out_specs=pl.BlockSpec((bm, bn), lambda j, i: (i, j)),
        out_shape=jax.ShapeDtypeStruct((m, f), BF16),
        scratch_shapes=[pltpu.VMEM((d, bn), BF16)],
        compiler_params=_params(2),
        name="up_proj",
    )(h, w_up)


def _down_kernel(a_ref, w_ref, x_ref, o_ref, acc_ref):
    kk = pl.program_id(2)
    part = jnp.dot(a_ref[...], w_ref[...].astype(BF16), preferred_element_type=F32)

    @pl.when(kk == 0)
    def _():
        acc_ref[...] = part

    @pl.when(kk > 0)
    def _():
        acc_ref[...] += part

    @pl.when(kk == pl.num_programs(2) - 1)
    def _():
        o_ref[...] = x_ref[...] + acc_ref[...]


def _down_proj(a, w_down, layer, x):
    m, d = x.shape
    f = a.shape[1]
    bm = _tile(m, 1024)
    bn = _tile(d, 1024)
    bk = _tile(f, 2048)
    return pl.pallas_call(
        _down_kernel,
        grid=(d // bn, m // bm, f // bk),
        in_specs=[pl.BlockSpec((bm, bk), lambda j, i, k: (i, k)),
                  pl.BlockSpec((None, bk, bn), lambda j, i, k: (layer, k, j)),
                  pl.BlockSpec((bm, bn), lambda j, i, k: (i, j))],
        out_specs=pl.BlockSpec((bm, bn), lambda j, i, k: (i, j)),
        out_shape=jax.ShapeDtypeStruct((m, d), F32),
        scratch_shapes=[pltpu.VMEM((bm, bn), F32)],
        compiler_params=_params(3),
        name="down_proj",
    )(a, w_down, x)


def kernel(x, w_in, w_out, conv_w, q_norm_g, k_norm_g, lambda_q1, lambda_k1, lambda_q2, lambda_k2,
           subln_g, attn_norm_g, mlp_norm_g, w_up, w_down, rel_bias):
    batch, seq, d = x.shape
    depth = w_in.shape[0]
    aw = d // 2
    cwid = d - aw
    n_heads = aw // HEAD_DV
    assert w_in.shape[2] == 3 * aw + 3 * cwid and rel_bias.shape == (NUM_BUCKETS, 2 * n_heads)
    t = _tile(seq, 256)

    tables = _bias_tables(rel_bias, n_heads, t)
    scale = HEAD_DK ** -0.5
    xf = x.reshape(batch * seq, d)
    for l in range(depth):
        lam_init = 0.8 - 0.6 * math.exp(-0.3 * l)
        gq = jnp.tile(q_norm_g[l] * scale, 2).reshape(1, LANES)
        gk = jnp.tile(k_norm_g[l], 2).reshape(1, LANES)
        lam_vecs = jnp.stack([lambda_q1[l], lambda_k1[l], lambda_q2[l], lambda_k2[l]])

        h = _rmsnorm(xf, attn_norm_g[l])
        qkv = _qkv_proj(h, w_in, l, gq, gk, aw)
        conv = _conv_proj(h, w_in, conv_w, l, aw, cwid, seq)
        attn = _attention(qkv, tables, lam_vecs, subln_g[l].reshape(1, HEAD_DV), lam_init,
                          batch, seq, n_heads, t)
        xf = _out_proj(attn, conv, w_out, l, xf)

        hm = _rmsnorm(xf, mlp_norm_g[l])
        xf = _down_proj(_up_proj(hm, w_up, l), w_down, l, xf)
    return xf.reshape(batch, seq, d)
```

```python
import functools
import math

import numpy as np
import jax
import jax.numpy as jnp
from jax import lax
from jax.experimental import pallas as pl
from jax.experimental.pallas import tpu as pltpu

HEAD_DV = 128
HEAD_DK = HEAD_DV // 2
CONV_K = 3
NUM_BUCKETS = 32
MAX_DISTANCE = 128
MAX_EXACT = NUM_BUCKETS // 2
EPS = 1e-6
NEG = -1e30
LOG2E = math.log2(math.e)

LANES = 128
SUBLANES = 8
VMEM_LIMIT_BYTES = 56 * 1024 * 1024

F32 = jnp.float32
BF16 = jnp.bfloat16


def _tile(dim, pref):
    if dim <= pref:
        return dim
    t = (pref // LANES) * LANES
    while dim % t:
        t -= LANES
    return t


def _params(n_axes):
    return pltpu.CompilerParams(
        dimension_semantics=("arbitrary",) * n_axes,
        vmem_limit_bytes=VMEM_LIMIT_BYTES,
    )


def _rmsnorm_kernel(x_ref, g_ref, o_ref):
    x = x_ref[...]
    ms = jnp.mean(x * x, axis=-1, keepdims=True)
    o_ref[...] = (x * lax.rsqrt(ms + EPS) * g_ref[...]).astype(o_ref.dtype)


def _rmsnorm(x, g):
    m, d = x.shape
    tm = _tile(m, 512)
    return pl.pallas_call(
        _rmsnorm_kernel,
        grid=(m // tm,),
        in_specs=[pl.BlockSpec((tm, d), lambda i: (i, 0)),
                  pl.BlockSpec((1, d), lambda i: (0, 0))],
        out_specs=pl.BlockSpec((tm, d), lambda i: (i, 0)),
        out_shape=jax.ShapeDtypeStruct((m, d), BF16),
        compiler_params=_params(1),
        name="rmsnorm",
    )(x, g.reshape(1, d))


def _cast_weight(w_ref, wbf_ref):
    k = w_ref.shape[0]
    ck = _tile(k, 256)
    for r in range(0, k, ck):
        wbf_ref[r:r + ck, :] = w_ref[r:r + ck, :].astype(BF16)


def _segment_ones(width):
    seg = np.arange(width) // HEAD_DK
    return jnp.asarray(seg[:, None] == seg[None, :], dtype=BF16)


def _qk_kernel(h_ref, w_ref, g_ref, e_ref, o_ref, wbf_ref):
    @pl.when(pl.program_id(1) == 0)
    def _():
        _cast_weight(w_ref, wbf_ref)

    acc = jnp.dot(h_ref[...], wbf_ref[...], preferred_element_type=F32)
    g = g_ref[...]
    gw = e_ref.shape[0]
    for c in range(0, acc.shape[1], gw):
        a = acc[:, c:c + gw]
        ssq = jnp.dot((a * a).astype(BF16), e_ref[...], preferred_element_type=F32)
        inv = lax.rsqrt(ssq * (1.0 / HEAD_DK) + EPS)
        for cc in range(0, gw, LANES):
            o_ref[:, c + cc:c + cc + LANES] = (
                a[:, cc:cc + LANES] * inv[:, cc:cc + LANES] * g).astype(o_ref.dtype)


def _qk_proj(h, w_in, layer, gains, aw):
    m, d = h.shape
    bm = _tile(m, 1024)
    gw = _tile(aw, 2 * LANES)
    return pl.pallas_call(
        _qk_kernel,
        grid=(2, m // bm),
        in_specs=[pl.BlockSpec((bm, d), lambda j, i: (i, 0)),
                  pl.BlockSpec((None, d, aw), lambda j, i: (layer, 0, j)),
                  pl.BlockSpec((None, 1, LANES), lambda j, i: (j, 0, 0)),
                  pl.BlockSpec((gw, gw), lambda j, i: (0, 0))],
        out_specs=pl.BlockSpec((bm, aw), lambda j, i: (i, j)),
        out_shape=jax.ShapeDtypeStruct((m, 2 * aw), BF16),
        scratch_shapes=[pltpu.VMEM((d, aw), BF16)],
        compiler_params=_params(2),
        name="qk_proj",
    )(h, w_in, gains, _segment_ones(gw))


def _dense_kernel(h_ref, w_ref, o_ref, wbf_ref, *, relu_sq):
    @pl.when(pl.program_id(1) == 0)
    def _():
        _cast_weight(w_ref, wbf_ref)

    acc = jnp.dot(h_ref[...], wbf_ref[...], preferred_element_type=F32)
    if relu_sq:
        acc = jnp.square(jnp.maximum(acc, 0.0))
    o_ref[...] = acc.astype(o_ref.dtype)


def _dense(h, w, layer, col0, ncols, relu_sq, name):
    m, d = h.shape
    bm = _tile(m, 1024)
    bn = _tile(ncols, 1024)
    assert col0 % bn == 0
    return pl.pallas_call(
        functools.partial(_dense_kernel, relu_sq=relu_sq),
        grid=(ncols // bn, m // bm),
        in_specs=[pl.BlockSpec((bm, d), lambda j, i: (i, 0)),
                  pl.BlockSpec((None, d, bn), lambda j, i: (layer, 0, col0 // bn + j))],
        out_specs=pl.BlockSpec((bm, bn), lambda j, i: (i, j)),
        out_shape=jax.ShapeDtypeStruct((m, ncols), BF16),
        scratch_shapes=[pltpu.VMEM((d, bn), BF16)],
        compiler_params=_params(2),
        name=name,
    )(h, w)


def _conv_kernel(h_ref, wb_ref, wc_ref, wi_ref, cw_ref, o_ref, wbf_ref, ubuf_ref, *, tiles_per_seq):
    i = pl.program_id(1)
    bm = h_ref.shape[0]

    @pl.when(i == 0)
    def _():
        _cast_weight(wb_ref, wbf_ref.at[0])
        _cast_weight(wc_ref, wbf_ref.at[1])
        _cast_weight(wi_ref, wbf_ref.at[2])

    @pl.when(i % tiles_per_seq == 0)
    def _():
        ubuf_ref[0:SUBLANES, :] = jnp.zeros((SUBLANES, ubuf_ref.shape[1]), F32)

    h = h_ref[...]
    gate_c = jnp.dot(h, wbf_ref[1], preferred_element_type=F32)
    conv_in = jnp.dot(h, wbf_ref[2], preferred_element_type=F32)
    ubuf_ref[SUBLANES:SUBLANES + bm, :] = gate_c * conv_in
    gate_b = jnp.dot(h, wbf_ref[0], preferred_element_type=F32)
    cw = cw_ref[...]
    conv = (cw[0:1] * ubuf_ref[SUBLANES - 2:SUBLANES - 2 + bm, :]
            + cw[1:2] * ubuf_ref[SUBLANES - 1:SUBLANES - 1 + bm, :]
            + cw[2:3] * ubuf_ref[SUBLANES:SUBLANES + bm, :])
    o_ref[...] = (gate_b * conv).astype(o_ref.dtype)
    ubuf_ref[0:SUBLANES, :] = ubuf_ref[bm:bm + SUBLANES, :]


def _conv_proj(h, w_in, conv_w, layer, aw, cwid, seq):
    m, d = h.shape
    bm = _tile(seq, 1024)
    bn = _tile(cwid, 512)
    off = 3 * aw // bn
    nb = cwid // bn
    return pl.pallas_call(
        functools.partial(_conv_kernel, tiles_per_seq=seq // bm),
        grid=(nb, m // bm),
        in_specs=[pl.BlockSpec((bm, d), lambda j, i: (i, 0)),
                  pl.BlockSpec((None, d, bn), lambda j, i: (layer, 0, off + j)),
                  pl.BlockSpec((None, d, bn), lambda j, i: (layer, 0, off + nb + j)),
                  pl.BlockSpec((None, d, bn), lambda j, i: (layer, 0, off + 2 * nb + j)),
                  pl.BlockSpec((None, CONV_K, bn), lambda j, i: (layer, 0, j))],
        out_specs=pl.BlockSpec((bm, bn), lambda j, i: (i, j)),
        out_shape=jax.ShapeDtypeStruct((m, cwid), BF16),
        scratch_shapes=[pltpu.VMEM((3, d, bn), BF16),
                        pltpu.VMEM((bm + SUBLANES, bn), F32)],
        compiler_params=_params(2),
        name="conv_proj",
    )(h, w_in, w_in, w_in, conv_w)


def _bucket(n):
    nf = np.maximum(n, MAX_EXACT).astype(np.float64)
    val = np.log(nf / MAX_EXACT) / math.log(MAX_DISTANCE / MAX_EXACT) * (NUM_BUCKETS - MAX_EXACT)
    large = np.minimum(MAX_EXACT + val.astype(np.int64), NUM_BUCKETS - 1)
    return np.where(n < MAX_EXACT, n, large)


def _bucket_tables(t):
    assert _bucket(np.array([t + 1]))[0] == NUM_BUCKETS - 1
    r = np.arange(t)[:, None]
    c = np.arange(t)[None, :]
    d0 = r - c
    diag = np.where(d0 >= 0, _bucket(np.maximum(d0, 0)), -1)
    prev = _bucket(t + r - c)
    return np.stack([diag, prev]).astype(np.int32)


def _bias_table_kernel(rb_ref, bk_ref, o_ref):
    h = pl.program_id(0)
    for mode in range(2):
        bk = bk_ref[mode]
        for mp in range(2):
            col = 2 * h + mp
            far = rb_ref[NUM_BUCKETS - 1, col]
            acc = jnp.where(bk < 0, NEG, 0.0).astype(F32)
            for b in range(NUM_BUCKETS - 1):
                acc = jnp.where(bk == b, (rb_ref[b, col] - far) * LOG2E, acc)
            o_ref[0, mode, mp] = acc


def _bias_tables(rel_bias, n_heads, t):
    buckets = jnp.asarray(_bucket_tables(t))
    return pl.pallas_call(
        _bias_table_kernel,
        grid=(n_heads,),
        in_specs=[pl.BlockSpec(memory_space=pltpu.SMEM),
                  pl.BlockSpec((2, t, t), lambda h: (0, 0, 0))],
        out_specs=pl.BlockSpec((1, 2, 2, t, t), lambda h: (h, 0, 0, 0, 0)),
        out_shape=jax.ShapeDtypeStruct((n_heads, 2, 2, t, t), F32),
        compiler_params=_params(1),
        name="bias_tables",
    )(rel_bias, buckets)


def _fold_lanes(x, op):
    return functools.reduce(op, [x[:, c:c + LANES] for c in range(0, x.shape[1], LANES)])


def _attn_kernel(q_ref, k_ref, v_ref, tb_ref, lam_ref, g_ref, o_ref, v1_ref, *, lam_init, t):
    seq = q_ref.shape[0]
    v1_ref[:, 0:HEAD_DV] = v_ref[...]
    v1_ref[:, HEAD_DV:] = jnp.ones((seq, LANES), BF16)
    lo = lax.broadcasted_iota(jnp.int32, (1, LANES), 1) < HEAD_DK
    lv = lam_ref[...]
    lam = (jnp.exp(jnp.sum(lv[0:1] * lv[1:2], axis=-1, keepdims=True))
           - jnp.exp(jnp.sum(lv[2:3] * lv[3:4], axis=-1, keepdims=True)) + lam_init)
    g = g_ref[...]
    contract_last = (((1,), (1,)), ((), ()))

    for i in range(seq // t):
        q = q_ref[i * t:(i + 1) * t, :]
        zero = jnp.zeros_like(q)
        q2 = jnp.concatenate([jnp.where(lo, q, zero), jnp.where(lo, zero, q)], axis=0)

        pieces = []
        if i >= 2:
            pieces.append((0, (i - 1) * t, None))
        if i >= 1:
            pieces.append(((i - 1) * t, t, 1))
        pieces.append((i * t, t, 0))

        scores = []
        for start, width, table in pieces:
            s = lax.dot_general(q2, k_ref[start:start + width, :], contract_last,
                                preferred_element_type=F32)
            if table is not None:
                s = s + tb_ref[0, table].reshape(2 * t, t)
            scores.append(s)
        m = functools.reduce(jnp.maximum, [_fold_lanes(s, jnp.maximum) for s in scores])
        m = jnp.max(m, axis=-1, keepdims=True)
        acc = None
        for s, (start, width, _) in zip(scores, pieces):
            p = jnp.exp2(s - m).astype(BF16)
            pv = jnp.dot(p, v1_ref[start:start + width, :], preferred_element_type=F32)
            acc = pv if acc is None else acc + pv
        out = acc[:, 0:HEAD_DV] / acc[:, HEAD_DV:]
        o = out[0:t] - lam * out[t:2 * t]
        ms = jnp.mean(o * o, axis=-1, keepdims=True)
        o_ref[i * t:(i + 1) * t, :] = (o * lax.rsqrt(ms + EPS) * g * (1.0 - lam_init)).astype(o_ref.dtype)


def _attention(qk, v, tables, lam_vecs, subln_g, lam_init, batch, seq, n_heads, t):
    return pl.pallas_call(
        functools.partial(_attn_kernel, lam_init=lam_init, t=t),
        grid=(batch, n_heads),
        in_specs=[pl.BlockSpec((seq, HEAD_DV), lambda b, h: (b, h)),
                  pl.BlockSpec((seq, HEAD_DV), lambda b, h: (b, n_heads + h)),
                  pl.BlockSpec((seq, HEAD_DV), lambda b, h: (b, h)),
                  pl.BlockSpec((1, 2, 2, t, t), lambda b, h: (h, 0, 0, 0, 0)),
                  pl.BlockSpec((4, HEAD_DK), lambda b, h: (0, 0)),
                  pl.BlockSpec((1, HEAD_DV), lambda b, h: (0, 0))],
        out_specs=pl.BlockSpec((seq, HEAD_DV), lambda b, h: (b, h)),
        out_shape=jax.ShapeDtypeStruct((batch * seq, n_heads * HEAD_DV), BF16),
        scratch_shapes=[pltpu.VMEM((seq, HEAD_DV + LANES), BF16)],
        compiler_params=_params(2),
        name="diff_attention",
    )(qk, qk, v, tables, lam_vecs, subln_g)


def _out_kernel(a_ref, c_ref, w_ref, x_ref, o_ref, wbf_ref):
    @pl.when(pl.program_id(1) == 0)
    def _():
        _cast_weight(w_ref, wbf_ref)

    aw = a_ref.shape[1]
    acc = jnp.dot(a_ref[...], wbf_ref[0:aw, :], preferred_element_type=F32)
    acc = acc + jnp.dot(c_ref[...], wbf_ref[aw:, :], preferred_element_type=F32)
    o_ref[...] = x_ref[...] + acc


def _out_proj(attn, conv, w_out, layer, x):
    m, d = x.shape
    aw, cwid = attn.shape[1], conv.shape[1]
    bm = _tile(m, 512)
    bn = _tile(d, 1024)
    return pl.pallas_call(
        _out_kernel,
        grid=(d // bn, m // bm),
        in_specs=[pl.BlockSpec((bm, aw), lambda j, i: (i, 0)),
                  pl.BlockSpec((bm, cwid), lambda j, i: (i, 0)),
                  pl.BlockSpec((None, aw + cwid, bn), lambda j, i: (layer, 0, j)),
                  pl.BlockSpec((bm, bn), lambda j, i: (i, j))],
        out_specs=pl.BlockSpec((bm, bn), lambda j, i: (i, j)),
        out_shape=jax.ShapeDtypeStruct((m, d), F32),
        scratch_shapes=[pltpu.VMEM((aw + cwid, bn), BF16)],
        compiler_params=_params(2),
        name="out_proj",
    )(attn, conv, w_out, x)


def _down_kernel(a_ref, w_ref, x_ref, o_ref, wbf_ref):
    @pl.when(pl.program_id(2) == 0)
    def _():
        o_ref[...] = x_ref[...]

    _cast_weight(w_ref, wbf_ref)
    o_ref[...] += jnp.dot(a_ref[...], wbf_ref[...], preferred_element_type=F32)


def _down_proj(a, w_down, layer, x):
    m, d = x.shape
    f = a.shape[1]
    bm = _tile(m, 1024)
    bn = _tile(d, 1024)
    bk = _tile(f, 2048)
    return pl.pallas_call(
        _down_kernel,
        grid=(d // bn, m // bm, f // bk),
        in_specs=[pl.BlockSpec((bm, bk), lambda j, i, k: (i, k)),
                  pl.BlockSpec((None, bk, bn), lambda j, i, k: (layer, k, j)),
                  pl.BlockSpec((bm, bn), lambda j, i, k: (i, j))],
        out_specs=pl.BlockSpec((bm, bn), lambda j, i, k: (i, j)),
        out_shape=jax.ShapeDtypeStruct((m, d), F32),
        scratch_shapes=[pltpu.VMEM((bk, bn), BF16)],
        compiler_params=_params(3),
        name="down_proj",
    )(a, w_down, x)


def kernel(x, w_in, w_out, conv_w, q_norm_g, k_norm_g, lambda_q1, lambda_k1, lambda_q2, lambda_k2,
           subln_g, attn_norm_g, mlp_norm_g, w_up, w_down, rel_bias):
    batch, seq, d = x.shape
    depth = w_in.shape[0]
    aw = d // 2
    cwid = d - aw
    n_heads = aw // HEAD_DV
    assert w_in.shape[2] == 3 * aw + 3 * cwid and rel_bias.shape == (NUM_BUCKETS, 2 * n_heads)
    t = _tile(seq, 256)

    tables = _bias_tables(rel_bias, n_heads, t)
    scale = HEAD_DK ** -0.5 * LOG2E
    xf = x.reshape(batch * seq, d)
    for l in range(depth):
        lam_init = 0.8 - 0.6 * math.exp(-0.3 * l)
        qk_gains = jnp.stack([jnp.tile(q_norm_g[l] * scale, 2), jnp.tile(k_norm_g[l], 2)])
        lam_vecs = jnp.stack([lambda_q1[l], lambda_k1[l], lambda_q2[l], lambda_k2[l]])

        h = _rmsnorm(xf, attn_norm_g[l])
        qk = _qk_proj(h, w_in, l, qk_gains.reshape(2, 1, LANES), aw)
        v = _dense(h, w_in, l, 2 * aw, aw, False, "v_proj")
        conv = _conv_proj(h, w_in, conv_w, l, aw, cwid, seq)
        attn = _attention(qk, v, tables, lam_vecs, subln_g[l].reshape(1, HEAD_DV), lam_init,
                          batch, seq, n_heads, t)
        xf = _out_proj(attn, conv, w_out, l, xf)

        hm = _rmsnorm(xf, mlp_norm_g[l])
        act = _dense(hm, w_up, l, 0, w_up.shape[2], True, "up_proj")
        xf = _down_proj(act, w_down, l, xf)
    return xf.reshape(batch, seq, d)
```

```python
import functools
import math

import numpy as np
import jax
import jax.numpy as jnp
from jax import lax
from jax.experimental import pallas as pl
from jax.experimental.pallas import tpu as pltpu

HEAD_DV = 128
HEAD_DK = HEAD_DV // 2
CONV_K = 3
NUM_BUCKETS = 32
MAX_DISTANCE = 128
MAX_EXACT = NUM_BUCKETS // 2
EPS = 1e-6
NEG = -1e30
LOG2E = math.log2(math.e)

LANES = 128
SUBLANES = 8
VMEM_LIMIT_BYTES = 56 * 1024 * 1024

F32 = jnp.float32
BF16 = jnp.bfloat16


def _tile(dim, pref):
    if dim <= pref:
        return dim
    t = (pref // LANES) * LANES
    while dim % t:
        t -= LANES
    return t


def _params(n_axes):
    return pltpu.CompilerParams(
        dimension_semantics=("arbitrary",) * n_axes,
        vmem_limit_bytes=VMEM_LIMIT_BYTES,
    )


def _rmsnorm_kernel(x_ref, g_ref, o_ref):
    x = x_ref[...]
    ms = jnp.mean(x * x, axis=-1, keepdims=True)
    o_ref[...] = (x * lax.rsqrt(ms + EPS) * g_ref[...]).astype(o_ref.dtype)


def _rmsnorm(x, g):
    m, d = x.shape
    tm = _tile(m, 512)
    return pl.pallas_call(
        _rmsnorm_kernel,
        grid=(m // tm,),
        in_specs=[pl.BlockSpec((tm, d), lambda i: (i, 0)),
                  pl.BlockSpec((1, d), lambda i: (0, 0))],
        out_specs=pl.BlockSpec((tm, d), lambda i: (i, 0)),
        out_shape=jax.ShapeDtypeStruct((m, d), BF16),
        compiler_params=_params(1),
        name="rmsnorm",
    )(x, g.reshape(1, d))


def _cast_weight(w_ref, wbf_ref):
    k = w_ref.shape[0]
    ck = _tile(k, 256)
    for r in range(0, k, ck):
        wbf_ref[r:r + ck, :] = w_ref[r:r + ck, :].astype(BF16)


def _segment_ones(width):
    seg = np.arange(width) // HEAD_DK
    return jnp.asarray(seg[:, None] == seg[None, :], dtype=BF16)


def _qk_kernel(h_ref, w_ref, g_ref, e_ref, o_ref, wbf_ref):
    @pl.when(pl.program_id(1) == 0)
    def _():
        _cast_weight(w_ref, wbf_ref)

    acc = jnp.dot(h_ref[...], wbf_ref[...], preferred_element_type=F32)
    g = g_ref[...]
    gw = e_ref.shape[0]
    for c in range(0, acc.shape[1], gw):
        a = acc[:, c:c + gw]
        ssq = jnp.dot((a * a).astype(BF16), e_ref[...], preferred_element_type=F32)
        inv = lax.rsqrt(ssq * (1.0 / HEAD_DK) + EPS)
        for cc in range(0, gw, LANES):
            o_ref[:, c + cc:c + cc + LANES] = (
                a[:, cc:cc + LANES] * inv[:, cc:cc + LANES] * g).astype(o_ref.dtype)


def _qk_proj(h, w_in, layer, gains, aw):
    m, d = h.shape
    bm = _tile(m, 1024)
    gw = _tile(aw, 2 * LANES)
    return pl.pallas_call(
        _qk_kernel,
        grid=(2, m // bm),
        in_specs=[pl.BlockSpec((bm, d), lambda j, i: (i, 0)),
                  pl.BlockSpec((None, d, aw), lambda j, i: (layer, 0, j)),
                  pl.BlockSpec((None, 1, LANES), lambda j, i: (j, 0, 0)),
                  pl.BlockSpec((gw, gw), lambda j, i: (0, 0))],
        out_specs=pl.BlockSpec((bm, aw), lambda j, i: (i, j)),
        out_shape=jax.ShapeDtypeStruct((m, 2 * aw), BF16),
        scratch_shapes=[pltpu.VMEM((d, aw), BF16)],
        compiler_params=_params(2),
        name="qk_proj",
    )(h, w_in, gains, _segment_ones(gw))


def _dense_kernel(h_ref, w_ref, o_ref, wbf_ref, *, relu_sq):
    @pl.when(pl.program_id(1) == 0)
    def _():
        _cast_weight(w_ref, wbf_ref)

    acc = jnp.dot(h_ref[...], wbf_ref[...], preferred_element_type=F32)
    if relu_sq:
        acc = jnp.square(jnp.maximum(acc, 0.0))
    o_ref[...] = acc.astype(o_ref.dtype)


def _dense(h, w, layer, col0, ncols, relu_sq, name):
    m, d = h.shape
    bm = _tile(m, 1024)
    bn = _tile(ncols, 1024)
    assert col0 % bn == 0
    return pl.pallas_call(
        functools.partial(_dense_kernel, relu_sq=relu_sq),
        grid=(ncols // bn, m // bm),
        in_specs=[pl.BlockSpec((bm, d), lambda j, i: (i, 0)),
                  pl.BlockSpec((None, d, bn), lambda j, i: (layer, 0, col0 // bn + j))],
        out_specs=pl.BlockSpec((bm, bn), lambda j, i: (i, j)),
        out_shape=jax.ShapeDtypeStruct((m, ncols), BF16),
        scratch_shapes=[pltpu.VMEM((d, bn), BF16)],
        compiler_params=_params(2),
        name=name,
    )(h, w)


def _conv_kernel(h_ref, wb_ref, wc_ref, wi_ref, cw_ref, o_ref, wbf_ref, ubuf_ref, *, tiles_per_seq):
    i = pl.program_id(1)
    bm = h_ref.shape[0]

    @pl.when(i == 0)
    def _():
        _cast_weight(wb_ref, wbf_ref.at[0])
        _cast_weight(wc_ref, wbf_ref.at[1])
        _cast_weight(wi_ref, wbf_ref.at[2])

    @pl.when(i % tiles_per_seq == 0)
    def _():
        ubuf_ref[0:SUBLANES, :] = jnp.zeros((SUBLANES, ubuf_ref.shape[1]), F32)

    h = h_ref[...]
    gate_c = jnp.dot(h, wbf_ref[1], preferred_element_type=F32)
    conv_in = jnp.dot(h, wbf_ref[2], preferred_element_type=F32)
    ubuf_ref[SUBLANES:SUBLANES + bm, :] = gate_c * conv_in
    gate_b = jnp.dot(h, wbf_ref[0], preferred_element_type=F32)
    cw = cw_ref[...]
    conv = (cw[0:1] * ubuf_ref[SUBLANES - 2:SUBLANES - 2 + bm, :]
            + cw[1:2] * ubuf_ref[SUBLANES - 1:SUBLANES - 1 + bm, :]
            + cw[2:3] * ubuf_ref[SUBLANES:SUBLANES + bm, :])
    o_ref[...] = (gate_b * conv).astype(o_ref.dtype)
    ubuf_ref[0:SUBLANES, :] = ubuf_ref[bm:bm + SUBLANES, :]


def _conv_proj(h, w_in, conv_w, layer, aw, cwid, seq):
    m, d = h.shape
    bm = _tile(seq, 1024)
    bn = _tile(cwid, 512)
    off = 3 * aw // bn
    nb = cwid // bn
    return pl.pallas_call(
        functools.partial(_conv_kernel, tiles_per_seq=seq // bm),
        grid=(nb, m // bm),
        in_specs=[pl.BlockSpec((bm, d), lambda j, i: (i, 0)),
                  pl.BlockSpec((None, d, bn), lambda j, i: (layer, 0, off + j)),
                  pl.BlockSpec((None, d, bn), lambda j, i: (layer, 0, off + nb + j)),
                  pl.BlockSpec((None, d, bn), lambda j, i: (layer, 0, off + 2 * nb + j)),
                  pl.BlockSpec((None, CONV_K, bn), lambda j, i: (layer, 0, j))],
        out_specs=pl.BlockSpec((bm, bn), lambda j, i: (i, j)),
        out_shape=jax.ShapeDtypeStruct((m, cwid), BF16),
        scratch_shapes=[pltpu.VMEM((3, d, bn), BF16),
                        pltpu.VMEM((bm + SUBLANES, bn), F32)],
        compiler_params=_params(2),
        name="conv_proj",
    )(h, w_in, w_in, w_in, conv_w)


def _bucket(n):
    nf = np.maximum(n, MAX_EXACT).astype(np.float64)
    val = np.log(nf / MAX_EXACT) / math.log(MAX_DISTANCE / MAX_EXACT) * (NUM_BUCKETS - MAX_EXACT)
    large = np.minimum(MAX_EXACT + val.astype(np.int64), NUM_BUCKETS - 1)
    return np.where(n < MAX_EXACT, n, large)


def _bucket_tables(t):
    assert _bucket(np.array([t + 1]))[0] == NUM_BUCKETS - 1
    r = np.arange(t)[:, None]
    c = np.arange(t)[None, :]
    d0 = r - c
    diag = np.where(d0 >= 0, _bucket(np.maximum(d0, 0)), -1)
    prev = _bucket(t + r - c)
    return np.stack([diag, prev]).astype(np.int32)


def _bias_table_kernel(rb_ref, bk_ref, o_ref):
    h = pl.program_id(0)
    for mode in range(2):
        bk = bk_ref[mode]
        for mp in range(2):
            col = 2 * h + mp
            far = rb_ref[NUM_BUCKETS - 1, col]
            acc = jnp.where(bk < 0, NEG, 0.0).astype(F32)
            for b in range(NUM_BUCKETS - 1):
                acc = jnp.where(bk == b, (rb_ref[b, col] - far) * LOG2E, acc)
            o_ref[0, mode, mp] = acc


def _bias_tables(rel_bias, n_heads, t):
    buckets = jnp.asarray(_bucket_tables(t))
    return pl.pallas_call(
        _bias_table_kernel,
        grid=(n_heads,),
        in_specs=[pl.BlockSpec(memory_space=pltpu.SMEM),
                  pl.BlockSpec((2, t, t), lambda h: (0, 0, 0))],
        out_specs=pl.BlockSpec((1, 2, 2, t, t), lambda h: (h, 0, 0, 0, 0)),
        out_shape=jax.ShapeDtypeStruct((n_heads, 2, 2, t, t), F32),
        compiler_params=_params(1),
        name="bias_tables",
    )(rel_bias, buckets)


def _fold_lanes(x, op):
    return functools.reduce(op, [x[:, c:c + LANES] for c in range(0, x.shape[1], LANES)])


def _attn_kernel(q_ref, k_ref, v_ref, tb_ref, lam_ref, g_ref, o_ref, v1_ref, *, lam_init, t):
    seq = q_ref.shape[0]
    heads = q_ref.shape[1] // HEAD_DV
    for hh in range(heads):
        v1_ref[hh, :, 0:HEAD_DV] = v_ref[:, hh * HEAD_DV:(hh + 1) * HEAD_DV]
        v1_ref[hh, :, HEAD_DV:] = jnp.ones((seq, LANES), BF16)
    lo = lax.broadcasted_iota(jnp.int32, (1, LANES), 1) < HEAD_DK
    lv = lam_ref[...]
    lam = (jnp.exp(jnp.sum(lv[0:1] * lv[1:2], axis=-1, keepdims=True))
           - jnp.exp(jnp.sum(lv[2:3] * lv[3:4], axis=-1, keepdims=True)) + lam_init)
    g = g_ref[...]
    contract_last = (((1,), (1,)), ((), ()))

    def tile(hh, i):
        cols = slice(hh * HEAD_DV, (hh + 1) * HEAD_DV)
        q = q_ref[i * t:(i + 1) * t, cols]
        zero = jnp.zeros_like(q)
        q2 = jnp.concatenate([jnp.where(lo, q, zero), jnp.where(lo, zero, q)], axis=0)

        pieces = []
        if i >= 2:
            pieces.append((0, (i - 1) * t, None))
        if i >= 1:
            pieces.append(((i - 1) * t, t, 1))
        pieces.append((i * t, t, 0))

        scores = []
        for start, width, table in pieces:
            s = lax.dot_general(q2, k_ref[start:start + width, cols], contract_last,
                                preferred_element_type=F32)
            if table is not None:
                s = s + tb_ref[hh, table].reshape(2 * t, t)
            scores.append(s)
        m = functools.reduce(jnp.maximum, [_fold_lanes(s, jnp.maximum) for s in scores])
        m = jnp.max(m, axis=-1, keepdims=True)
        acc = None
        for s, (start, width, _) in zip(scores, pieces):
            p = jnp.exp2(s - m).astype(BF16)
            pv = jnp.dot(p, v1_ref[hh, start:start + width, :], preferred_element_type=F32)
            acc = pv if acc is None else acc + pv
        out = acc[:, 0:HEAD_DV] / acc[:, HEAD_DV:]
        o = out[0:t] - lam * out[t:2 * t]
        ms = jnp.mean(o * o, axis=-1, keepdims=True)
        o_ref[i * t:(i + 1) * t, cols] = (o * lax.rsqrt(ms + EPS) * g * (1.0 - lam_init)).astype(o_ref.dtype)

    nq = seq // t
    for i in [0] + list(range(nq - 1, 0, -1)):
        for hh in range(heads):
            tile(hh, i)


def _attention(qk, v, tables, lam_vecs, subln_g, lam_init, batch, seq, n_heads, t):
    hps = 2 if n_heads % 2 == 0 else 1
    w = hps * HEAD_DV
    ng = n_heads // hps
    return pl.pallas_call(
        functools.partial(_attn_kernel, lam_init=lam_init, t=t),
        grid=(batch, ng),
        in_specs=[pl.BlockSpec((seq, w), lambda b, h: (b, h)),
                  pl.BlockSpec((seq, w), lambda b, h: (b, ng + h)),
                  pl.BlockSpec((seq, w), lambda b, h: (b, h)),
                  pl.BlockSpec((hps, 2, 2, t, t), lambda b, h: (h, 0, 0, 0, 0)),
                  pl.BlockSpec((4, HEAD_DK), lambda b, h: (0, 0)),
                  pl.BlockSpec((1, HEAD_DV), lambda b, h: (0, 0))],
        out_specs=pl.BlockSpec((seq, w), lambda b, h: (b, h)),
        out_shape=jax.ShapeDtypeStruct((batch * seq, n_heads * HEAD_DV), BF16),
        scratch_shapes=[pltpu.VMEM((hps, seq, HEAD_DV + LANES), BF16)],
        compiler_params=_params(2),
        name="diff_attention",
    )(qk, qk, v, tables, lam_vecs, subln_g)


def _out_kernel(a_ref, c_ref, w_ref, x_ref, g_ref, xo_ref, h_ref, wbf_ref, *, n_pro):
    s = pl.program_id(0)
    ck = w_ref.shape[0]

    @pl.when(s < n_pro)
    def _():
        wbf_ref[pl.ds(pl.multiple_of(s * ck, ck), ck), :] = w_ref[...].astype(BF16)

    @pl.when(s >= n_pro)
    def _():
        aw = a_ref.shape[1]
        acc = jnp.dot(a_ref[...], wbf_ref[0:aw, :], preferred_element_type=F32)
        acc = acc + jnp.dot(c_ref[...], wbf_ref[aw:, :], preferred_element_type=F32)
        xn = x_ref[...] + acc
        xo_ref[...] = xn
        ms = jnp.mean(xn * xn, axis=-1, keepdims=True)
        h_ref[...] = (xn * lax.rsqrt(ms + EPS) * g_ref[...]).astype(h_ref.dtype)


def _out_proj(attn, conv, w_out, layer, x, g):
    m, d = x.shape
    aw, cwid = attn.shape[1], conv.shape[1]
    bm = _tile(m, 512)
    ck = _tile(aw + cwid, 512)
    n_pro = (aw + cwid) // ck

    def row(s):
        return jnp.maximum(s - n_pro, 0)

    return pl.pallas_call(
        functools.partial(_out_kernel, n_pro=n_pro),
        grid=(n_pro + m // bm,),
        in_specs=[pl.BlockSpec((bm, aw), lambda s: (row(s), 0)),
                  pl.BlockSpec((bm, cwid), lambda s: (row(s), 0)),
                  pl.BlockSpec((None, ck, d), lambda s: (layer, jnp.minimum(s, n_pro - 1), 0)),
                  pl.BlockSpec((bm, d), lambda s: (row(s), 0)),
                  pl.BlockSpec((1, d), lambda s: (0, 0))],
        out_specs=[pl.BlockSpec((bm, d), lambda s: (row(s), 0)),
                   pl.BlockSpec((bm, d), lambda s: (row(s), 0))],
        out_shape=[jax.ShapeDtypeStruct((m, d), F32), jax.ShapeDtypeStruct((m, d), BF16)],
        scratch_shapes=[pltpu.VMEM((aw + cwid, d), BF16)],
        compiler_params=_params(1),
        name="out_proj",
    )(attn, conv, w_out, x, g.reshape(1, d))


def _down_kernel(a_ref, w_ref, x_ref, o_ref, wbf_ref):
    @pl.when(pl.program_id(2) == 0)
    def _():
        o_ref[...] = x_ref[...]

    _cast_weight(w_ref, wbf_ref)
    o_ref[...] += jnp.dot(a_ref[...], wbf_ref[...], preferred_element_type=F32)


def _down_proj(a, w_down, layer, x):
    m, d = x.shape
    f = a.shape[1]
    bm = _tile(m, 1024)
    bn = _tile(d, 1024)
    bk = _tile(f, 2048)
    return pl.pallas_call(
        _down_kernel,
        grid=(d // bn, m // bm, f // bk),
        in_specs=[pl.BlockSpec((bm, bk), lambda j, i, k: (i, k)),
                  pl.BlockSpec((None, bk, bn), lambda j, i, k: (layer, k, j)),
                  pl.BlockSpec((bm, bn), lambda j, i, k: (i, j))],
        out_specs=pl.BlockSpec((bm, bn), lambda j, i, k: (i, j)),
        out_shape=jax.ShapeDtypeStruct((m, d), F32),
        scratch_shapes=[pltpu.VMEM((bk, bn), BF16)],
        compiler_params=_params(3),
        name="down_proj",
    )(a, w_down, x)


def kernel(x, w_in, w_out, conv_w, q_norm_g, k_norm_g, lambda_q1, lambda_k1, lambda_q2, lambda_k2,
           subln_g, attn_norm_g, mlp_norm_g, w_up, w_down, rel_bias):
    batch, seq, d = x.shape
    depth = w_in.shape[0]
    aw = d // 2
    cwid = d - aw
    n_heads = aw // HEAD_DV
    assert w_in.shape[2] == 3 * aw + 3 * cwid and rel_bias.shape == (NUM_BUCKETS, 2 * n_heads)
    t = _tile(seq, 256)

    tables = _bias_tables(rel_bias, n_heads, t)
    scale = HEAD_DK ** -0.5 * LOG2E
    xf = x.reshape(batch * seq, d)
    for l in range(depth):
        lam_init = 0.8 - 0.6 * math.exp(-0.3 * l)
        qk_gains = jnp.stack([jnp.tile(q_norm_g[l] * scale, 2), jnp.tile(k_norm_g[l], 2)])
        lam_vecs = jnp.stack([lambda_q1[l], lambda_k1[l], lambda_q2[l], lambda_k2[l]])

        h = _rmsnorm(xf, attn_norm_g[l])
        qk = _qk_proj(h, w_in, l, qk_gains.reshape(2, 1, LANES), aw)
        v = _dense(h, w_in, l, 2 * aw, aw, False, "v_proj")
        conv = _conv_proj(h, w_in, conv_w, l, aw, cwid, seq)
        attn = _attention(qk, v, tables, lam_vecs, subln_g[l].reshape(1, HEAD_DV), lam_init,
                          batch, seq, n_heads, t)
        xf, hm = _out_proj(attn, conv, w_out, l, xf, mlp_norm_g[l])
        act = _dense(hm, w_up, l, 0, w_up.shape[2], True, "up_proj")
        xf = _down_proj(act, w_down, l, xf)
    return xf.reshape(batch, seq, d)
```

```python
import functools
import math

import numpy as np
import jax
import jax.numpy as jnp
from jax import lax
from jax.experimental import pallas as pl
from jax.experimental.pallas import tpu as pltpu

HEAD_DV = 128
HEAD_DK = HEAD_DV // 2
CONV_K = 3
NUM_BUCKETS = 32
MAX_DISTANCE = 128
MAX_EXACT = NUM_BUCKETS // 2
EPS = 1e-6
NEG = -1e30
LOG2E = math.log2(math.e)

LANES = 128
SUBLANES = 8
VMEM_LIMIT_BYTES = 56 * 1024 * 1024

F32 = jnp.float32
BF16 = jnp.bfloat16


def _tile(dim, pref):
    if dim <= pref:
        return dim
    t = (pref // LANES) * LANES
    while dim % t:
        t -= LANES
    return t


def _params(n_axes):
    return pltpu.CompilerParams(
        dimension_semantics=("arbitrary",) * n_axes,
        vmem_limit_bytes=VMEM_LIMIT_BYTES,
    )


def _rmsnorm_kernel(x_ref, g_ref, o_ref):
    x = x_ref[...]
    ms = jnp.mean(x * x, axis=-1, keepdims=True)
    o_ref[...] = (x * lax.rsqrt(ms + EPS) * g_ref[...]).astype(o_ref.dtype)


def _rmsnorm(x, g):
    m, d = x.shape
    tm = _tile(m, 512)
    return pl.pallas_call(
        _rmsnorm_kernel,
        grid=(m // tm,),
        in_specs=[pl.BlockSpec((tm, d), lambda i: (i, 0)),
                  pl.BlockSpec((1, d), lambda i: (0, 0))],
        out_specs=pl.BlockSpec((tm, d), lambda i: (i, 0)),
        out_shape=jax.ShapeDtypeStruct((m, d), BF16),
        compiler_params=_params(1),
        name="rmsnorm",
    )(x, g.reshape(1, d))


def _cast_weight(w_ref, wbf_ref):
    k = w_ref.shape[0]
    ck = _tile(k, 256)
    for r in range(0, k, ck):
        wbf_ref[r:r + ck, :] = w_ref[r:r + ck, :].astype(BF16)


def _segment_ones(width):
    seg = np.arange(width) // HEAD_DK
    return jnp.asarray(seg[:, None] == seg[None, :], dtype=BF16)


def _qk_kernel(h_ref, w_ref, g_ref, e_ref, o_ref, wbf_ref):
    @pl.when(pl.program_id(1) == 0)
    def _():
        _cast_weight(w_ref, wbf_ref)

    acc = jnp.dot(h_ref[...], wbf_ref[...], preferred_element_type=F32)
    g = g_ref[...]
    gw = e_ref.shape[0]
    for c in range(0, acc.shape[1], gw):
        a = acc[:, c:c + gw]
        ssq = jnp.dot((a * a).astype(BF16), e_ref[...], preferred_element_type=F32)
        inv = lax.rsqrt(ssq * (1.0 / HEAD_DK) + EPS)
        for cc in range(0, gw, LANES):
            o_ref[:, c + cc:c + cc + LANES] = (
                a[:, cc:cc + LANES] * inv[:, cc:cc + LANES] * g).astype(o_ref.dtype)


def _qk_proj(h, w_in, layer, gains, aw):
    m, d = h.shape
    bm = _tile(m, 1024)
    gw = _tile(aw, 2 * LANES)
    return pl.pallas_call(
        _qk_kernel,
        grid=(2, m // bm),
        in_specs=[pl.BlockSpec((bm, d), lambda j, i: (i, 0)),
                  pl.BlockSpec((None, d, aw), lambda j, i: (layer, 0, j)),
                  pl.BlockSpec((None, 1, LANES), lambda j, i: (j, 0, 0)),
                  pl.BlockSpec((gw, gw), lambda j, i: (0, 0))],
        out_specs=pl.BlockSpec((bm, aw), lambda j, i: (i, j)),
        out_shape=jax.ShapeDtypeStruct((m, 2 * aw), BF16),
        scratch_shapes=[pltpu.VMEM((d, aw), BF16)],
        compiler_params=_params(2),
        name="qk_proj",
    )(h, w_in, gains, _segment_ones(gw))


def _dense_kernel(h_ref, w_ref, o_ref, wbf_ref):
    @pl.when(pl.program_id(1) == 0)
    def _():
        _cast_weight(w_ref, wbf_ref)

    o_ref[...] = jnp.dot(h_ref[...], wbf_ref[...], preferred_element_type=F32).astype(o_ref.dtype)


def _dense(h, w, layer, col0, ncols, name):
    m, d = h.shape
    bm = _tile(m, 1024)
    bn = _tile(ncols, 1024)
    assert col0 % bn == 0
    return pl.pallas_call(
        _dense_kernel,
        grid=(ncols // bn, m // bm),
        in_specs=[pl.BlockSpec((bm, d), lambda j, i: (i, 0)),
                  pl.BlockSpec((None, d, bn), lambda j, i: (layer, 0, col0 // bn + j))],
        out_specs=pl.BlockSpec((bm, bn), lambda j, i: (i, j)),
        out_shape=jax.ShapeDtypeStruct((m, ncols), BF16),
        scratch_shapes=[pltpu.VMEM((d, bn), BF16)],
        compiler_params=_params(2),
        name=name,
    )(h, w)


def _conv_kernel(h_ref, wb_ref, wc_ref, wi_ref, cw_ref, o_ref, wbf_ref, ubuf_ref, *, tiles_per_seq):
    i = pl.program_id(1)
    bm = h_ref.shape[0]

    @pl.when(i == 0)
    def _():
        _cast_weight(wb_ref, wbf_ref.at[0])
        _cast_weight(wc_ref, wbf_ref.at[1])
        _cast_weight(wi_ref, wbf_ref.at[2])

    @pl.when(i % tiles_per_seq == 0)
    def _():
        ubuf_ref[0:SUBLANES, :] = jnp.zeros((SUBLANES, ubuf_ref.shape[1]), F32)

    h = h_ref[...]
    gate_c = jnp.dot(h, wbf_ref[1], preferred_element_type=F32)
    conv_in = jnp.dot(h, wbf_ref[2], preferred_element_type=F32)
    ubuf_ref[SUBLANES:SUBLANES + bm, :] = gate_c * conv_in
    gate_b = jnp.dot(h, wbf_ref[0], preferred_element_type=F32)
    cw = cw_ref[...]
    conv = (cw[0:1] * ubuf_ref[SUBLANES - 2:SUBLANES - 2 + bm, :]
            + cw[1:2] * ubuf_ref[SUBLANES - 1:SUBLANES - 1 + bm, :]
            + cw[2:3] * ubuf_ref[SUBLANES:SUBLANES + bm, :])
    o_ref[...] = (gate_b * conv).astype(o_ref.dtype)
    ubuf_ref[0:SUBLANES, :] = ubuf_ref[bm:bm + SUBLANES, :]


def _conv_proj(h, w_in, conv_w, layer, aw, cwid, seq):
    m, d = h.shape
    bm = _tile(seq, 1024)
    bn = _tile(cwid, 512)
    off = 3 * aw // bn
    nb = cwid // bn
    return pl.pallas_call(
        functools.partial(_conv_kernel, tiles_per_seq=seq // bm),
        grid=(nb, m // bm),
        in_specs=[pl.BlockSpec((bm, d), lambda j, i: (i, 0)),
                  pl.BlockSpec((None, d, bn), lambda j, i: (layer, 0, off + j)),
                  pl.BlockSpec((None, d, bn), lambda j, i: (layer, 0, off + nb + j)),
                  pl.BlockSpec((None, d, bn), lambda j, i: (layer, 0, off + 2 * nb + j)),
                  pl.BlockSpec((None, CONV_K, bn), lambda j, i: (layer, 0, j))],
        out_specs=pl.BlockSpec((bm, bn), lambda j, i: (i, j)),
        out_shape=jax.ShapeDtypeStruct((m, cwid), BF16),
        scratch_shapes=[pltpu.VMEM((3, d, bn), BF16),
                        pltpu.VMEM((bm + SUBLANES, bn), F32)],
        compiler_params=_params(2),
        name="conv_proj",
    )(h, w_in, w_in, w_in, conv_w)


def _bucket(n):
    nf = np.maximum(n, MAX_EXACT).astype(np.float64)
    val = np.log(nf / MAX_EXACT) / math.log(MAX_DISTANCE / MAX_EXACT) * (NUM_BUCKETS - MAX_EXACT)
    large = np.minimum(MAX_EXACT + val.astype(np.int64), NUM_BUCKETS - 1)
    return np.where(n < MAX_EXACT, n, large)


def _bucket_tables(t):
    assert _bucket(np.array([t + 1]))[0] == NUM_BUCKETS - 1
    r = np.arange(t)[:, None]
    c = np.arange(t)[None, :]
    d0 = r - c
    diag = np.where(d0 >= 0, _bucket(np.maximum(d0, 0)), -1)
    prev = _bucket(t + r - c)
    return np.stack([diag, prev]).astype(np.int32)


def _bias_table_kernel(rb_ref, bk_ref, o_ref):
    h = pl.program_id(0)
    for mode in range(2):
        bk = bk_ref[mode]
        for mp in range(2):
            col = 2 * h + mp
            far = rb_ref[NUM_BUCKETS - 1, col]
            acc = jnp.where(bk < 0, NEG, 0.0).astype(F32)
            for b in range(NUM_BUCKETS - 1):
                acc = jnp.where(bk == b, (rb_ref[b, col] - far) * LOG2E, acc)
            o_ref[0, mode, mp] = acc


def _bias_tables(rel_bias, n_heads, t):
    buckets = jnp.asarray(_bucket_tables(t))
    return pl.pallas_call(
        _bias_table_kernel,
        grid=(n_heads,),
        in_specs=[pl.BlockSpec(memory_space=pltpu.SMEM),
                  pl.BlockSpec((2, t, t), lambda h: (0, 0, 0))],
        out_specs=pl.BlockSpec((1, 2, 2, t, t), lambda h: (h, 0, 0, 0, 0)),
        out_shape=jax.ShapeDtypeStruct((n_heads, 2, 2, t, t), F32),
        compiler_params=_params(1),
        name="bias_tables",
    )(rel_bias, buckets)


def _fold_lanes(x, op):
    return functools.reduce(op, [x[:, c:c + LANES] for c in range(0, x.shape[1], LANES)])


def _attn_kernel(q_ref, k_ref, v_ref, tb_ref, lam_ref, g_ref, o_ref, v1_ref, *, lam_init, t):
    seq = q_ref.shape[0]
    heads = q_ref.shape[1] // HEAD_DV
    for hh in range(heads):
        v1_ref[hh, :, 0:HEAD_DV] = v_ref[:, hh * HEAD_DV:(hh + 1) * HEAD_DV]
        v1_ref[hh, :, HEAD_DV:] = jnp.ones((seq, LANES), BF16)
    lo = lax.broadcasted_iota(jnp.int32, (1, LANES), 1) < HEAD_DK
    lv = lam_ref[...]
    lam = (jnp.exp(jnp.sum(lv[0:1] * lv[1:2], axis=-1, keepdims=True))
           - jnp.exp(jnp.sum(lv[2:3] * lv[3:4], axis=-1, keepdims=True)) + lam_init)
    g = g_ref[...]
    contract_last = (((1,), (1,)), ((), ()))

    def tile(hh, i):
        cols = slice(hh * HEAD_DV, (hh + 1) * HEAD_DV)
        q = q_ref[i * t:(i + 1) * t, cols]
        zero = jnp.zeros_like(q)
        q2 = jnp.concatenate([jnp.where(lo, q, zero), jnp.where(lo, zero, q)], axis=0)

        pieces = []
        if i >= 2:
            pieces.append((0, (i - 1) * t, None))
        if i >= 1:
            pieces.append(((i - 1) * t, t, 1))
        pieces.append((i * t, t, 0))

        scores = []
        for start, width, table in pieces:
            s = lax.dot_general(q2, k_ref[start:start + width, cols], contract_last,
                                preferred_element_type=F32)
            if table is not None:
                s = s + tb_ref[hh, table].reshape(2 * t, t)
            scores.append(s)
        m = functools.reduce(jnp.maximum, [_fold_lanes(s, jnp.maximum) for s in scores])
        m = jnp.max(m, axis=-1, keepdims=True)
        acc = None
        for s, (start, width, _) in zip(scores, pieces):
            p = jnp.exp2(s - m).astype(BF16)
            pv = jnp.dot(p, v1_ref[hh, start:start + width, :], preferred_element_type=F32)
            acc = pv if acc is None else acc + pv
        out = acc[:, 0:HEAD_DV] / acc[:, HEAD_DV:]
        o = out[0:t] - lam * out[t:2 * t]
        ms = jnp.mean(o * o, axis=-1, keepdims=True)
        o_ref[i * t:(i + 1) * t, cols] = (o * lax.rsqrt(ms + EPS) * g * (1.0 - lam_init)).astype(o_ref.dtype)

    nq = seq // t
    for i in [0] + list(range(nq - 1, 0, -1)):
        for hh in range(heads):
            tile(hh, i)


def _attention(qk, v, tables, lam_vecs, subln_g, lam_init, batch, seq, n_heads, t):
    hps = 2 if n_heads % 2 == 0 else 1
    w = hps * HEAD_DV
    ng = n_heads // hps
    return pl.pallas_call(
        functools.partial(_attn_kernel, lam_init=lam_init, t=t),
        grid=(batch, ng),
        in_specs=[pl.BlockSpec((seq, w), lambda b, h: (b, h)),
                  pl.BlockSpec((seq, w), lambda b, h: (b, ng + h)),
                  pl.BlockSpec((seq, w), lambda b, h: (b, h)),
                  pl.BlockSpec((hps, 2, 2, t, t), lambda b, h: (h, 0, 0, 0, 0)),
                  pl.BlockSpec((4, HEAD_DK), lambda b, h: (0, 0)),
                  pl.BlockSpec((1, HEAD_DV), lambda b, h: (0, 0))],
        out_specs=pl.BlockSpec((seq, w), lambda b, h: (b, h)),
        out_shape=jax.ShapeDtypeStruct((batch * seq, n_heads * HEAD_DV), BF16),
        scratch_shapes=[pltpu.VMEM((hps, seq, HEAD_DV + LANES), BF16)],
        compiler_params=_params(2),
        name="diff_attention",
    )(qk, qk, v, tables, lam_vecs, subln_g)


def _out_kernel(a_ref, c_ref, w_ref, x_ref, g_ref, xo_ref, h_ref, wbf_ref, *, n_pro):
    s = pl.program_id(0)
    ck = w_ref.shape[0]

    @pl.when(s < n_pro)
    def _():
        wbf_ref[pl.ds(pl.multiple_of(s * ck, ck), ck), :] = w_ref[...].astype(BF16)

    @pl.when(s >= n_pro)
    def _():
        aw = a_ref.shape[1]
        acc = jnp.dot(a_ref[...], wbf_ref[0:aw, :], preferred_element_type=F32)
        acc = acc + jnp.dot(c_ref[...], wbf_ref[aw:, :], preferred_element_type=F32)
        xn = x_ref[...] + acc
        xo_ref[...] = xn
        ms = jnp.mean(xn * xn, axis=-1, keepdims=True)
        h_ref[...] = (xn * lax.rsqrt(ms + EPS) * g_ref[...]).astype(h_ref.dtype)


def _out_proj(attn, conv, w_out, layer, x, g):
    m, d = x.shape
    aw, cwid = attn.shape[1], conv.shape[1]
    bm = _tile(m, 512)
    ck = _tile(aw + cwid, 512)
    n_pro = (aw + cwid) // ck

    def row(s):
        return jnp.maximum(s - n_pro, 0)

    return pl.pallas_call(
        functools.partial(_out_kernel, n_pro=n_pro),
        grid=(n_pro + m // bm,),
        in_specs=[pl.BlockSpec((bm, aw), lambda s: (row(s), 0)),
                  pl.BlockSpec((bm, cwid), lambda s: (row(s), 0)),
                  pl.BlockSpec((None, ck, d), lambda s: (layer, jnp.minimum(s, n_pro - 1), 0)),
                  pl.BlockSpec((bm, d), lambda s: (row(s), 0)),
                  pl.BlockSpec((1, d), lambda s: (0, 0))],
        out_specs=[pl.BlockSpec((bm, d), lambda s: (row(s), 0)),
                   pl.BlockSpec((bm, d), lambda s: (row(s), 0))],
        out_shape=[jax.ShapeDtypeStruct((m, d), F32), jax.ShapeDtypeStruct((m, d), BF16)],
        scratch_shapes=[pltpu.VMEM((aw + cwid, d), BF16)],
        compiler_params=_params(1),
        name="out_proj",
    )(attn, conv, w_out, x, g.reshape(1, d))


def _up_kernel(h_ref, w_ref, wd_ref, o_ref, wdo_ref, wbf_ref):
    @pl.when(pl.program_id(1) == 0)
    def _():
        _cast_weight(w_ref, wbf_ref)

    wdo_ref[...] = wd_ref[...].astype(BF16)
    acc = jnp.dot(h_ref[...], wbf_ref[...], preferred_element_type=F32)
    o_ref[...] = jnp.square(jnp.maximum(acc, 0.0)).astype(o_ref.dtype)


def _up_proj(h, w_up, w_down, layer):
    m, d = h.shape
    f = w_up.shape[2]
    bm = _tile(m, 1024)
    bn = _tile(f, 1024)
    nm = m // bm
    steps = (f // bn) * nm
    slab = f // steps
    assert f % steps == 0 and slab % (2 * SUBLANES) == 0
    return pl.pallas_call(
        _up_kernel,
        grid=(f // bn, nm),
        in_specs=[pl.BlockSpec((bm, d), lambda j, i: (i, 0)),
                  pl.BlockSpec((None, d, bn), lambda j, i: (layer, 0, j)),
                  pl.BlockSpec((None, slab, d), lambda j, i: (layer, j * nm + i, 0))],
        out_specs=[pl.BlockSpec((bm, bn), lambda j, i: (i, j)),
                   pl.BlockSpec((slab, d), lambda j, i: (j * nm + i, 0))],
        out_shape=[jax.ShapeDtypeStruct((m, f), BF16), jax.ShapeDtypeStruct((f, d), BF16)],
        scratch_shapes=[pltpu.VMEM((d, bn), BF16)],
        compiler_params=_params(2),
        name="up_proj",
    )(h, w_up, w_down)


def _down_kernel(a_ref, w_ref, x_ref, *rest, with_norm):
    if with_norm:
        g_ref, xo_ref, h_ref = rest
    else:
        (xo_ref,) = rest
    k = pl.program_id(1)
    last = pl.num_programs(1) - 1

    @pl.when(k == 0)
    def _():
        xo_ref[...] = x_ref[...]

    if not with_norm:
        xo_ref[...] += jnp.dot(a_ref[...], w_ref[...], preferred_element_type=F32)
        return

    @pl.when(k < last)
    def _():
        xo_ref[...] += jnp.dot(a_ref[...], w_ref[...], preferred_element_type=F32)

    @pl.when(k == last)
    def _():
        xn = xo_ref[...] + jnp.dot(a_ref[...], w_ref[...], preferred_element_type=F32)
        xo_ref[...] = xn
        ms = jnp.mean(xn * xn, axis=-1, keepdims=True)
        h_ref[...] = (xn * lax.rsqrt(ms + EPS) * g_ref[...]).astype(h_ref.dtype)


def _down_proj(a, w_down_bf, x, g_next):
    m, d = x.shape
    f = a.shape[1]
    bm = _tile(m, 512)
    bk = _tile(f, 2048)
    with_norm = g_next is not None
    row = pl.BlockSpec((bm, d), lambda i, k: (i, 0))
    in_specs = [pl.BlockSpec((bm, bk), lambda i, k: (i, k)),
                pl.BlockSpec((bk, d), lambda i, k: (k, 0)),
                row]
    args = [a, w_down_bf, x]
    out_specs = [row]
    out_shape = [jax.ShapeDtypeStruct((m, d), F32)]
    if with_norm:
        in_specs.append(pl.BlockSpec((1, d), lambda i, k: (0, 0)))
        args.append(g_next.reshape(1, d))
        out_specs.append(row)
        out_shape.append(jax.ShapeDtypeStruct((m, d), BF16))
    outs = pl.pallas_call(
        functools.partial(_down_kernel, with_norm=with_norm),
        grid=(m // bm, f // bk),
        in_specs=in_specs,
        out_specs=out_specs,
        out_shape=out_shape,
        compiler_params=_params(2),
        name="down_proj",
    )(*args)
    return (outs[0], outs[1]) if with_norm else (outs[0], None)


def kernel(x, w_in, w_out, conv_w, q_norm_g, k_norm_g, lambda_q1, lambda_k1, lambda_q2, lambda_k2,
           subln_g, attn_norm_g, mlp_norm_g, w_up, w_down, rel_bias):
    batch, seq, d = x.shape
    depth = w_in.shape[0]
    aw = d // 2
    cwid = d - aw
    n_heads = aw // HEAD_DV
    assert w_in.shape[2] == 3 * aw + 3 * cwid and rel_bias.shape == (NUM_BUCKETS, 2 * n_heads)
    t = _tile(seq, 256)

    tables = _bias_tables(rel_bias, n_heads, t)
    scale = HEAD_DK ** -0.5 * LOG2E
    xf = x.reshape(batch * seq, d)
    h = _rmsnorm(xf, attn_norm_g[0])
    for l in range(depth):
        lam_init = 0.8 - 0.6 * math.exp(-0.3 * l)
        qk_gains = jnp.stack([jnp.tile(q_norm_g[l] * scale, 2), jnp.tile(k_norm_g[l], 2)])
        lam_vecs = jnp.stack([lambda_q1[l], lambda_k1[l], lambda_q2[l], lambda_k2[l]])

        qk = _qk_proj(h, w_in, l, qk_gains.reshape(2, 1, LANES), aw)
        v = _dense(h, w_in, l, 2 * aw, aw, "v_proj")
        conv = _conv_proj(h, w_in, conv_w, l, aw, cwid, seq)
        attn = _attention(qk, v, tables, lam_vecs, subln_g[l].reshape(1, HEAD_DV), lam_init,
                          batch, seq, n_heads, t)
        xf, hm = _out_proj(attn, conv, w_out, l, xf, mlp_norm_g[l])
        act, w_down_bf = _up_proj(hm, w_up, w_down, l)
        xf, h = _down_proj(act, w_down_bf, xf, attn_norm_g[l + 1] if l + 1 < depth else None)
    return xf.reshape(batch, seq, d)
```

```python
import functools
import math

import numpy as np
import jax
import jax.numpy as jnp
from jax import lax
from jax.experimental import pallas as pl
from jax.experimental.pallas import tpu as pltpu

HEAD_DV = 128
HEAD_DK = HEAD_DV // 2
CONV_K = 3
NUM_BUCKETS = 32
MAX_DISTANCE = 128
MAX_EXACT = NUM_BUCKETS // 2
EPS = 1e-6
NEG = -1e30
LOG2E = math.log2(math.e)

LANES = 128
SUBLANES = 8
VMEM_LIMIT_BYTES = 60 * 1024 * 1024

F32 = jnp.float32
BF16 = jnp.bfloat16


def _tile(dim, pref):
    if dim <= pref:
        return dim
    t = (pref // LANES) * LANES
    while dim % t:
        t -= LANES
    return t


def _params(n_axes):
    return pltpu.CompilerParams(
        dimension_semantics=("arbitrary",) * n_axes,
        vmem_limit_bytes=VMEM_LIMIT_BYTES,
    )


def _rmsnorm_kernel(x_ref, g_ref, o_ref):
    x = x_ref[...]
    ms = jnp.mean(x * x, axis=-1, keepdims=True)
    o_ref[...] = (x * lax.rsqrt(ms + EPS) * g_ref[...]).astype(o_ref.dtype)


def _rmsnorm(x, g):
    m, d = x.shape
    tm = _tile(m, 512)
    return pl.pallas_call(
        _rmsnorm_kernel,
        grid=(m // tm,),
        in_specs=[pl.BlockSpec((tm, d), lambda i: (i, 0)),
                  pl.BlockSpec((1, d), lambda i: (0, 0))],
        out_specs=pl.BlockSpec((tm, d), lambda i: (i, 0)),
        out_shape=jax.ShapeDtypeStruct((m, d), BF16),
        compiler_params=_params(1),
        name="rmsnorm",
    )(x, g.reshape(1, d))


def _cast_weight(w_ref, wbf_ref):
    k = w_ref.shape[0]
    ck = _tile(k, 256)
    for r in range(0, k, ck):
        wbf_ref[r:r + ck, :] = w_ref[r:r + ck, :].astype(BF16)


def _segment_ones(width):
    seg = np.arange(width) // HEAD_DK
    return jnp.asarray(seg[:, None] == seg[None, :], dtype=BF16)


def _qk_kernel(h_ref, w_ref, g_ref, e_ref, o_ref, wbf_ref):
    @pl.when(pl.program_id(1) == 0)
    def _():
        _cast_weight(w_ref, wbf_ref)

    acc = jnp.dot(h_ref[...], wbf_ref[...], preferred_element_type=F32)
    g = g_ref[...]
    gw = e_ref.shape[0]
    for c in range(0, acc.shape[1], gw):
        a = acc[:, c:c + gw]
        ssq = jnp.dot((a * a).astype(BF16), e_ref[...], preferred_element_type=F32)
        inv = lax.rsqrt(ssq * (1.0 / HEAD_DK) + EPS)
        for cc in range(0, gw, LANES):
            o_ref[:, c + cc:c + cc + LANES] = (
                a[:, cc:cc + LANES] * inv[:, cc:cc + LANES] * g).astype(o_ref.dtype)


def _qk_proj(h, w_in, layer, gains, aw):
    m, d = h.shape
    bm = _tile(m, 1024)
    gw = _tile(aw, 2 * LANES)
    return pl.pallas_call(
        _qk_kernel,
        grid=(2, m // bm),
        in_specs=[pl.BlockSpec((bm, d), lambda j, i: (i, 0)),
                  pl.BlockSpec((None, d, aw), lambda j, i: (layer, 0, j)),
                  pl.BlockSpec((None, 1, LANES), lambda j, i: (j, 0, 0)),
                  pl.BlockSpec((gw, gw), lambda j, i: (0, 0))],
        out_specs=pl.BlockSpec((bm, aw), lambda j, i: (i, j)),
        out_shape=jax.ShapeDtypeStruct((m, 2 * aw), BF16),
        scratch_shapes=[pltpu.VMEM((d, aw), BF16)],
        compiler_params=_params(2),
        name="qk_proj",
    )(h, w_in, gains, _segment_ones(gw))


def _dense_kernel(h_ref, w_ref, o_ref, wbf_ref):
    @pl.when(pl.program_id(1) == 0)
    def _():
        _cast_weight(w_ref, wbf_ref)

    o_ref[...] = jnp.dot(h_ref[...], wbf_ref[...], preferred_element_type=F32).astype(o_ref.dtype)


def _dense(h, w, layer, col0, ncols, name):
    m, d = h.shape
    bm = _tile(m, 1024)
    bn = _tile(ncols, 1024)
    assert col0 % bn == 0
    return pl.pallas_call(
        _dense_kernel,
        grid=(ncols // bn, m // bm),
        in_specs=[pl.BlockSpec((bm, d), lambda j, i: (i, 0)),
                  pl.BlockSpec((None, d, bn), lambda j, i: (layer, 0, col0 // bn + j))],
        out_specs=pl.BlockSpec((bm, bn), lambda j, i: (i, j)),
        out_shape=jax.ShapeDtypeStruct((m, ncols), BF16),
        scratch_shapes=[pltpu.VMEM((d, bn), BF16)],
        compiler_params=_params(2),
        name=name,
    )(h, w)


def _conv_kernel(h_ref, wb_ref, wc_ref, wi_ref, cw_ref, o_ref, wbf_ref, ubuf_ref, *, tiles_per_seq):
    i = pl.program_id(1)
    bm = h_ref.shape[0]

    @pl.when(i == 0)
    def _():
        _cast_weight(wb_ref, wbf_ref.at[0])
        _cast_weight(wc_ref, wbf_ref.at[1])
        _cast_weight(wi_ref, wbf_ref.at[2])

    @pl.when(i % tiles_per_seq == 0)
    def _():
        ubuf_ref[0:SUBLANES, :] = jnp.zeros((SUBLANES, ubuf_ref.shape[1]), F32)

    h = h_ref[...]
    gate_c = jnp.dot(h, wbf_ref[1], preferred_element_type=F32)
    conv_in = jnp.dot(h, wbf_ref[2], preferred_element_type=F32)
    ubuf_ref[SUBLANES:SUBLANES + bm, :] = gate_c * conv_in
    gate_b = jnp.dot(h, wbf_ref[0], preferred_element_type=F32)
    cw = cw_ref[...]
    conv = (cw[0:1] * ubuf_ref[SUBLANES - 2:SUBLANES - 2 + bm, :]
            + cw[1:2] * ubuf_ref[SUBLANES - 1:SUBLANES - 1 + bm, :]
            + cw[2:3] * ubuf_ref[SUBLANES:SUBLANES + bm, :])
    o_ref[...] = (gate_b * conv).astype(o_ref.dtype)
    ubuf_ref[0:SUBLANES, :] = ubuf_ref[bm:bm + SUBLANES, :]


def _conv_proj(h, w_in, conv_w, layer, aw, cwid, seq):
    m, d = h.shape
    bm = _tile(seq, 1024)
    bn = _tile(cwid, 512)
    off = 3 * aw // bn
    nb = cwid // bn
    return pl.pallas_call(
        functools.partial(_conv_kernel, tiles_per_seq=seq // bm),
        grid=(nb, m // bm),
        in_specs=[pl.BlockSpec((bm, d), lambda j, i: (i, 0)),
                  pl.BlockSpec((None, d, bn), lambda j, i: (layer, 0, off + j)),
                  pl.BlockSpec((None, d, bn), lambda j, i: (layer, 0, off + nb + j)),
                  pl.BlockSpec((None, d, bn), lambda j, i: (layer, 0, off + 2 * nb + j)),
                  pl.BlockSpec((None, CONV_K, bn), lambda j, i: (layer, 0, j))],
        out_specs=pl.BlockSpec((bm, bn), lambda j, i: (i, j)),
        out_shape=jax.ShapeDtypeStruct((m, cwid), BF16),
        scratch_shapes=[pltpu.VMEM((3, d, bn), BF16),
                        pltpu.VMEM((bm + SUBLANES, bn), F32)],
        compiler_params=_params(2),
        name="conv_proj",
    )(h, w_in, w_in, w_in, conv_w)


def _bucket(n):
    nf = np.maximum(n, MAX_EXACT).astype(np.float64)
    val = np.log(nf / MAX_EXACT) / math.log(MAX_DISTANCE / MAX_EXACT) * (NUM_BUCKETS - MAX_EXACT)
    large = np.minimum(MAX_EXACT + val.astype(np.int64), NUM_BUCKETS - 1)
    return np.where(n < MAX_EXACT, n, large)


def _bucket_tables(t):
    assert _bucket(np.array([t + 1]))[0] == NUM_BUCKETS - 1
    r = np.arange(t)[:, None]
    c = np.arange(t)[None, :]
    d0 = r - c
    diag = np.where(d0 >= 0, _bucket(np.maximum(d0, 0)), -1)
    prev = _bucket(t + r - c)
    return np.stack([diag, prev]).astype(np.int32)


def _bias_table_kernel(rb_ref, bk_ref, o_ref):
    h = pl.program_id(0)
    for mode in range(2):
        bk = bk_ref[mode]
        for mp in range(2):
            col = 2 * h + mp
            far = rb_ref[NUM_BUCKETS - 1, col]
            acc = jnp.where(bk < 0, NEG, 0.0).astype(F32)
            for b in range(NUM_BUCKETS - 1):
                acc = jnp.where(bk == b, (rb_ref[b, col] - far) * LOG2E, acc)
            o_ref[0, mode, mp] = acc


def _bias_tables(rel_bias, n_heads, t):
    buckets = jnp.asarray(_bucket_tables(t))
    return pl.pallas_call(
        _bias_table_kernel,
        grid=(n_heads,),
        in_specs=[pl.BlockSpec(memory_space=pltpu.SMEM),
                  pl.BlockSpec((2, t, t), lambda h: (0, 0, 0))],
        out_specs=pl.BlockSpec((1, 2, 2, t, t), lambda h: (h, 0, 0, 0, 0)),
        out_shape=jax.ShapeDtypeStruct((n_heads, 2, 2, t, t), F32),
        compiler_params=_params(1),
        name="bias_tables",
    )(rel_bias, buckets)


def _fold_lanes(x, op):
    return functools.reduce(op, [x[:, c:c + LANES] for c in range(0, x.shape[1], LANES)])


def _attn_kernel(q_ref, k_ref, v_ref, tb_ref, lam_ref, g_ref, wd_ref, o_ref, wdo_ref, v1_ref,
                 *, lam_init, t):
    seq = q_ref.shape[0]
    heads = q_ref.shape[1] // HEAD_DV
    wdo_ref[...] = wd_ref[...].astype(BF16)
    for hh in range(heads):
        v1_ref[hh, :, 0:HEAD_DV] = v_ref[:, hh * HEAD_DV:(hh + 1) * HEAD_DV]
        v1_ref[hh, :, HEAD_DV:] = jnp.ones((seq, LANES), BF16)
    lo = lax.broadcasted_iota(jnp.int32, (1, LANES), 1) < HEAD_DK
    lv = lam_ref[...]
    lam = (jnp.exp(jnp.sum(lv[0:1] * lv[1:2], axis=-1, keepdims=True))
           - jnp.exp(jnp.sum(lv[2:3] * lv[3:4], axis=-1, keepdims=True)) + lam_init)
    g = g_ref[...]
    contract_last = (((1,), (1,)), ((), ()))

    def tile(hh, i):
        cols = slice(hh * HEAD_DV, (hh + 1) * HEAD_DV)
        q = q_ref[i * t:(i + 1) * t, cols]
        zero = jnp.zeros_like(q)
        q2 = jnp.concatenate([jnp.where(lo, q, zero), jnp.where(lo, zero, q)], axis=0)

        pieces = []
        if i >= 2:
            pieces.append((0, (i - 1) * t, None))
        if i >= 1:
            pieces.append(((i - 1) * t, t, 1))
        pieces.append((i * t, t, 0))

        scores = []
        for start, width, table in pieces:
            s = lax.dot_general(q2, k_ref[start:start + width, cols], contract_last,
                                preferred_element_type=F32)
            if table is not None:
                s = s + tb_ref[hh, table].reshape(2 * t, t)
            scores.append(s)
        m = functools.reduce(jnp.maximum, [_fold_lanes(s, jnp.maximum) for s in scores])
        m = jnp.max(m, axis=-1, keepdims=True)
        acc = None
        for s, (start, width, _) in zip(scores, pieces):
            p = jnp.exp2(s - m).astype(BF16)
            pv = jnp.dot(p, v1_ref[hh, start:start + width, :], preferred_element_type=F32)
            acc = pv if acc is None else acc + pv
        out = acc[:, 0:HEAD_DV] / acc[:, HEAD_DV:]
        o = out[0:t] - lam * out[t:2 * t]
        ms = jnp.mean(o * o, axis=-1, keepdims=True)
        o_ref[i * t:(i + 1) * t, cols] = (o * lax.rsqrt(ms + EPS) * g * (1.0 - lam_init)).astype(o_ref.dtype)

    nq = seq // t
    for i in [0] + list(range(nq - 1, 0, -1)):
        for hh in range(heads):
            tile(hh, i)


def _attention(qk, v, tables, lam_vecs, subln_g, lam_init, w_down, layer, batch, seq, n_heads, t):
    hps = 2 if n_heads % 2 == 0 else 1
    w = hps * HEAD_DV
    ng = n_heads // hps
    f, d = w_down.shape[1:]
    slab = f // (batch * ng)
    assert f % (batch * ng) == 0 and slab % (2 * SUBLANES) == 0
    return pl.pallas_call(
        functools.partial(_attn_kernel, lam_init=lam_init, t=t),
        grid=(batch, ng),
        in_specs=[pl.BlockSpec((seq, w), lambda b, h: (b, h)),
                  pl.BlockSpec((seq, w), lambda b, h: (b, ng + h)),
                  pl.BlockSpec((seq, w), lambda b, h: (b, h)),
                  pl.BlockSpec((hps, 2, 2, t, t), lambda b, h: (h, 0, 0, 0, 0)),
                  pl.BlockSpec((4, HEAD_DK), lambda b, h: (0, 0)),
                  pl.BlockSpec((1, HEAD_DV), lambda b, h: (0, 0)),
                  pl.BlockSpec((None, slab, d), lambda b, h: (layer, b * ng + h, 0))],
        out_specs=[pl.BlockSpec((seq, w), lambda b, h: (b, h)),
                   pl.BlockSpec((slab, d), lambda b, h: (b * ng + h, 0))],
        out_shape=[jax.ShapeDtypeStruct((batch * seq, n_heads * HEAD_DV), BF16),
                   jax.ShapeDtypeStruct((f, d), BF16)],
        scratch_shapes=[pltpu.VMEM((hps, seq, HEAD_DV + LANES), BF16)],
        compiler_params=_params(2),
        name="diff_attention",
    )(qk, qk, v, tables, lam_vecs, subln_g, w_down)


def _out_kernel(a_ref, c_ref, w_ref, x_ref, g_ref, xo_ref, h_ref, wbf_ref, *, n_pro):
    s = pl.program_id(0)
    ck = w_ref.shape[0]

    @pl.when(s < n_pro)
    def _():
        wbf_ref[pl.ds(pl.multiple_of(s * ck, ck), ck), :] = w_ref[...].astype(BF16)

    @pl.when(s >= n_pro)
    def _():
        aw = a_ref.shape[1]
        acc = jnp.dot(a_ref[...], wbf_ref[0:aw, :], preferred_element_type=F32)
        acc = acc + jnp.dot(c_ref[...], wbf_ref[aw:, :], preferred_element_type=F32)
        xn = x_ref[...] + acc
        xo_ref[...] = xn
        ms = jnp.mean(xn * xn, axis=-1, keepdims=True)
        h_ref[...] = (xn * lax.rsqrt(ms + EPS) * g_ref[...]).astype(h_ref.dtype)


def _out_proj(attn, conv, w_out, layer, x, g):
    m, d = x.shape
    aw, cwid = attn.shape[1], conv.shape[1]
    bm = _tile(m, 512)
    ck = _tile(aw + cwid, 512)
    n_pro = (aw + cwid) // ck

    def row(s):
        return jnp.maximum(s - n_pro, 0)

    return pl.pallas_call(
        functools.partial(_out_kernel, n_pro=n_pro),
        grid=(n_pro + m // bm,),
        in_specs=[pl.BlockSpec((bm, aw), lambda s: (row(s), 0)),
                  pl.BlockSpec((bm, cwid), lambda s: (row(s), 0)),
                  pl.BlockSpec((None, ck, d), lambda s: (layer, jnp.minimum(s, n_pro - 1), 0)),
                  pl.BlockSpec((bm, d), lambda s: (row(s), 0)),
                  pl.BlockSpec((1, d), lambda s: (0, 0))],
        out_specs=[pl.BlockSpec((bm, d), lambda s: (row(s), 0)),
                   pl.BlockSpec((bm, d), lambda s: (row(s), 0))],
        out_shape=[jax.ShapeDtypeStruct((m, d), F32), jax.ShapeDtypeStruct((m, d), BF16)],
        scratch_shapes=[pltpu.VMEM((aw + cwid, d), BF16)],
        compiler_params=_params(1),
        name="out_proj",
    )(attn, conv, w_out, x, g.reshape(1, d))


def _up_kernel(h_ref, w_ref, o_ref, wbf_ref):
    @pl.when(pl.program_id(1) == 0)
    def _():
        _cast_weight(w_ref, wbf_ref)

    acc = jnp.dot(h_ref[...], wbf_ref[...], preferred_element_type=F32)
    o_ref[...] = jnp.square(jnp.maximum(acc, 0.0)).astype(o_ref.dtype)


def _up_proj(h, w_up, layer):
    m, d = h.shape
    f = w_up.shape[2]
    bm = _tile(m, 2048)
    bn = _tile(f, 1024)
    return pl.pallas_call(
        _up_kernel,
        grid=(f // bn, m // bm),
        in_specs=[pl.BlockSpec((bm, d), lambda j, i: (i, 0)),
                  pl.BlockSpec((None, d, bn), lambda j, i: (layer, 0, j))],
        out_specs=pl.BlockSpec((bm, bn), lambda j, i: (i, j)),
        out_shape=jax.ShapeDtypeStruct((m, f), BF16),
        scratch_shapes=[pltpu.VMEM((d, bn), BF16)],
        compiler_params=_params(2),
        name="up_proj",
    )(h, w_up)


def _down_kernel(a_ref, w_ref, x_ref, *rest, with_norm):
    if with_norm:
        g_ref, xo_ref, h_ref = rest
    else:
        (xo_ref,) = rest
    k = pl.program_id(1)
    last = pl.num_programs(1) - 1

    @pl.when(k == 0)
    def _():
        xo_ref[...] = x_ref[...]

    if not with_norm:
        xo_ref[...] += jnp.dot(a_ref[...], w_ref[...], preferred_element_type=F32)
        return

    @pl.when(k < last)
    def _():
        xo_ref[...] += jnp.dot(a_ref[...], w_ref[...], preferred_element_type=F32)

    @pl.when(k == last)
    def _():
        xn = xo_ref[...] + jnp.dot(a_ref[...], w_ref[...], preferred_element_type=F32)
        xo_ref[...] = xn
        ms = jnp.mean(xn * xn, axis=-1, keepdims=True)
        h_ref[...] = (xn * lax.rsqrt(ms + EPS) * g_ref[...]).astype(h_ref.dtype)


def _down_proj(a, w_down_bf, x, g_next):
    m, d = x.shape
    f = a.shape[1]
    bm = _tile(m, 512)
    bk = _tile(f, 2048)
    with_norm = g_next is not None
    row = pl.BlockSpec((bm, d), lambda i, k: (i, 0))
    in_specs = [pl.BlockSpec((bm, bk), lambda i, k: (i, k)),
                pl.BlockSpec((bk, d), lambda i, k: (k, 0)),
                row]
    args = [a, w_down_bf, x]
    out_specs = [row]
    out_shape = [jax.ShapeDtypeStruct((m, d), F32)]
    if with_norm:
        in_specs.append(pl.BlockSpec((1, d), lambda i, k: (0, 0)))
        args.append(g_next.reshape(1, d))
        out_specs.append(row)
        out_shape.append(jax.ShapeDtypeStruct((m, d), BF16))
    outs = pl.pallas_call(
        functools.partial(_down_kernel, with_norm=with_norm),
        grid=(m // bm, f // bk),
        in_specs=in_specs,
        out_specs=out_specs,
        out_shape=out_shape,
        compiler_params=_params(2),
        name="down_proj",
    )(*args)
    return (outs[0], outs[1]) if with_norm else (outs[0], None)


def kernel(x, w_in, w_out, conv_w, q_norm_g, k_norm_g, lambda_q1, lambda_k1, lambda_q2, lambda_k2,
           subln_g, attn_norm_g, mlp_norm_g, w_up, w_down, rel_bias):
    batch, seq, d = x.shape
    depth = w_in.shape[0]
    aw = d // 2
    cwid = d - aw
    n_heads = aw // HEAD_DV
    assert w_in.shape[2] == 3 * aw + 3 * cwid and rel_bias.shape == (NUM_BUCKETS, 2 * n_heads)
    t = _tile(seq, 256)

    tables = _bias_tables(rel_bias, n_heads, t)
    scale = HEAD_DK ** -0.5 * LOG2E
    xf = x.reshape(batch * seq, d)
    h = _rmsnorm(xf, attn_norm_g[0])
    for l in range(depth):
        lam_init = 0.8 - 0.6 * math.exp(-0.3 * l)
        qk_gains = jnp.stack([jnp.tile(q_norm_g[l] * scale, 2), jnp.tile(k_norm_g[l], 2)])
        lam_vecs = jnp.stack([lambda_q1[l], lambda_k1[l], lambda_q2[l], lambda_k2[l]])

        qk = _qk_proj(h, w_in, l, qk_gains.reshape(2, 1, LANES), aw)
        v = _dense(h, w_in, l, 2 * aw, aw, "v_proj")
        conv = _conv_proj(h, w_in, conv_w, l, aw, cwid, seq)
        attn, w_down_bf = _attention(qk, v, tables, lam_vecs, subln_g[l].reshape(1, HEAD_DV), lam_init,
                                     w_down, l, batch, seq, n_heads, t)
        xf, hm = _out_proj(attn, conv, w_out, l, xf, mlp_norm_g[l])
        act = _up_proj(hm, w_up, l)
        xf, h = _down_proj(act, w_down_bf, xf, attn_norm_g[l + 1] if l + 1 < depth else None)
    return xf.reshape(batch, seq, d)
```

```python
import functools
import math

import numpy as np
import jax
import jax.numpy as jnp
from jax import lax
from jax.experimental import pallas as pl
from jax.experimental.pallas import tpu as pltpu

HEAD_DV = 128
HEAD_DK = HEAD_DV // 2
CONV_K = 3
NUM_BUCKETS = 32
MAX_DISTANCE = 128
MAX_EXACT = NUM_BUCKETS // 2
EPS = 1e-6
NEG = -1e30
LOG2E = math.log2(math.e)

LANES = 128
SUBLANES = 8
VMEM_LIMIT_BYTES = 60 * 1024 * 1024

F32 = jnp.float32
BF16 = jnp.bfloat16


def _tile(dim, pref):
    if dim <= pref:
        return dim
    t = (pref // LANES) * LANES
    while dim % t:
        t -= LANES
    return t


def _params(n_axes):
    return pltpu.CompilerParams(
        dimension_semantics=("arbitrary",) * n_axes,
        vmem_limit_bytes=VMEM_LIMIT_BYTES,
    )


def _rmsnorm_kernel(x_ref, g_ref, o_ref):
    x = x_ref[...]
    ms = jnp.mean(x * x, axis=-1, keepdims=True)
    o_ref[...] = (x * lax.rsqrt(ms + EPS) * g_ref[...]).astype(o_ref.dtype)


def _rmsnorm(x, g):
    m, d = x.shape
    tm = _tile(m, 512)
    return pl.pallas_call(
        _rmsnorm_kernel,
        grid=(m // tm,),
        in_specs=[pl.BlockSpec((tm, d), lambda i: (i, 0)),
                  pl.BlockSpec((1, d), lambda i: (0, 0))],
        out_specs=pl.BlockSpec((tm, d), lambda i: (i, 0)),
        out_shape=jax.ShapeDtypeStruct((m, d), BF16),
        compiler_params=_params(1),
        name="rmsnorm",
    )(x, g.reshape(1, d))


def _cast_weight(w_ref, wbf_ref):
    k = w_ref.shape[0]
    ck = _tile(k, 256)
    for r in range(0, k, ck):
        wbf_ref[r:r + ck, :] = w_ref[r:r + ck, :].astype(BF16)


def _segment_ones(width):
    seg = np.arange(width) // HEAD_DK
    return jnp.asarray(seg[:, None] == seg[None, :], dtype=BF16)


def _qk_kernel(h_ref, w_ref, g_ref, e_ref, wd_ref, o_ref, wdo_ref, wbf_ref):
    @pl.when(pl.program_id(1) == 0)
    def _():
        _cast_weight(w_ref, wbf_ref)

    wdo_ref[...] = wd_ref[...].astype(BF16)
    acc = jnp.dot(h_ref[...], wbf_ref[...], preferred_element_type=F32)
    g = g_ref[...]
    gw = e_ref.shape[0]
    for c in range(0, acc.shape[1], gw):
        a = acc[:, c:c + gw]
        ssq = jnp.dot((a * a).astype(BF16), e_ref[...], preferred_element_type=F32)
        inv = lax.rsqrt(ssq * (1.0 / HEAD_DK) + EPS)
        for cc in range(0, gw, LANES):
            o_ref[:, c + cc:c + cc + LANES] = (
                a[:, cc:cc + LANES] * inv[:, cc:cc + LANES] * g).astype(o_ref.dtype)


def _qk_proj(h, w_in, layer, gains, aw, w_down):
    m, d = h.shape
    bm = _tile(m, 1024)
    nm = m // bm
    gw = _tile(aw, 2 * LANES)
    f = w_down.shape[1]
    slab = f // (2 * nm)
    assert f % (2 * nm) == 0 and slab % (2 * SUBLANES) == 0
    return pl.pallas_call(
        _qk_kernel,
        grid=(2, nm),
        in_specs=[pl.BlockSpec((bm, d), lambda j, i: (i, 0)),
                  pl.BlockSpec((None, d, aw), lambda j, i: (layer, 0, j)),
                  pl.BlockSpec((None, 1, LANES), lambda j, i: (j, 0, 0)),
                  pl.BlockSpec((gw, gw), lambda j, i: (0, 0)),
                  pl.BlockSpec((None, slab, d), lambda j, i: (layer, j * nm + i, 0))],
        out_specs=[pl.BlockSpec((bm, aw), lambda j, i: (i, j)),
                   pl.BlockSpec((slab, d), lambda j, i: (j * nm + i, 0))],
        out_shape=[jax.ShapeDtypeStruct((m, 2 * aw), BF16), jax.ShapeDtypeStruct((f, d), BF16)],
        scratch_shapes=[pltpu.VMEM((d, aw), BF16)],
        compiler_params=_params(2),
        name="qk_proj",
    )(h, w_in, gains, _segment_ones(gw), w_down)


def _dense_kernel(h_ref, w_ref, o_ref, wbf_ref):
    @pl.when(pl.program_id(1) == 0)
    def _():
        _cast_weight(w_ref, wbf_ref)

    o_ref[...] = jnp.dot(h_ref[...], wbf_ref[...], preferred_element_type=F32).astype(o_ref.dtype)


def _dense(h, w, layer, col0, ncols, name):
    m, d = h.shape
    bm = _tile(m, 1024)
    bn = _tile(ncols, 1024)
    assert col0 % bn == 0
    return pl.pallas_call(
        _dense_kernel,
        grid=(ncols // bn, m // bm),
        in_specs=[pl.BlockSpec((bm, d), lambda j, i: (i, 0)),
                  pl.BlockSpec((None, d, bn), lambda j, i: (layer, 0, col0 // bn + j))],
        out_specs=pl.BlockSpec((bm, bn), lambda j, i: (i, j)),
        out_shape=jax.ShapeDtypeStruct((m, ncols), BF16),
        scratch_shapes=[pltpu.VMEM((d, bn), BF16)],
        compiler_params=_params(2),
        name=name,
    )(h, w)


def _conv_kernel(h_ref, wb_ref, wc_ref, wi_ref, cw_ref, o_ref, wbf_ref, ubuf_ref, *, tiles_per_seq):
    i = pl.program_id(1)
    bm = h_ref.shape[0]

    @pl.when(i == 0)
    def _():
        _cast_weight(wb_ref, wbf_ref.at[0])
        _cast_weight(wc_ref, wbf_ref.at[1])
        _cast_weight(wi_ref, wbf_ref.at[2])

    @pl.when(i % tiles_per_seq == 0)
    def _():
        ubuf_ref[0:SUBLANES, :] = jnp.zeros((SUBLANES, ubuf_ref.shape[1]), F32)

    h = h_ref[...]
    gate_c = jnp.dot(h, wbf_ref[1], preferred_element_type=F32)
    conv_in = jnp.dot(h, wbf_ref[2], preferred_element_type=F32)
    ubuf_ref[SUBLANES:SUBLANES + bm, :] = gate_c * conv_in
    gate_b = jnp.dot(h, wbf_ref[0], preferred_element_type=F32)
    cw = cw_ref[...]
    conv = (cw[0:1] * ubuf_ref[SUBLANES - 2:SUBLANES - 2 + bm, :]
            + cw[1:2] * ubuf_ref[SUBLANES - 1:SUBLANES - 1 + bm, :]
            + cw[2:3] * ubuf_ref[SUBLANES:SUBLANES + bm, :])
    o_ref[...] = (gate_b * conv).astype(o_ref.dtype)
    ubuf_ref[0:SUBLANES, :] = ubuf_ref[bm:bm + SUBLANES, :]


def _conv_proj(h, w_in, conv_w, layer, aw, cwid, seq):
    m, d = h.shape
    bm = _tile(seq, 1024)
    bn = _tile(cwid, 512)
    off = 3 * aw // bn
    nb = cwid // bn
    return pl.pallas_call(
        functools.partial(_conv_kernel, tiles_per_seq=seq // bm),
        grid=(nb, m // bm),
        in_specs=[pl.BlockSpec((bm, d), lambda j, i: (i, 0)),
                  pl.BlockSpec((None, d, bn), lambda j, i: (layer, 0, off + j)),
                  pl.BlockSpec((None, d, bn), lambda j, i: (layer, 0, off + nb + j)),
                  pl.BlockSpec((None, d, bn), lambda j, i: (layer, 0, off + 2 * nb + j)),
                  pl.BlockSpec((None, CONV_K, bn), lambda j, i: (layer, 0, j))],
        out_specs=pl.BlockSpec((bm, bn), lambda j, i: (i, j)),
        out_shape=jax.ShapeDtypeStruct((m, cwid), BF16),
        scratch_shapes=[pltpu.VMEM((3, d, bn), BF16),
                        pltpu.VMEM((bm + SUBLANES, bn), F32)],
        compiler_params=_params(2),
        name="conv_proj",
    )(h, w_in, w_in, w_in, conv_w)


def _bucket(n):
    nf = np.maximum(n, MAX_EXACT).astype(np.float64)
    val = np.log(nf / MAX_EXACT) / math.log(MAX_DISTANCE / MAX_EXACT) * (NUM_BUCKETS - MAX_EXACT)
    large = np.minimum(MAX_EXACT + val.astype(np.int64), NUM_BUCKETS - 1)
    return np.where(n < MAX_EXACT, n, large)


def _bucket_tables(t):
    assert _bucket(np.array([t + 1]))[0] == NUM_BUCKETS - 1
    r = np.arange(t)[:, None]
    c = np.arange(t)[None, :]
    d0 = r - c
    diag = np.where(d0 >= 0, _bucket(np.maximum(d0, 0)), -1)
    prev = _bucket(t + r - c)
    return np.stack([diag, prev]).astype(np.int32)


def _bias_table_kernel(rb_ref, bk_ref, o_ref):
    h = pl.program_id(0)
    for mode in range(2):
        bk = bk_ref[mode]
        for mp in range(2):
            col = 2 * h + mp
            far = rb_ref[NUM_BUCKETS - 1, col]
            acc = jnp.where(bk < 0, NEG, 0.0).astype(F32)
            for b in range(NUM_BUCKETS - 1):
                acc = jnp.where(bk == b, (rb_ref[b, col] - far) * LOG2E, acc)
            o_ref[0, mode, mp] = acc


def _bias_tables(rel_bias, n_heads, t):
    buckets = jnp.asarray(_bucket_tables(t))
    return pl.pallas_call(
        _bias_table_kernel,
        grid=(n_heads,),
        in_specs=[pl.BlockSpec(memory_space=pltpu.SMEM),
                  pl.BlockSpec((2, t, t), lambda h: (0, 0, 0))],
        out_specs=pl.BlockSpec((1, 2, 2, t, t), lambda h: (h, 0, 0, 0, 0)),
        out_shape=jax.ShapeDtypeStruct((n_heads, 2, 2, t, t), F32),
        compiler_params=_params(1),
        name="bias_tables",
    )(rel_bias, buckets)


def _fold_lanes(x, op):
    return functools.reduce(op, [x[:, c:c + LANES] for c in range(0, x.shape[1], LANES)])


def _attn_kernel(q_ref, k_ref, v_ref, tb_ref, lam_ref, g_ref, o_ref, v1_ref, *, lam_init, t):
    seq = q_ref.shape[0]
    heads = q_ref.shape[1] // HEAD_DV
    for hh in range(heads):
        v1_ref[hh, :, 0:HEAD_DV] = v_ref[:, hh * HEAD_DV:(hh + 1) * HEAD_DV]
        v1_ref[hh, :, HEAD_DV:] = jnp.ones((seq, LANES), BF16)
    lo = lax.broadcasted_iota(jnp.int32, (1, LANES), 1) < HEAD_DK
    lv = lam_ref[...]
    lam = (jnp.exp(jnp.sum(lv[0:1] * lv[1:2], axis=-1, keepdims=True))
           - jnp.exp(jnp.sum(lv[2:3] * lv[3:4], axis=-1, keepdims=True)) + lam_init)
    g = g_ref[...]
    contract_last = (((1,), (1,)), ((), ()))

    def tile(hh, i):
        cols = slice(hh * HEAD_DV, (hh + 1) * HEAD_DV)
        q = q_ref[i * t:(i + 1) * t, cols]
        zero = jnp.zeros_like(q)
        q2 = jnp.concatenate([jnp.where(lo, q, zero), jnp.where(lo, zero, q)], axis=0)

        pieces = []
        if i >= 2:
            pieces.append((0, (i - 1) * t, None))
        if i >= 1:
            pieces.append(((i - 1) * t, t, 1))
        pieces.append((i * t, t, 0))

        scores = []
        for start, width, table in pieces:
            s = lax.dot_general(q2, k_ref[start:start + width, cols], contract_last,
                                preferred_element_type=F32)
            if table is not None:
                s = s + tb_ref[hh, table].reshape(2 * t, t)
            scores.append(s)
        m = functools.reduce(jnp.maximum, [_fold_lanes(s, jnp.maximum) for s in scores])
        m = jnp.max(m, axis=-1, keepdims=True)
        acc = None
        for s, (start, width, _) in zip(scores, pieces):
            p = jnp.exp2(s - m).astype(BF16)
            pv = jnp.dot(p, v1_ref[hh, start:start + width, :], preferred_element_type=F32)
            acc = pv if acc is None else acc + pv
        out = acc[:, 0:HEAD_DV] / acc[:, HEAD_DV:]
        o = out[0:t] - lam * out[t:2 * t]
        ms = jnp.mean(o * o, axis=-1, keepdims=True)
        o_ref[i * t:(i + 1) * t, cols] = (o * lax.rsqrt(ms + EPS) * g * (1.0 - lam_init)).astype(o_ref.dtype)

    nq = seq // t
    for i in [0] + list(range(nq - 1, 0, -1)):
        for hh in range(heads):
            tile(hh, i)


def _attention(qk, v, tables, lam_vecs, subln_g, lam_init, batch, seq, n_heads, t):
    hps = 2 if n_heads % 2 == 0 else 1
    w = hps * HEAD_DV
    ng = n_heads // hps
    return pl.pallas_call(
        functools.partial(_attn_kernel, lam_init=lam_init, t=t),
        grid=(batch, ng),
        in_specs=[pl.BlockSpec((seq, w), lambda b, h: (b, h)),
                  pl.BlockSpec((seq, w), lambda b, h: (b, ng + h)),
                  pl.BlockSpec((seq, w), lambda b, h: (b, h)),
                  pl.BlockSpec((hps, 2, 2, t, t), lambda b, h: (h, 0, 0, 0, 0)),
                  pl.BlockSpec((4, HEAD_DK), lambda b, h: (0, 0)),
                  pl.BlockSpec((1, HEAD_DV), lambda b, h: (0, 0))],
        out_specs=pl.BlockSpec((seq, w), lambda b, h: (b, h)),
        out_shape=jax.ShapeDtypeStruct((batch * seq, n_heads * HEAD_DV), BF16),
        scratch_shapes=[pltpu.VMEM((hps, seq, HEAD_DV + LANES), BF16)],
        compiler_params=_params(2),
        name="diff_attention",
    )(qk, qk, v, tables, lam_vecs, subln_g)


def _out_kernel(a_ref, c_ref, w_ref, x_ref, g_ref, xo_ref, h_ref, wbf_ref, *, n_pro):
    s = pl.program_id(0)
    ck = w_ref.shape[0]

    @pl.when(s < n_pro)
    def _():
        wbf_ref[pl.ds(pl.multiple_of(s * ck, ck), ck), :] = w_ref[...].astype(BF16)

    @pl.when(s >= n_pro)
    def _():
        aw = a_ref.shape[1]
        acc = jnp.dot(a_ref[...], wbf_ref[0:aw, :], preferred_element_type=F32)
        acc = acc + jnp.dot(c_ref[...], wbf_ref[aw:, :], preferred_element_type=F32)
        xn = x_ref[...] + acc
        xo_ref[...] = xn
        ms = jnp.mean(xn * xn, axis=-1, keepdims=True)
        h_ref[...] = (xn * lax.rsqrt(ms + EPS) * g_ref[...]).astype(h_ref.dtype)


def _out_proj(attn, conv, w_out, layer, x, g):
    m, d = x.shape
    aw, cwid = attn.shape[1], conv.shape[1]
    bm = _tile(m, 512)
    ck = _tile(aw + cwid, 512)
    n_pro = (aw + cwid) // ck

    def row(s):
        return jnp.maximum(s - n_pro, 0)

    return pl.pallas_call(
        functools.partial(_out_kernel, n_pro=n_pro),
        grid=(n_pro + m // bm,),
        in_specs=[pl.BlockSpec((bm, aw), lambda s: (row(s), 0)),
                  pl.BlockSpec((bm, cwid), lambda s: (row(s), 0)),
                  pl.BlockSpec((None, ck, d), lambda s: (layer, jnp.minimum(s, n_pro - 1), 0)),
                  pl.BlockSpec((bm, d), lambda s: (row(s), 0)),
                  pl.BlockSpec((1, d), lambda s: (0, 0))],
        out_specs=[pl.BlockSpec((bm, d), lambda s: (row(s), 0)),
                   pl.BlockSpec((bm, d), lambda s: (row(s), 0))],
        out_shape=[jax.ShapeDtypeStruct((m, d), F32), jax.ShapeDtypeStruct((m, d), BF16)],
        scratch_shapes=[pltpu.VMEM((aw + cwid, d), BF16)],
        compiler_params=_params(1),
        name="out_proj",
    )(attn, conv, w_out, x, g.reshape(1, d))


def _up_kernel(h_ref, w_ref, o_ref, wbf_ref):
    @pl.when(pl.program_id(1) == 0)
    def _():
        _cast_weight(w_ref, wbf_ref)

    acc = jnp.dot(h_ref[...], wbf_ref[...], preferred_element_type=F32)
    o_ref[...] = jnp.square(jnp.maximum(acc, 0.0)).astype(o_ref.dtype)


def _up_proj(h, w_up, layer):
    m, d = h.shape
    f = w_up.shape[2]
    bm = _tile(m, 2048)
    bn = _tile(f, 1024)
    return pl.pallas_call(
        _up_kernel,
        grid=(f // bn, m // bm),
        in_specs=[pl.BlockSpec((bm, d), lambda j, i: (i, 0)),
                  pl.BlockSpec((None, d, bn), lambda j, i: (layer, 0, j))],
        out_specs=pl.BlockSpec((bm, bn), lambda j, i: (i, j)),
        out_shape=jax.ShapeDtypeStruct((m, f), BF16),
        scratch_shapes=[pltpu.VMEM((d, bn), BF16)],
        compiler_params=_params(2),
        name="up_proj",
    )(h, w_up)


def _down_kernel(a_ref, w_ref, x_ref, *rest, with_norm):
    if with_norm:
        g_ref, xo_ref, h_ref = rest
    else:
        (xo_ref,) = rest
    k = pl.program_id(1)
    last = pl.num_programs(1) - 1

    @pl.when(k == 0)
    def _():
        xo_ref[...] = x_ref[...]

    if not with_norm:
        xo_ref[...] += jnp.dot(a_ref[...], w_ref[...], preferred_element_type=F32)
        return

    @pl.when(k < last)
    def _():
        xo_ref[...] += jnp.dot(a_ref[...], w_ref[...], preferred_element_type=F32)

    @pl.when(k == last)
    def _():
        xn = xo_ref[...] + jnp.dot(a_ref[...], w_ref[...], preferred_element_type=F32)
        xo_ref[...] = xn
        ms = jnp.mean(xn * xn, axis=-1, keepdims=True)
        h_ref[...] = (xn * lax.rsqrt(ms + EPS) * g_ref[...]).astype(h_ref.dtype)


def _down_proj(a, w_down_bf, x, g_next):
    m, d = x.shape
    f = a.shape[1]
    bm = _tile(m, 512)
    bk = _tile(f, 2048)
    with_norm = g_next is not None
    row = pl.BlockSpec((bm, d), lambda i, k: (i, 0))
    in_specs = [pl.BlockSpec((bm, bk), lambda i, k: (i, k)),
                pl.BlockSpec((bk, d), lambda i, k: (k, 0)),
                row]
    args = [a, w_down_bf, x]
    out_specs = [row]
    out_shape = [jax.ShapeDtypeStruct((m, d), F32)]
    if with_norm:
        in_specs.append(pl.BlockSpec((1, d), lambda i, k: (0, 0)))
        args.append(g_next.reshape(1, d))
        out_specs.append(row)
        out_shape.append(jax.ShapeDtypeStruct((m, d), BF16))
    outs = pl.pallas_call(
        functools.partial(_down_kernel, with_norm=with_norm),
        grid=(m // bm, f // bk),
        in_specs=in_specs,
        out_specs=out_specs,
        out_shape=out_shape,
        compiler_params=_params(2),
        name="down_proj",
    )(*args)
    return (outs[0], outs[1]) if with_norm else (outs[0], None)


def kernel(x, w_in, w_out, conv_w, q_norm_g, k_norm_g, lambda_q1, lambda_k1, lambda_q2, lambda_k2,
           subln_g, attn_norm_g, mlp_norm_g, w_up, w_down, rel_bias):
    batch, seq, d = x.shape
    depth = w_in.shape[0]
    aw = d // 2
    cwid = d - aw
    n_heads = aw // HEAD_DV
    assert w_in.shape[2] == 3 * aw + 3 * cwid and rel_bias.shape == (NUM_BUCKETS, 2 * n_heads)
    t = _tile(seq, 256)

    tables = _bias_tables(rel_bias, n_heads, t)
    scale = HEAD_DK ** -0.5 * LOG2E
    xf = x.reshape(batch * seq, d)
    h = _rmsnorm(xf, attn_norm_g[0])
    for l in range(depth):
        lam_init = 0.8 - 0.6 * math.exp(-0.3 * l)
        qk_gains = jnp.stack([jnp.tile(q_norm_g[l] * scale, 2), jnp.tile(k_norm_g[l], 2)])
        lam_vecs = jnp.stack([lambda_q1[l], lambda_k1[l], lambda_q2[l], lambda_k2[l]])

        qk, w_down_bf = _qk_proj(h, w_in, l, qk_gains.reshape(2, 1, LANES), aw, w_down)
        v = _dense(h, w_in, l, 2 * aw, aw, "v_proj")
        conv = _conv_proj(h, w_in, conv_w, l, aw, cwid, seq)
        attn = _attention(qk, v, tables, lam_vecs, subln_g[l].reshape(1, HEAD_DV), lam_init,
                          batch, seq, n_heads, t)
        xf, hm = _out_proj(attn, conv, w_out, l, xf, mlp_norm_g[l])
        act = _up_proj(hm, w_up, l)
        xf, h = _down_proj(act, w_down_bf, xf, attn_norm_g[l + 1] if l + 1 < depth else None)
    return xf.reshape(batch, seq, d)
```

```python
import functools
import math

import numpy as np
import jax
import jax.numpy as jnp
from jax import lax
from jax.experimental import pallas as pl
from jax.experimental.pallas import tpu as pltpu

HEAD_DV = 128
HEAD_DK = HEAD_DV // 2
CONV_K = 3
NUM_BUCKETS = 32
MAX_DISTANCE = 128
MAX_EXACT = NUM_BUCKETS // 2
EPS = 1e-6
NEG = -1e30
LOG2E = math.log2(math.e)

LANES = 128
SUBLANES = 8
VMEM_LIMIT_BYTES = 60 * 1024 * 1024

F32 = jnp.float32
BF16 = jnp.bfloat16


def _tile(dim, pref):
    if dim <= pref:
        return dim
    t = (pref // LANES) * LANES
    while dim % t:
        t -= LANES
    return t


def _params(n_axes):
    return pltpu.CompilerParams(
        dimension_semantics=("arbitrary",) * n_axes,
        vmem_limit_bytes=VMEM_LIMIT_BYTES,
    )


def _rmsnorm_kernel(x_ref, g_ref, o_ref):
    x = x_ref[...]
    ms = jnp.mean(x * x, axis=-1, keepdims=True)
    o_ref[...] = (x * lax.rsqrt(ms + EPS) * g_ref[...]).astype(o_ref.dtype)


def _rmsnorm(x, g):
    m, d = x.shape
    tm = _tile(m, 512)
    return pl.pallas_call(
        _rmsnorm_kernel,
        grid=(m // tm,),
        in_specs=[pl.BlockSpec((tm, d), lambda i: (i, 0)),
                  pl.BlockSpec((1, d), lambda i: (0, 0))],
        out_specs=pl.BlockSpec((tm, d), lambda i: (i, 0)),
        out_shape=jax.ShapeDtypeStruct((m, d), BF16),
        compiler_params=_params(1),
        name="rmsnorm",
    )(x, g.reshape(1, d))


def _cast_weight(w_ref, wbf_ref):
    k = w_ref.shape[0]
    ck = _tile(k, 256)
    for r in range(0, k, ck):
        wbf_ref[r:r + ck, :] = w_ref[r:r + ck, :].astype(BF16)


def _segment_ones(width):
    seg = np.arange(width) // HEAD_DK
    return jnp.asarray(seg[:, None] == seg[None, :], dtype=BF16)


def _qk_kernel(h_ref, w_ref, g_ref, e_ref, wd_ref, o_ref, wdo_ref, wbf_ref):
    @pl.when(pl.program_id(1) == 0)
    def _():
        _cast_weight(w_ref, wbf_ref)

    wdo_ref[...] = wd_ref[...].astype(BF16)
    acc = jnp.dot(h_ref[...], wbf_ref[...], preferred_element_type=F32)
    g = g_ref[...]
    gw = e_ref.shape[0]
    for c in range(0, acc.shape[1], gw):
        a = acc[:, c:c + gw]
        ssq = jnp.dot((a * a).astype(BF16), e_ref[...], preferred_element_type=F32)
        inv = lax.rsqrt(ssq * (1.0 / HEAD_DK) + EPS)
        for cc in range(0, gw, LANES):
            o_ref[:, c + cc:c + cc + LANES] = (
                a[:, cc:cc + LANES] * inv[:, cc:cc + LANES] * g).astype(o_ref.dtype)


def _qk_proj(h, w_in, layer, gains, aw, w_down):
    m, d = h.shape
    bm = _tile(m, 1024)
    nm = m // bm
    gw = _tile(aw, 2 * LANES)
    f = w_down.shape[1]
    slab = f // (2 * nm)
    assert f % (2 * nm) == 0 and slab % (2 * SUBLANES) == 0
    return pl.pallas_call(
        _qk_kernel,
        grid=(2, nm),
        in_specs=[pl.BlockSpec((bm, d), lambda j, i: (i, 0)),
                  pl.BlockSpec((None, d, aw), lambda j, i: (layer, 0, j)),
                  pl.BlockSpec((None, 1, LANES), lambda j, i: (j, 0, 0)),
                  pl.BlockSpec((gw, gw), lambda j, i: (0, 0)),
                  pl.BlockSpec((None, slab, d), lambda j, i: (layer, j * nm + i, 0))],
        out_specs=[pl.BlockSpec((bm, aw), lambda j, i: (i, j)),
                   pl.BlockSpec((slab, d), lambda j, i: (j * nm + i, 0))],
        out_shape=[jax.ShapeDtypeStruct((m, 2 * aw), BF16), jax.ShapeDtypeStruct((f, d), BF16)],
        scratch_shapes=[pltpu.VMEM((d, aw), BF16)],
        compiler_params=_params(2),
        name="qk_proj",
    )(h, w_in, gains, _segment_ones(gw), w_down)


def _dense_kernel(h_ref, w_ref, o_ref, wbf_ref):
    @pl.when(pl.program_id(1) == 0)
    def _():
        _cast_weight(w_ref, wbf_ref)

    o_ref[...] = jnp.dot(h_ref[...], wbf_ref[...], preferred_element_type=F32).astype(o_ref.dtype)


def _dense(h, w, layer, col0, ncols, name):
    m, d = h.shape
    bm = _tile(m, 2048)
    bn = _tile(ncols, 1024)
    assert col0 % bn == 0
    return pl.pallas_call(
        _dense_kernel,
        grid=(ncols // bn, m // bm),
        in_specs=[pl.BlockSpec((bm, d), lambda j, i: (i, 0)),
                  pl.BlockSpec((None, d, bn), lambda j, i: (layer, 0, col0 // bn + j))],
        out_specs=pl.BlockSpec((bm, bn), lambda j, i: (i, j)),
        out_shape=jax.ShapeDtypeStruct((m, ncols), BF16),
        scratch_shapes=[pltpu.VMEM((d, bn), BF16)],
        compiler_params=_params(2),
        name=name,
    )(h, w)


def _conv_kernel(h_ref, wb_ref, wc_ref, wi_ref, cw_ref, o_ref, wbf_ref, ubuf_ref, *, tiles_per_seq):
    i = pl.program_id(1)
    bm = h_ref.shape[0]

    @pl.when(i == 0)
    def _():
        _cast_weight(wb_ref, wbf_ref.at[0])
        _cast_weight(wc_ref, wbf_ref.at[1])
        _cast_weight(wi_ref, wbf_ref.at[2])

    @pl.when(i % tiles_per_seq == 0)
    def _():
        ubuf_ref[0:SUBLANES, :] = jnp.zeros((SUBLANES, ubuf_ref.shape[1]), F32)

    h = h_ref[...]
    gate_c = jnp.dot(h, wbf_ref[1], preferred_element_type=F32)
    conv_in = jnp.dot(h, wbf_ref[2], preferred_element_type=F32)
    ubuf_ref[SUBLANES:SUBLANES + bm, :] = gate_c * conv_in
    gate_b = jnp.dot(h, wbf_ref[0], preferred_element_type=F32)
    cw = cw_ref[...]
    conv = (cw[0:1] * ubuf_ref[SUBLANES - 2:SUBLANES - 2 + bm, :]
            + cw[1:2] * ubuf_ref[SUBLANES - 1:SUBLANES - 1 + bm, :]
            + cw[2:3] * ubuf_ref[SUBLANES:SUBLANES + bm, :])
    o_ref[...] = (gate_b * conv).astype(o_ref.dtype)
    ubuf_ref[0:SUBLANES, :] = ubuf_ref[bm:bm + SUBLANES, :]


def _conv_proj(h, w_in, conv_w, layer, aw, cwid, seq):
    m, d = h.shape
    bm = _tile(seq, 1024)
    bn = _tile(cwid, 512)
    off = 3 * aw // bn
    nb = cwid // bn
    return pl.pallas_call(
        functools.partial(_conv_kernel, tiles_per_seq=seq // bm),
        grid=(nb, m // bm),
        in_specs=[pl.BlockSpec((bm, d), lambda j, i: (i, 0)),
                  pl.BlockSpec((None, d, bn), lambda j, i: (layer, 0, off + j)),
                  pl.BlockSpec((None, d, bn), lambda j, i: (layer, 0, off + nb + j)),
                  pl.BlockSpec((None, d, bn), lambda j, i: (layer, 0, off + 2 * nb + j)),
                  pl.BlockSpec((None, CONV_K, bn), lambda j, i: (layer, 0, j))],
        out_specs=pl.BlockSpec((bm, bn), lambda j, i: (i, j)),
        out_shape=jax.ShapeDtypeStruct((m, cwid), BF16),
        scratch_shapes=[pltpu.VMEM((3, d, bn), BF16),
                        pltpu.VMEM((bm + SUBLANES, bn), F32)],
        compiler_params=_params(2),
        name="conv_proj",
    )(h, w_in, w_in, w_in, conv_w)


def _bucket(n):
    nf = np.maximum(n, MAX_EXACT).astype(np.float64)
    val = np.log(nf / MAX_EXACT) / math.log(MAX_DISTANCE / MAX_EXACT) * (NUM_BUCKETS - MAX_EXACT)
    large = np.minimum(MAX_EXACT + val.astype(np.int64), NUM_BUCKETS - 1)
    return np.where(n < MAX_EXACT, n, large)


def _bucket_tables(t):
    assert _bucket(np.array([t + 1]))[0] == NUM_BUCKETS - 1
    r = np.arange(t)[:, None]
    c = np.arange(t)[None, :]
    d0 = r - c
    diag = np.where(d0 >= 0, _bucket(np.maximum(d0, 0)), -1)
    prev = _bucket(t + r - c)
    return np.stack([diag, prev]).astype(np.int32)


def _bias_table_kernel(rb_ref, bk_ref, o_ref):
    h = pl.program_id(0)
    for mode in range(2):
        bk = bk_ref[mode]
        for mp in range(2):
            col = 2 * h + mp
            far = rb_ref[NUM_BUCKETS - 1, col]
            acc = jnp.where(bk < 0, NEG, 0.0).astype(F32)
            for b in range(NUM_BUCKETS - 1):
                acc = jnp.where(bk == b, (rb_ref[b, col] - far) * LOG2E, acc)
            o_ref[0, mode, mp] = acc


def _bias_tables(rel_bias, n_heads, t):
    buckets = jnp.asarray(_bucket_tables(t))
    return pl.pallas_call(
        _bias_table_kernel,
        grid=(n_heads,),
        in_specs=[pl.BlockSpec(memory_space=pltpu.SMEM),
                  pl.BlockSpec((2, t, t), lambda h: (0, 0, 0))],
        out_specs=pl.BlockSpec((1, 2, 2, t, t), lambda h: (h, 0, 0, 0, 0)),
        out_shape=jax.ShapeDtypeStruct((n_heads, 2, 2, t, t), F32),
        compiler_params=_params(1),
        name="bias_tables",
    )(rel_bias, buckets)


def _fold_lanes(x, op):
    return functools.reduce(op, [x[:, c:c + LANES] for c in range(0, x.shape[1], LANES)])


def _attn_kernel(q_ref, k_ref, v_ref, tb_ref, lam_ref, g_ref, o_ref, v1_ref, *, t):
    seq = q_ref.shape[0]
    heads = q_ref.shape[1] // HEAD_DV
    for hh in range(heads):
        v1_ref[hh, :, 0:HEAD_DV] = v_ref[:, hh * HEAD_DV:(hh + 1) * HEAD_DV]
        v1_ref[hh, :, HEAD_DV:] = jnp.ones((seq, LANES), BF16)
    lo = lax.broadcasted_iota(jnp.int32, (1, LANES), 1) < HEAD_DK
    lv = lam_ref[...]
    lam = (jnp.exp(jnp.sum(lv[0:1] * lv[1:2], axis=-1, keepdims=True))
           - jnp.exp(jnp.sum(lv[2:3] * lv[3:4], axis=-1, keepdims=True)) + lv[4:5, 0:1])
    out_gain = lv[5:6, 0:1]
    g = g_ref[...]
    contract_last = (((1,), (1,)), ((), ()))

    def tile(hh, i):
        cols = slice(hh * HEAD_DV, (hh + 1) * HEAD_DV)
        q = q_ref[i * t:(i + 1) * t, cols]
        zero = jnp.zeros_like(q)
        q2 = jnp.concatenate([jnp.where(lo, q, zero), jnp.where(lo, zero, q)], axis=0)

        pieces = []
        if i >= 2:
            pieces.append((0, (i - 1) * t, None))
        if i >= 1:
            pieces.append(((i - 1) * t, t, 1))
        pieces.append((i * t, t, 0))

        scores = []
        for start, width, table in pieces:
            s = lax.dot_general(q2, k_ref[start:start + width, cols], contract_last,
                                preferred_element_type=F32)
            if table is not None:
                s = s + tb_ref[hh, table].reshape(2 * t, t)
            scores.append(s)
        m = functools.reduce(jnp.maximum, [_fold_lanes(s, jnp.maximum) for s in scores])
        m = jnp.max(m, axis=-1, keepdims=True)
        acc = None
        for s, (start, width, _) in zip(scores, pieces):
            p = jnp.exp2(s - m).astype(BF16)
            pv = jnp.dot(p, v1_ref[hh, start:start + width, :], preferred_element_type=F32)
            acc = pv if acc is None else acc + pv
        out = acc[:, 0:HEAD_DV] / acc[:, HEAD_DV:]
        o = out[0:t] - lam * out[t:2 * t]
        ms = jnp.mean(o * o, axis=-1, keepdims=True)
        o_ref[i * t:(i + 1) * t, cols] = (o * lax.rsqrt(ms + EPS) * g * out_gain).astype(o_ref.dtype)

    nq = seq // t
    for i in [0] + list(range(nq - 1, 0, -1)):
        for hh in range(heads):
            tile(hh, i)


def _attention(qk, v, tables, lam_vecs, subln_g, batch, seq, n_heads, t):
    hps = 4 if n_heads % 4 == 0 else 1
    w = hps * HEAD_DV
    ng = n_heads // hps
    return pl.pallas_call(
        functools.partial(_attn_kernel, t=t),
        grid=(batch, ng),
        in_specs=[pl.BlockSpec((seq, w), lambda b, h: (b, h)),
                  pl.BlockSpec((seq, w), lambda b, h: (b, ng + h)),
                  pl.BlockSpec((seq, w), lambda b, h: (b, h)),
                  pl.BlockSpec((hps, 2, 2, t, t), lambda b, h: (h, 0, 0, 0, 0)),
                  pl.BlockSpec((6, HEAD_DK), lambda b, h: (0, 0)),
                  pl.BlockSpec((1, HEAD_DV), lambda b, h: (0, 0))],
        out_specs=pl.BlockSpec((seq, w), lambda b, h: (b, h)),
        out_shape=jax.ShapeDtypeStruct((batch * seq, n_heads * HEAD_DV), BF16),
        scratch_shapes=[pltpu.VMEM((hps, seq, HEAD_DV + LANES), BF16)],
        compiler_params=_params(2),
        name="diff_attention",
    )(qk, qk, v, tables, lam_vecs, subln_g)


def _out_kernel(a_ref, c_ref, w_ref, x_ref, g_ref, xo_ref, h_ref, wbf_ref, *, n_pro):
    s = pl.program_id(0)
    ck = w_ref.shape[0]

    @pl.when(s < n_pro)
    def _():
        wbf_ref[pl.ds(pl.multiple_of(s * ck, ck), ck), :] = w_ref[...].astype(BF16)

    @pl.when(s >= n_pro)
    def _():
        aw = a_ref.shape[1]
        acc = jnp.dot(a_ref[...], wbf_ref[0:aw, :], preferred_element_type=F32)
        acc = acc + jnp.dot(c_ref[...], wbf_ref[aw:, :], preferred_element_type=F32)
        xn = x_ref[...] + acc
        xo_ref[...] = xn
        ms = jnp.mean(xn * xn, axis=-1, keepdims=True)
        h_ref[...] = (xn * lax.rsqrt(ms + EPS) * g_ref[...]).astype(h_ref.dtype)


def _out_proj(attn, conv, w_out, layer, x, g):
    m, d = x.shape
    aw, cwid = attn.shape[1], conv.shape[1]
    bm = _tile(m, 512)
    ck = _tile(aw + cwid, 512)
    n_pro = (aw + cwid) // ck

    def row(s):
        return jnp.maximum(s - n_pro, 0)

    return pl.pallas_call(
        functools.partial(_out_kernel, n_pro=n_pro),
        grid=(n_pro + m // bm,),
        in_specs=[pl.BlockSpec((bm, aw), lambda s: (row(s), 0)),
                  pl.BlockSpec((bm, cwid), lambda s: (row(s), 0)),
                  pl.BlockSpec((None, ck, d), lambda s: (layer, jnp.minimum(s, n_pro - 1), 0)),
                  pl.BlockSpec((bm, d), lambda s: (row(s), 0)),
                  pl.BlockSpec((1, d), lambda s: (0, 0))],
        out_specs=[pl.BlockSpec((bm, d), lambda s: (row(s), 0)),
                   pl.BlockSpec((bm, d), lambda s: (row(s), 0))],
        out_shape=[jax.ShapeDtypeStruct((m, d), F32), jax.ShapeDtypeStruct((m, d), BF16)],
        scratch_shapes=[pltpu.VMEM((aw + cwid, d), BF16)],
        compiler_params=_params(1),
        name="out_proj",
    )(attn, conv, w_out, x, g.reshape(1, d))


def _up_kernel(h_ref, w_ref, o_ref, wbf_ref):
    @pl.when(pl.program_id(1) == 0)
    def _():
        _cast_weight(w_ref, wbf_ref)

    acc = jnp.dot(h_ref[...], wbf_ref[...], preferred_element_type=F32)
    o_ref[...] = jnp.square(jnp.maximum(acc, 0.0)).astype(o_ref.dtype)


def _up_proj(h, w_up, layer):
    m, d = h.shape
    f = w_up.shape[2]
    bm = _tile(m, 2048)
    bn = _tile(f, 1024)
    return pl.pallas_call(
        _up_kernel,
        grid=(f // bn, m // bm),
        in_specs=[pl.BlockSpec((bm, d), lambda j, i: (i, 0)),
                  pl.BlockSpec((None, d, bn), lambda j, i: (layer, 0, j))],
        out_specs=pl.BlockSpec((bm, bn), lambda j, i: (i, j)),
        out_shape=jax.ShapeDtypeStruct((m, f), BF16),
        scratch_shapes=[pltpu.VMEM((d, bn), BF16)],
        compiler_params=_params(2),
        name="up_proj",
    )(h, w_up)


def _down_kernel(a_ref, w_ref, x_ref, *rest, with_norm):
    if with_norm:
        g_ref, xo_ref, h_ref = rest
    else:
        (xo_ref,) = rest
    k = pl.program_id(1)
    last = pl.num_programs(1) - 1

    @pl.when(k == 0)
    def _():
        xo_ref[...] = x_ref[...]

    if not with_norm:
        xo_ref[...] += jnp.dot(a_ref[...], w_ref[...], preferred_element_type=F32)
        return

    @pl.when(k < last)
    def _():
        xo_ref[...] += jnp.dot(a_ref[...], w_ref[...], preferred_element_type=F32)

    @pl.when(k == last)
    def _():
        xn = xo_ref[...] + jnp.dot(a_ref[...], w_ref[...], preferred_element_type=F32)
        xo_ref[...] = xn
        ms = jnp.mean(xn * xn, axis=-1, keepdims=True)
        h_ref[...] = (xn * lax.rsqrt(ms + EPS) * g_ref[...]).astype(h_ref.dtype)


def _down_proj(a, w_down_bf, x, g_next):
    m, d = x.shape
    f = a.shape[1]
    bm = _tile(m, 1024)
    bk = _tile(f, 1024)
    with_norm = g_next is not None
    row = pl.BlockSpec((bm, d), lambda i, k: (i, 0))
    in_specs = [pl.BlockSpec((bm, bk), lambda i, k: (i, k)),
                pl.BlockSpec((bk, d), lambda i, k: (k, 0)),
                row]
    args = [a, w_down_bf, x]
    out_specs = [row]
    out_shape = [jax.ShapeDtypeStruct((m, d), F32)]
    if with_norm:
        in_specs.append(pl.BlockSpec((1, d), lambda i, k: (0, 0)))
        args.append(g_next.reshape(1, d))
        out_specs.append(row)
        out_shape.append(jax.ShapeDtypeStruct((m, d), BF16))
    outs = pl.pallas_call(
        functools.partial(_down_kernel, with_norm=with_norm),
        grid=(m // bm, f // bk),
        in_specs=in_specs,
        out_specs=out_specs,
        out_shape=out_shape,
        compiler_params=_params(2),
        name="down_proj",
    )(*args)
    return (outs[0], outs[1]) if with_norm else (outs[0], None)


def kernel(x, w_in, w_out, conv_w, q_norm_g, k_norm_g, lambda_q1, lambda_k1, lambda_q2, lambda_k2,
           subln_g, attn_norm_g, mlp_norm_g, w_up, w_down, rel_bias):
    batch, seq, d = x.shape
    depth = w_in.shape[0]
    aw = d // 2
    cwid = d - aw
    n_heads = aw // HEAD_DV
    assert w_in.shape[2] == 3 * aw + 3 * cwid and rel_bias.shape == (NUM_BUCKETS, 2 * n_heads)
    t = _tile(seq, 256)

    tables = _bias_tables(rel_bias, n_heads, t)
    scale = HEAD_DK ** -0.5 * LOG2E
    xf = x.reshape(batch * seq, d)
    h = _rmsnorm(xf, attn_norm_g[0])
    for l in range(depth):
        lam_init = 0.8 - 0.6 * math.exp(-0.3 * l)
        qk_gains = jnp.stack([jnp.tile(q_norm_g[l] * scale, 2), jnp.tile(k_norm_g[l], 2)])
        lam_vecs = jnp.stack([lambda_q1[l], lambda_k1[l], lambda_q2[l], lambda_k2[l],
                              jnp.full((HEAD_DK,), lam_init, F32),
                              jnp.full((HEAD_DK,), 1.0 - lam_init, F32)])

        qk, w_down_bf = _qk_proj(h, w_in, l, qk_gains.reshape(2, 1, LANES), aw, w_down)
        v = _dense(h, w_in, l, 2 * aw, aw, "v_proj")
        conv = _conv_proj(h, w_in, conv_w, l, aw, cwid, seq)
        attn = _attention(qk, v, tables, lam_vecs, subln_g[l].reshape(1, HEAD_DV),
                          batch, seq, n_heads, t)
        xf, hm = _out_proj(attn, conv, w_out, l, xf, mlp_norm_g[l])
        act = _up_proj(hm, w_up, l)
        xf, h = _down_proj(act, w_down_bf, xf, attn_norm_g[l + 1] if l + 1 < depth else None)
    return xf.reshape(batch, seq, d)
```

```python
import functools
import math

import numpy as np
import jax
import jax.numpy as jnp
from jax import lax
from jax.experimental import pallas as pl
from jax.experimental.pallas import tpu as pltpu

HEAD_DV = 128
HEAD_DK = HEAD_DV // 2
CONV_K = 3
NUM_BUCKETS = 32
MAX_DISTANCE = 128
MAX_EXACT = NUM_BUCKETS // 2
EPS = 1e-6
NEG = -1e30
LOG2E = math.log2(math.e)

LANES = 128
SUBLANES = 8
VMEM_LIMIT_BYTES = 60 * 1024 * 1024

F32 = jnp.float32
BF16 = jnp.bfloat16


def _tile(dim, pref):
    if dim <= pref:
        return dim
    t = (pref // LANES) * LANES
    while dim % t:
        t -= LANES
    return t


def _params(n_axes):
    return pltpu.CompilerParams(
        dimension_semantics=("arbitrary",) * n_axes,
        vmem_limit_bytes=VMEM_LIMIT_BYTES,
    )


def _rmsnorm_kernel(x_ref, g_ref, o_ref):
    x = x_ref[...]
    ms = jnp.mean(x * x, axis=-1, keepdims=True)
    o_ref[...] = (x * lax.rsqrt(ms + EPS) * g_ref[...]).astype(o_ref.dtype)


def _rmsnorm(x, g):
    m, d = x.shape
    tm = _tile(m, 512)
    return pl.pallas_call(
        _rmsnorm_kernel,
        grid=(m // tm,),
        in_specs=[pl.BlockSpec((tm, d), lambda i: (i, 0)),
                  pl.BlockSpec((1, d), lambda i: (0, 0))],
        out_specs=pl.BlockSpec((tm, d), lambda i: (i, 0)),
        out_shape=jax.ShapeDtypeStruct((m, d), BF16),
        compiler_params=_params(1),
        name="rmsnorm",
    )(x, g.reshape(1, d))


def _cast_weight(w_ref, wbf_ref):
    k = w_ref.shape[0]
    ck = _tile(k, 256)
    for r in range(0, k, ck):
        wbf_ref[r:r + ck, :] = w_ref[r:r + ck, :].astype(BF16)


def _segment_ones(width):
    seg = np.arange(width) // HEAD_DK
    return jnp.asarray(seg[:, None] == seg[None, :], dtype=BF16)


def _qk_kernel(h_ref, w_ref, g_ref, e_ref, wd_ref, o_ref, wdo_ref, wbf_ref):
    @pl.when(pl.program_id(1) == 0)
    def _():
        _cast_weight(w_ref, wbf_ref)

    wdo_ref[...] = wd_ref[...].astype(BF16)
    acc = jnp.dot(h_ref[...], wbf_ref[...], preferred_element_type=F32)
    g = g_ref[...]
    gw = e_ref.shape[0]
    for c in range(0, acc.shape[1], gw):
        a = acc[:, c:c + gw]
        ssq = jnp.dot((a * a).astype(BF16), e_ref[...], preferred_element_type=F32)
        inv = lax.rsqrt(ssq * (1.0 / HEAD_DK) + EPS)
        for cc in range(0, gw, LANES):
            o_ref[:, c + cc:c + cc + LANES] = (
                a[:, cc:cc + LANES] * inv[:, cc:cc + LANES] * g).astype(o_ref.dtype)


def _qk_proj(h, w_in, layer, gains, aw, w_down):
    m, d = h.shape
    bm = _tile(m, 1024)
    nm = m // bm
    gw = _tile(aw, 2 * LANES)
    f = w_down.shape[1]
    slab = f // (2 * nm)
    assert f % (2 * nm) == 0 and slab % (2 * SUBLANES) == 0
    return pl.pallas_call(
        _qk_kernel,
        grid=(2, nm),
        in_specs=[pl.BlockSpec((bm, d), lambda j, i: (i, 0)),
                  pl.BlockSpec((None, d, aw), lambda j, i: (layer, 0, j)),
                  pl.BlockSpec((None, 1, LANES), lambda j, i: (j, 0, 0)),
                  pl.BlockSpec((gw, gw), lambda j, i: (0, 0)),
                  pl.BlockSpec((None, slab, d), lambda j, i: (layer, j * nm + i, 0))],
        out_specs=[pl.BlockSpec((bm, aw), lambda j, i: (i, j)),
                   pl.BlockSpec((slab, d), lambda j, i: (j * nm + i, 0))],
        out_shape=[jax.ShapeDtypeStruct((m, 2 * aw), BF16), jax.ShapeDtypeStruct((f, d), BF16)],
        scratch_shapes=[pltpu.VMEM((d, aw), BF16)],
        compiler_params=_params(2),
        name="qk_proj",
    )(h, w_in, gains, _segment_ones(gw), w_down)


def _conv_kernel(h_ref, wb_ref, wc_ref, wi_ref, wv_ref, cw_ref, o_ref, v_ref, wbf_ref, ubuf_ref,
                 *, tiles_per_seq):
    i = pl.program_id(1)
    bm = h_ref.shape[0]

    @pl.when(i == 0)
    def _():
        _cast_weight(wb_ref, wbf_ref.at[0])
        _cast_weight(wc_ref, wbf_ref.at[1])
        _cast_weight(wi_ref, wbf_ref.at[2])
        _cast_weight(wv_ref, wbf_ref.at[3])

    @pl.when(i % tiles_per_seq == 0)
    def _():
        ubuf_ref[0:SUBLANES, :] = jnp.zeros((SUBLANES, ubuf_ref.shape[1]), F32)

    h = h_ref[...]
    gate_c = jnp.dot(h, wbf_ref[1], preferred_element_type=F32)
    conv_in = jnp.dot(h, wbf_ref[2], preferred_element_type=F32)
    ubuf_ref[SUBLANES:SUBLANES + bm, :] = gate_c * conv_in
    gate_b = jnp.dot(h, wbf_ref[0], preferred_element_type=F32)
    cw = cw_ref[...]
    conv = (cw[0:1] * ubuf_ref[SUBLANES - 2:SUBLANES - 2 + bm, :]
            + cw[1:2] * ubuf_ref[SUBLANES - 1:SUBLANES - 1 + bm, :]
            + cw[2:3] * ubuf_ref[SUBLANES:SUBLANES + bm, :])
    o_ref[...] = (gate_b * conv).astype(o_ref.dtype)
    ubuf_ref[0:SUBLANES, :] = ubuf_ref[bm:bm + SUBLANES, :]
    v_ref[...] = jnp.dot(h, wbf_ref[3], preferred_element_type=F32).astype(v_ref.dtype)


def _conv_v_proj(h, w_in, conv_w, layer, aw, cwid, seq):
    m, d = h.shape
    assert aw == cwid
    bm = _tile(seq, 1024)
    bn = _tile(cwid, 512)
    off = 3 * aw // bn
    nb = cwid // bn
    return pl.pallas_call(
        functools.partial(_conv_kernel, tiles_per_seq=seq // bm),
        grid=(nb, m // bm),
        in_specs=[pl.BlockSpec((bm, d), lambda j, i: (i, 0)),
                  pl.BlockSpec((None, d, bn), lambda j, i: (layer, 0, off + j)),
                  pl.BlockSpec((None, d, bn), lambda j, i: (layer, 0, off + nb + j)),
                  pl.BlockSpec((None, d, bn), lambda j, i: (layer, 0, off + 2 * nb + j)),
                  pl.BlockSpec((None, d, bn), lambda j, i: (layer, 0, off - nb + j)),
                  pl.BlockSpec((None, CONV_K, bn), lambda j, i: (layer, 0, j))],
        out_specs=[pl.BlockSpec((bm, bn), lambda j, i: (i, j)),
                   pl.BlockSpec((bm, bn), lambda j, i: (i, j))],
        out_shape=[jax.ShapeDtypeStruct((m, cwid), BF16), jax.ShapeDtypeStruct((m, aw), BF16)],
        scratch_shapes=[pltpu.VMEM((4, d, bn), BF16),
                        pltpu.VMEM((bm + SUBLANES, bn), F32)],
        compiler_params=_params(2),
        name="conv_v_proj",
    )(h, w_in, w_in, w_in, w_in, conv_w)


def _bucket(n):
    nf = np.maximum(n, MAX_EXACT).astype(np.float64)
    val = np.log(nf / MAX_EXACT) / math.log(MAX_DISTANCE / MAX_EXACT) * (NUM_BUCKETS - MAX_EXACT)
    large = np.minimum(MAX_EXACT + val.astype(np.int64), NUM_BUCKETS - 1)
    return np.where(n < MAX_EXACT, n, large)


def _bucket_tables(t):
    assert _bucket(np.array([t + 1]))[0] == NUM_BUCKETS - 1
    r = np.arange(t)[:, None]
    c = np.arange(t)[None, :]
    d0 = r - c
    diag = np.where(d0 >= 0, _bucket(np.maximum(d0, 0)), -1)
    prev = _bucket(t + r - c)
    return np.stack([diag, prev]).astype(np.int32)


def _bias_table_kernel(rb_ref, bk_ref, o_ref):
    h = pl.program_id(0)
    for mode in range(2):
        bk = bk_ref[mode]
        for mp in range(2):
            col = 2 * h + mp
            far = rb_ref[NUM_BUCKETS - 1, col]
            acc = jnp.where(bk < 0, NEG, 0.0).astype(F32)
            for b in range(NUM_BUCKETS - 1):
                acc = jnp.where(bk == b, (rb_ref[b, col] - far) * LOG2E, acc)
            o_ref[0, mode, mp] = acc


def _bias_tables(rel_bias, n_heads, t):
    buckets = jnp.asarray(_bucket_tables(t))
    return pl.pallas_call(
        _bias_table_kernel,
        grid=(n_heads,),
        in_specs=[pl.BlockSpec(memory_space=pltpu.SMEM),
                  pl.BlockSpec((2, t, t), lambda h: (0, 0, 0))],
        out_specs=pl.BlockSpec((1, 2, 2, t, t), lambda h: (h, 0, 0, 0, 0)),
        out_shape=jax.ShapeDtypeStruct((n_heads, 2, 2, t, t), F32),
        compiler_params=_params(1),
        name="bias_tables",
    )(rel_bias, buckets)


def _fold_lanes(x, op):
    return functools.reduce(op, [x[:, c:c + LANES] for c in range(0, x.shape[1], LANES)])


def _attn_kernel(q_ref, k_ref, v_ref, tb_ref, lam_ref, g_ref, o_ref, v1_ref, *, t):
    seq = q_ref.shape[0]
    heads = q_ref.shape[1] // HEAD_DV
    for hh in range(heads):
        v1_ref[hh, :, 0:HEAD_DV] = v_ref[:, hh * HEAD_DV:(hh + 1) * HEAD_DV]
        v1_ref[hh, :, HEAD_DV:] = jnp.ones((seq, LANES), BF16)
    lo = lax.broadcasted_iota(jnp.int32, (1, LANES), 1) < HEAD_DK
    lv = lam_ref[...]
    lam = (jnp.exp(jnp.sum(lv[0:1] * lv[1:2], axis=-1, keepdims=True))
           - jnp.exp(jnp.sum(lv[2:3] * lv[3:4], axis=-1, keepdims=True)) + lv[4:5, 0:1])
    out_gain = lv[5:6, 0:1]
    g = g_ref[...]
    contract_last = (((1,), (1,)), ((), ()))

    def tile(hh, i):
        cols = slice(hh * HEAD_DV, (hh + 1) * HEAD_DV)
        q = q_ref[i * t:(i + 1) * t, cols]
        zero = jnp.zeros_like(q)
        q2 = jnp.concatenate([jnp.where(lo, q, zero), jnp.where(lo, zero, q)], axis=0)

        pieces = []
        if i >= 2:
            pieces.append((0, (i - 1) * t, None))
        if i >= 1:
            pieces.append(((i - 1) * t, t, 1))
        pieces.append((i * t, t, 0))

        scores = []
        for start, width, table in pieces:
            s = lax.dot_general(q2, k_ref[start:start + width, cols], contract_last,
                                preferred_element_type=F32)
            if table is not None:
                s = s + tb_ref[hh, table].reshape(2 * t, t)
            scores.append(s)
        m = functools.reduce(jnp.maximum, [_fold_lanes(s, jnp.maximum) for s in scores])
        m = jnp.max(m, axis=-1, keepdims=True)
        acc = None
        for s, (start, width, _) in zip(scores, pieces):
            p = jnp.exp2(s - m).astype(BF16)
            pv = jnp.dot(p, v1_ref[hh, start:start + width, :], preferred_element_type=F32)
            acc = pv if acc is None else acc + pv
        out = acc[:, 0:HEAD_DV] / acc[:, HEAD_DV:]
        o = out[0:t] - lam * out[t:2 * t]
        ms = jnp.mean(o * o, axis=-1, keepdims=True)
        o_ref[i * t:(i + 1) * t, cols] = (o * lax.rsqrt(ms + EPS) * g * out_gain).astype(o_ref.dtype)

    nq = seq // t
    for i in [0] + list(range(nq - 1, 0, -1)):
        for hh in range(heads):
            tile(hh, i)


def _attention(qk, v, tables, lam_vecs, subln_g, batch, seq, n_heads, t):
    hps = 2 if n_heads % 2 == 0 else 1
    w = hps * HEAD_DV
    ng = n_heads // hps
    return pl.pallas_call(
        functools.partial(_attn_kernel, t=t),
        grid=(batch, ng),
        in_specs=[pl.BlockSpec((seq, w), lambda b, h: (b, h)),
                  pl.BlockSpec((seq, w), lambda b, h: (b, ng + h)),
                  pl.BlockSpec((seq, w), lambda b, h: (b, h)),
                  pl.BlockSpec((hps, 2, 2, t, t), lambda b, h: (h, 0, 0, 0, 0)),
                  pl.BlockSpec((6, HEAD_DK), lambda b, h: (0, 0)),
                  pl.BlockSpec((1, HEAD_DV), lambda b, h: (0, 0))],
        out_specs=pl.BlockSpec((seq, w), lambda b, h: (b, h)),
        out_shape=jax.ShapeDtypeStruct((batch * seq, n_heads * HEAD_DV), BF16),
        scratch_shapes=[pltpu.VMEM((hps, seq, HEAD_DV + LANES), BF16)],
        compiler_params=_params(2),
        name="diff_attention",
    )(qk, qk, v, tables, lam_vecs, subln_g)


def _out_kernel(a_ref, c_ref, w_ref, x_ref, g_ref, xo_ref, h_ref, wbf_ref, *, n_pro):
    s = pl.program_id(0)
    ck = w_ref.shape[0]

    @pl.when(s < n_pro)
    def _():
        wbf_ref[pl.ds(pl.multiple_of(s * ck, ck), ck), :] = w_ref[...].astype(BF16)

    @pl.when(s >= n_pro)
    def _():
        aw = a_ref.shape[1]
        acc = jnp.dot(a_ref[...], wbf_ref[0:aw, :], preferred_element_type=F32)
        acc = acc + jnp.dot(c_ref[...], wbf_ref[aw:, :], preferred_element_type=F32)
        xn = x_ref[...] + acc
        xo_ref[...] = xn
        ms = jnp.mean(xn * xn, axis=-1, keepdims=True)
        h_ref[...] = (xn * lax.rsqrt(ms + EPS) * g_ref[...]).astype(h_ref.dtype)


def _out_proj(attn, conv, w_out, layer, x, g):
    m, d = x.shape
    aw, cwid = attn.shape[1], conv.shape[1]
    bm = _tile(m, 512)
    ck = _tile(aw + cwid, 512)
    n_pro = (aw + cwid) // ck

    def row(s):
        return jnp.maximum(s - n_pro, 0)

    return pl.pallas_call(
        functools.partial(_out_kernel, n_pro=n_pro),
        grid=(n_pro + m // bm,),
        in_specs=[pl.BlockSpec((bm, aw), lambda s: (row(s), 0)),
                  pl.BlockSpec((bm, cwid), lambda s: (row(s), 0)),
                  pl.BlockSpec((None, ck, d), lambda s: (layer, jnp.minimum(s, n_pro - 1), 0)),
                  pl.BlockSpec((bm, d), lambda s: (row(s), 0)),
                  pl.BlockSpec((1, d), lambda s: (0, 0))],
        out_specs=[pl.BlockSpec((bm, d), lambda s: (row(s), 0)),
                   pl.BlockSpec((bm, d), lambda s: (row(s), 0))],
        out_shape=[jax.ShapeDtypeStruct((m, d), F32), jax.ShapeDtypeStruct((m, d), BF16)],
        scratch_shapes=[pltpu.VMEM((aw + cwid, d), BF16)],
        compiler_params=_params(1),
        name="out_proj",
    )(attn, conv, w_out, x, g.reshape(1, d))


def _up_kernel(h_ref, w_ref, o_ref, wbf_ref):
    @pl.when(pl.program_id(1) == 0)
    def _():
        _cast_weight(w_ref, wbf_ref)

    acc = jnp.dot(h_ref[...], wbf_ref[...], preferred_element_type=F32)
    o_ref[...] = jnp.square(jnp.maximum(acc, 0.0)).astype(o_ref.dtype)


def _up_proj(h, w_up, layer):
    m, d = h.shape
    f = w_up.shape[2]
    bm = _tile(m, 2048)
    bn = _tile(f, 1024)
    return pl.pallas_call(
        _up_kernel,
        grid=(f // bn, m // bm),
        in_specs=[pl.BlockSpec((bm, d), lambda j, i: (i, 0)),
                  pl.BlockSpec((None, d, bn), lambda j, i: (layer, 0, j))],
        out_specs=pl.BlockSpec((bm, bn), lambda j, i: (i, j)),
        out_shape=jax.ShapeDtypeStruct((m, f), BF16),
        scratch_shapes=[pltpu.VMEM((d, bn), BF16)],
        compiler_params=_params(2),
        name="up_proj",
    )(h, w_up)


def _down_kernel(a_ref, w_ref, x_ref, *rest, with_norm):
    if with_norm:
        g_ref, xo_ref, h_ref = rest
    else:
        (xo_ref,) = rest
    k = pl.program_id(1)
    last = pl.num_programs(1) - 1

    @pl.when(k == 0)
    def _():
        xo_ref[...] = x_ref[...]

    if not with_norm:
        xo_ref[...] += jnp.dot(a_ref[...], w_ref[...], preferred_element_type=F32)
        return

    @pl.when(k < last)
    def _():
        xo_ref[...] += jnp.dot(a_ref[...], w_ref[...], preferred_element_type=F32)

    @pl.when(k == last)
    def _():
        xn = xo_ref[...] + jnp.dot(a_ref[...], w_ref[...], preferred_element_type=F32)
        xo_ref[...] = xn
        ms = jnp.mean(xn * xn, axis=-1, keepdims=True)
        h_ref[...] = (xn * lax.rsqrt(ms + EPS) * g_ref[...]).astype(h_ref.dtype)


def _down_proj(a, w_down_bf, x, g_next):
    m, d = x.shape
    f = a.shape[1]
    bm = _tile(m, 1024)
    bk = _tile(f, 1024)
    with_norm = g_next is not None
    row = pl.BlockSpec((bm, d), lambda i, k: (i, 0))
    in_specs = [pl.BlockSpec((bm, bk), lambda i, k: (i, k)),
                pl.BlockSpec((bk, d), lambda i, k: (k, 0)),
                row]
    args = [a, w_down_bf, x]
    out_specs = [row]
    out_shape = [jax.ShapeDtypeStruct((m, d), F32)]
    if with_norm:
        in_specs.append(pl.BlockSpec((1, d), lambda i, k: (0, 0)))
        args.append(g_next.reshape(1, d))
        out_specs.append(row)
        out_shape.append(jax.ShapeDtypeStruct((m, d), BF16))
    outs = pl.pallas_call(
        functools.partial(_down_kernel, with_norm=with_norm),
        grid=(m // bm, f // bk),
        in_specs=in_specs,
        out_specs=out_specs,
        out_shape=out_shape,
        compiler_params=_params(2),
        name="down_proj",
    )(*args)
    return (outs[0], outs[1]) if with_norm else (outs[0], None)


def kernel(x, w_in, w_out, conv_w, q_norm_g, k_norm_g, lambda_q1, lambda_k1, lambda_q2, lambda_k2,
           subln_g, attn_norm_g, mlp_norm_g, w_up, w_down, rel_bias):
    batch, seq, d = x.shape
    depth = w_in.shape[0]
    aw = d // 2
    cwid = d - aw
    n_heads = aw // HEAD_DV
    assert w_in.shape[2] == 3 * aw + 3 * cwid and rel_bias.shape == (NUM_BUCKETS, 2 * n_heads)
    t = _tile(seq, 256)

    tables = _bias_tables(rel_bias, n_heads, t)
    scale = HEAD_DK ** -0.5 * LOG2E
    xf = x.reshape(batch * seq, d)
    h = _rmsnorm(xf, attn_norm_g[0])
    for l in range(depth):
        lam_init = 0.8 - 0.6 * math.exp(-0.3 * l)
        qk_gains = jnp.stack([jnp.tile(q_norm_g[l] * scale, 2), jnp.tile(k_norm_g[l], 2)])
        lam_vecs = jnp.stack([lambda_q1[l], lambda_k1[l], lambda_q2[l], lambda_k2[l],
                              jnp.full((HEAD_DK,), lam_init, F32),
                              jnp.full((HEAD_DK,), 1.0 - lam_init, F32)])

        qk, w_down_bf = _qk_proj(h, w_in, l, qk_gains.reshape(2, 1, LANES), aw, w_down)
        conv, v = _conv_v_proj(h, w_in, conv_w, l, aw, cwid, seq)
        attn = _attention(qk, v, tables, lam_vecs, subln_g[l].reshape(1, HEAD_DV),
                          batch, seq, n_heads, t)
        xf, hm = _out_proj(attn, conv, w_out, l, xf, mlp_norm_g[l])
        act = _up_proj(hm, w_up, l)
        xf, h = _down_proj(act, w_down_bf, xf, attn_norm_g[l + 1] if l + 1 < depth else None)
    return xf.reshape(batch, seq, d)
```

```python
import functools
import math

import numpy as np
import jax
import jax.numpy as jnp
from jax import lax
from jax.experimental import pallas as pl
from jax.experimental.pallas import tpu as pltpu

HEAD_DV = 128
HEAD_DK = HEAD_DV // 2
CONV_K = 3
NUM_BUCKETS = 32
MAX_DISTANCE = 128
MAX_EXACT = NUM_BUCKETS // 2
EPS = 1e-6
NEG = -1e30
LOG2E = math.log2(math.e)

LANES = 128
SUBLANES = 8
VMEM_LIMIT_BYTES = 60 * 1024 * 1024

F32 = jnp.float32
BF16 = jnp.bfloat16


def _tile(dim, pref):
    if dim <= pref:
        return dim
    t = (pref // LANES) * LANES
    while dim % t:
        t -= LANES
    return t


def _params(n_axes):
    return pltpu.CompilerParams(
        dimension_semantics=("arbitrary",) * n_axes,
        vmem_limit_bytes=VMEM_LIMIT_BYTES,
    )


def _rmsnorm_kernel(x_ref, g_ref, o_ref):
    x = x_ref[...]
    ms = jnp.mean(x * x, axis=-1, keepdims=True)
    o_ref[...] = (x * lax.rsqrt(ms + EPS) * g_ref[...]).astype(o_ref.dtype)


def _rmsnorm(x, g):
    m, d = x.shape
    tm = _tile(m, 512)
    return pl.pallas_call(
        _rmsnorm_kernel,
        grid=(m // tm,),
        in_specs=[pl.BlockSpec((tm, d), lambda i: (i, 0)),
                  pl.BlockSpec((1, d), lambda i: (0, 0))],
        out_specs=pl.BlockSpec((tm, d), lambda i: (i, 0)),
        out_shape=jax.ShapeDtypeStruct((m, d), BF16),
        compiler_params=_params(1),
        name="rmsnorm",
    )(x, g.reshape(1, d))


def _cast_weight(w_ref, wbf_ref):
    k = w_ref.shape[0]
    ck = _tile(k, 256)
    for r in range(0, k, ck):
        wbf_ref[r:r + ck, :] = w_ref[r:r + ck, :].astype(BF16)


def _segment_ones(width):
    seg = np.arange(width) // HEAD_DK
    return jnp.asarray(seg[:, None] == seg[None, :], dtype=BF16)


def _qk_kernel(h_ref, w_ref, g_ref, e_ref, wd_ref, o_ref, wdo_ref, wbf_ref):
    @pl.when(pl.program_id(1) == 0)
    def _():
        _cast_weight(w_ref, wbf_ref)

    wdo_ref[...] = wd_ref[...].astype(BF16)
    acc = jnp.dot(h_ref[...], wbf_ref[...], preferred_element_type=F32)
    g = g_ref[...]
    gw = e_ref.shape[0]
    for c in range(0, acc.shape[1], gw):
        a = acc[:, c:c + gw]
        ssq = jnp.dot((a * a).astype(BF16), e_ref[...], preferred_element_type=F32)
        inv = lax.rsqrt(ssq * (1.0 / HEAD_DK) + EPS)
        for cc in range(0, gw, LANES):
            o_ref[:, c + cc:c + cc + LANES] = (
                a[:, cc:cc + LANES] * inv[:, cc:cc + LANES] * g).astype(o_ref.dtype)


def _qk_proj(h, w_in, layer, gains, aw, w_down):
    m, d = h.shape
    bm = _tile(m, 1024)
    nm = m // bm
    gw = _tile(aw, 2 * LANES)
    f = w_down.shape[1]
    slab = f // (2 * nm)
    assert f % (2 * nm) == 0 and slab % (2 * SUBLANES) == 0
    return pl.pallas_call(
        _qk_kernel,
        grid=(2, nm),
        in_specs=[pl.BlockSpec((bm, d), lambda j, i: (i, 0)),
                  pl.BlockSpec((None, d, aw), lambda j, i: (layer, 0, j)),
                  pl.BlockSpec((None, 1, LANES), lambda j, i: (j, 0, 0)),
                  pl.BlockSpec((gw, gw), lambda j, i: (0, 0)),
                  pl.BlockSpec((None, slab, d), lambda j, i: (layer, j * nm + i, 0))],
        out_specs=[pl.BlockSpec((bm, aw), lambda j, i: (i, j)),
                   pl.BlockSpec((slab, d), lambda j, i: (j * nm + i, 0))],
        out_shape=[jax.ShapeDtypeStruct((m, 2 * aw), BF16), jax.ShapeDtypeStruct((f, d), BF16)],
        scratch_shapes=[pltpu.VMEM((d, aw), BF16)],
        compiler_params=_params(2),
        name="qk_proj",
    )(h, w_in, gains, _segment_ones(gw), w_down)


def _conv_kernel(h_ref, wb_ref, wc_ref, wi_ref, wv_ref, cw_ref, o_ref, v_ref, wbf_ref, ubuf_ref,
                 *, tiles_per_seq):
    i = pl.program_id(1)
    bm = h_ref.shape[0]

    @pl.when(i == 0)
    def _():
        _cast_weight(wb_ref, wbf_ref.at[0])
        _cast_weight(wc_ref, wbf_ref.at[1])
        _cast_weight(wi_ref, wbf_ref.at[2])
        _cast_weight(wv_ref, wbf_ref.at[3])

    @pl.when(i % tiles_per_seq == 0)
    def _():
        ubuf_ref[0:SUBLANES, :] = jnp.zeros((SUBLANES, ubuf_ref.shape[1]), F32)

    h = h_ref[...]
    gate_c = jnp.dot(h, wbf_ref[1], preferred_element_type=F32)
    conv_in = jnp.dot(h, wbf_ref[2], preferred_element_type=F32)
    ubuf_ref[SUBLANES:SUBLANES + bm, :] = gate_c * conv_in
    gate_b = jnp.dot(h, wbf_ref[0], preferred_element_type=F32)
    cw = cw_ref[...]
    conv = (cw[0:1] * ubuf_ref[SUBLANES - 2:SUBLANES - 2 + bm, :]
            + cw[1:2] * ubuf_ref[SUBLANES - 1:SUBLANES - 1 + bm, :]
            + cw[2:3] * ubuf_ref[SUBLANES:SUBLANES + bm, :])
    o_ref[...] = (gate_b * conv).astype(o_ref.dtype)
    ubuf_ref[0:SUBLANES, :] = ubuf_ref[bm:bm + SUBLANES, :]
    v_ref[...] = jnp.dot(h, wbf_ref[3], preferred_element_type=F32).astype(v_ref.dtype)


def _conv_v_proj(h, w_in, conv_w, layer, aw, cwid, seq):
    m, d = h.shape
    assert aw == cwid
    bm = _tile(seq, 1024)
    bn = _tile(cwid, 512)
    off = 3 * aw // bn
    nb = cwid // bn
    return pl.pallas_call(
        functools.partial(_conv_kernel, tiles_per_seq=seq // bm),
        grid=(nb, m // bm),
        in_specs=[pl.BlockSpec((bm, d), lambda j, i: (i, 0)),
                  pl.BlockSpec((None, d, bn), lambda j, i: (layer, 0, off + j)),
                  pl.BlockSpec((None, d, bn), lambda j, i: (layer, 0, off + nb + j)),
                  pl.BlockSpec((None, d, bn), lambda j, i: (layer, 0, off + 2 * nb + j)),
                  pl.BlockSpec((None, d, bn), lambda j, i: (layer, 0, off - nb + j)),
                  pl.BlockSpec((None, CONV_K, bn), lambda j, i: (layer, 0, j))],
        out_specs=[pl.BlockSpec((bm, bn), lambda j, i: (i, j)),
                   pl.BlockSpec((bm, bn), lambda j, i: (i, j))],
        out_shape=[jax.ShapeDtypeStruct((m, cwid), BF16), jax.ShapeDtypeStruct((m, aw), BF16)],
        scratch_shapes=[pltpu.VMEM((4, d, bn), BF16),
                        pltpu.VMEM((bm + SUBLANES, bn), F32)],
        compiler_params=_params(2),
        name="conv_v_proj",
    )(h, w_in, w_in, w_in, w_in, conv_w)


def _bucket(n):
    nf = np.maximum(n, MAX_EXACT).astype(np.float64)
    val = np.log(nf / MAX_EXACT) / math.log(MAX_DISTANCE / MAX_EXACT) * (NUM_BUCKETS - MAX_EXACT)
    large = np.minimum(MAX_EXACT + val.astype(np.int64), NUM_BUCKETS - 1)
    return np.where(n < MAX_EXACT, n, large)


def _bucket_tables(t):
    assert _bucket(np.array([t + 1]))[0] == NUM_BUCKETS - 1
    r = np.arange(t)[:, None]
    c = np.arange(t)[None, :]
    d0 = r - c
    diag = np.where(d0 >= 0, _bucket(np.maximum(d0, 0)), -1)
    prev = _bucket(t + r - c)
    return np.stack([diag, prev]).astype(np.int32)


def _bias_table_kernel(rb_ref, bk_ref, o_ref):
    h = pl.program_id(0)
    t = bk_ref.shape[1]
    for mode in range(2):
        bk = bk_ref[mode]
        for mp in range(2):
            col = 2 * h + mp
            far = rb_ref[NUM_BUCKETS - 1, col]
            acc = jnp.where(bk < 0, NEG, 0.0).astype(F32)
            for b in range(NUM_BUCKETS - 1):
                acc = jnp.where(bk == b, (rb_ref[b, col] - far) * LOG2E, acc)
            o_ref[0, mp, :, (1 - mode) * t:(2 - mode) * t] = acc


def _bias_tables(rel_bias, n_heads, t):
    buckets = jnp.asarray(_bucket_tables(t))
    return pl.pallas_call(
        _bias_table_kernel,
        grid=(n_heads,),
        in_specs=[pl.BlockSpec(memory_space=pltpu.SMEM),
                  pl.BlockSpec((2, t, t), lambda h: (0, 0, 0))],
        out_specs=pl.BlockSpec((1, 2, t, 2 * t), lambda h: (h, 0, 0, 0)),
        out_shape=jax.ShapeDtypeStruct((n_heads, 2, t, 2 * t), F32),
        compiler_params=_params(1),
        name="bias_tables",
    )(rel_bias, buckets)


def _fold_lanes(x, op):
    return functools.reduce(op, [x[:, c:c + LANES] for c in range(0, x.shape[1], LANES)])


def _attn_kernel(q_ref, k_ref, v_ref, tb_ref, lam_ref, g_ref, o_ref, v1_ref, *, t):
    seq = q_ref.shape[0]
    heads = q_ref.shape[1] // HEAD_DV
    for hh in range(heads):
        v1_ref[hh, :, 0:HEAD_DV] = v_ref[:, hh * HEAD_DV:(hh + 1) * HEAD_DV]
        v1_ref[hh, :, HEAD_DV:] = jnp.ones((seq, LANES), BF16)
    lo = lax.broadcasted_iota(jnp.int32, (1, LANES), 1) < HEAD_DK
    lv = lam_ref[...]
    lam = (jnp.exp(jnp.sum(lv[0:1] * lv[1:2], axis=-1, keepdims=True))
           - jnp.exp(jnp.sum(lv[2:3] * lv[3:4], axis=-1, keepdims=True)) + lv[4:5, 0:1])
    out_gain = lv[5:6, 0:1]
    g = g_ref[...]
    contract_last = (((1,), (1,)), ((), ()))

    def tile(hh, i):
        cols = slice(hh * HEAD_DV, (hh + 1) * HEAD_DV)
        q = q_ref[i * t:(i + 1) * t, cols]
        zero = jnp.zeros_like(q)
        q2 = jnp.concatenate([jnp.where(lo, q, zero), jnp.where(lo, zero, q)], axis=0)

        pieces = []
        if i >= 2:
            pieces.append((0, (i - 1) * t, False))
        near = min(i, 1) * t
        pieces.append((i * t - near, near + t, True))

        scores = []
        for start, width, biased in pieces:
            s = lax.dot_general(q2, k_ref[start:start + width, cols], contract_last,
                                preferred_element_type=F32)
            if biased:
                s = s + tb_ref[hh, :, :, 2 * t - width:].reshape(2 * t, width)
            scores.append(s)
        m = functools.reduce(jnp.maximum, [_fold_lanes(s, jnp.maximum) for s in scores])
        m = jnp.max(m, axis=-1, keepdims=True)
        acc = None
        for s, (start, width, _) in zip(scores, pieces):
            p = jnp.exp2(s - m).astype(BF16)
            pv = jnp.dot(p, v1_ref[hh, start:start + width, :], preferred_element_type=F32)
            acc = pv if acc is None else acc + pv
        out = acc[:, 0:HEAD_DV] / acc[:, HEAD_DV:]
        o = out[0:t] - lam * out[t:2 * t]
        ms = jnp.mean(o * o, axis=-1, keepdims=True)
        o_ref[i * t:(i + 1) * t, cols] = (o * lax.rsqrt(ms + EPS) * g * out_gain).astype(o_ref.dtype)

    nq = seq // t
    for i in [0] + list(range(nq - 1, 0, -1)):
        for hh in range(heads):
            tile(hh, i)


def _attention(qk, v, tables, lam_vecs, subln_g, batch, seq, n_heads, t):
    hps = 2 if n_heads % 2 == 0 else 1
    w = hps * HEAD_DV
    ng = n_heads // hps
    return pl.pallas_call(
        functools.partial(_attn_kernel, t=t),
        grid=(batch, ng),
        in_specs=[pl.BlockSpec((seq, w), lambda b, h: (b, h)),
                  pl.BlockSpec((seq, w), lambda b, h: (b, ng + h)),
                  pl.BlockSpec((seq, w), lambda b, h: (b, h)),
                  pl.BlockSpec((hps, 2, t, 2 * t), lambda b, h: (h, 0, 0, 0)),
                  pl.BlockSpec((6, HEAD_DK), lambda b, h: (0, 0)),
                  pl.BlockSpec((1, HEAD_DV), lambda b, h: (0, 0))],
        out_specs=pl.BlockSpec((seq, w), lambda b, h: (b, h)),
        out_shape=jax.ShapeDtypeStruct((batch * seq, n_heads * HEAD_DV), BF16),
        scratch_shapes=[pltpu.VMEM((hps, seq, HEAD_DV + LANES), BF16)],
        compiler_params=_params(2),
        name="diff_attention",
    )(qk, qk, v, tables, lam_vecs, subln_g)


def _out_kernel(a_ref, c_ref, w_ref, x_ref, g_ref, xo_ref, h_ref, wbf_ref, *, n_pro):
    s = pl.program_id(0)
    ck = w_ref.shape[0]

    @pl.when(s < n_pro)
    def _():
        wbf_ref[pl.ds(pl.multiple_of(s * ck, ck), ck), :] = w_ref[...].astype(BF16)

    @pl.when(s >= n_pro)
    def _():
        aw = a_ref.shape[1]
        acc = jnp.dot(a_ref[...], wbf_ref[0:aw, :], preferred_element_type=F32)
        acc = acc + jnp.dot(c_ref[...], wbf_ref[aw:, :], preferred_element_type=F32)
        xn = x_ref[...] + acc
        xo_ref[...] = xn
        ms = jnp.mean(xn * xn, axis=-1, keepdims=True)
        h_ref[...] = (xn * lax.rsqrt(ms + EPS) * g_ref[...]).astype(h_ref.dtype)


def _out_proj(attn, conv, w_out, layer, x, g):
    m, d = x.shape
    aw, cwid = attn.shape[1], conv.shape[1]
    bm = _tile(m, 512)
    ck = _tile(aw + cwid, 512)
    n_pro = (aw + cwid) // ck

    def row(s):
        return jnp.maximum(s - n_pro, 0)

    return pl.pallas_call(
        functools.partial(_out_kernel, n_pro=n_pro),
        grid=(n_pro + m // bm,),
        in_specs=[pl.BlockSpec((bm, aw), lambda s: (row(s), 0)),
                  pl.BlockSpec((bm, cwid), lambda s: (row(s), 0)),
                  pl.BlockSpec((None, ck, d), lambda s: (layer, jnp.minimum(s, n_pro - 1), 0)),
                  pl.BlockSpec((bm, d), lambda s: (row(s), 0)),
                  pl.BlockSpec((1, d), lambda s: (0, 0))],
        out_specs=[pl.BlockSpec((bm, d), lambda s: (row(s), 0)),
                   pl.BlockSpec((bm, d), lambda s: (row(s), 0))],
        out_shape=[jax.ShapeDtypeStruct((m, d), F32), jax.ShapeDtypeStruct((m, d), BF16)],
        scratch_shapes=[pltpu.VMEM((aw + cwid, d), BF16)],
        compiler_params=_params(1),
        name="out_proj",
    )(attn, conv, w_out, x, g.reshape(1, d))


def _up_kernel(h_ref, w_ref, o_ref, wbf_ref):
    @pl.when(pl.program_id(1) == 0)
    def _():
        _cast_weight(w_ref, wbf_ref)

    acc = jnp.dot(h_ref[...], wbf_ref[...], preferred_element_type=F32)
    o_ref[...] = jnp.square(jnp.maximum(acc, 0.0)).astype(o_ref.dtype)


def _up_proj(h, w_up, layer):
    m, d = h.shape
    f = w_up.shape[2]
    bm = _tile(m, 2048)
    bn = _tile(f, 1024)
    return pl.pallas_call(
        _up_kernel,
        grid=(f // bn, m // bm),
        in_specs=[pl.BlockSpec((bm, d), lambda j, i: (i, 0)),
                  pl.BlockSpec((None, d, bn), lambda j, i: (layer, 0, j))],
        out_specs=pl.BlockSpec((bm, bn), lambda j, i: (i, j)),
        out_shape=jax.ShapeDtypeStruct((m, f), BF16),
        scratch_shapes=[pltpu.VMEM((d, bn), BF16)],
        compiler_params=_params(2),
        name="up_proj",
    )(h, w_up)


def _down_kernel(a_ref, w_ref, x_ref, *rest, with_norm):
    if with_norm:
        g_ref, xo_ref, h_ref = rest
    else:
        (xo_ref,) = rest
    k = pl.program_id(1)
    last = pl.num_programs(1) - 1

    @pl.when(k == 0)
    def _():
        xo_ref[...] = x_ref[...]

    if not with_norm:
        xo_ref[...] += jnp.dot(a_ref[...], w_ref[...], preferred_element_type=F32)
        return

    @pl.when(k < last)
    def _():
        xo_ref[...] += jnp.dot(a_ref[...], w_ref[...], preferred_element_type=F32)

    @pl.when(k == last)
    def _():
        xn = xo_ref[...] + jnp.dot(a_ref[...], w_ref[...], preferred_element_type=F32)
        xo_ref[...] = xn
        ms = jnp.mean(xn * xn, axis=-1, keepdims=True)
        h_ref[...] = (xn * lax.rsqrt(ms + EPS) * g_ref[...]).astype(h_ref.dtype)


def _down_proj(a, w_down_bf, x, g_next):
    m, d = x.shape
    f = a.shape[1]
    bm = _tile(m, 1024)
    bk = _tile(f, 1024)
    with_norm = g_next is not None
    row = pl.BlockSpec((bm, d), lambda i, k: (i, 0))
    in_specs = [pl.BlockSpec((bm, bk), lambda i, k: (i, k)),
                pl.BlockSpec((bk, d), lambda i, k: (k, 0)),
                row]
    args = [a, w_down_bf, x]
    out_specs = [row]
    out_shape = [jax.ShapeDtypeStruct((m, d), F32)]
    if with_norm:
        in_specs.append(pl.BlockSpec((1, d), lambda i, k: (0, 0)))
        args.append(g_next.reshape(1, d))
        out_specs.append(row)
        out_shape.append(jax.ShapeDtypeStruct((m, d), BF16))
    outs = pl.pallas_call(
        functools.partial(_down_kernel, with_norm=with_norm),
        grid=(m // bm, f // bk),
        in_specs=in_specs,
        out_specs=out_specs,
        out_shape=out_shape,
        compiler_params=_params(2),
        name="down_proj",
    )(*args)
    return (outs[0], outs[1]) if with_norm else (outs[0], None)


def kernel(x, w_in, w_out, conv_w, q_norm_g, k_norm_g, lambda_q1, lambda_k1, lambda_q2, lambda_k2,
           subln_g, attn_norm_g, mlp_norm_g, w_up, w_down, rel_bias):
    batch, seq, d = x.shape
    depth = w_in.shape[0]
    aw = d // 2
    cwid = d - aw
    n_heads = aw // HEAD_DV
    assert w_in.shape[2] == 3 * aw + 3 * cwid and rel_bias.shape == (NUM_BUCKETS, 2 * n_heads)
    t = _tile(seq, 256)

    tables = _bias_tables(rel_bias, n_heads, t)
    scale = HEAD_DK ** -0.5 * LOG2E
    xf = x.reshape(batch * seq, d)
    h = _rmsnorm(xf, attn_norm_g[0])
    for l in range(depth):
        lam_init = 0.8 - 0.6 * math.exp(-0.3 * l)
        qk_gains = jnp.stack([jnp.tile(q_norm_g[l] * scale, 2), jnp.tile(k_norm_g[l], 2)])
        lam_vecs = jnp.stack([lambda_q1[l], lambda_k1[l], lambda_q2[l], lambda_k2[l],
                              jnp.full((HEAD_DK,), lam_init, F32),
                              jnp.full((HEAD_DK,), 1.0 - lam_init, F32)])

        qk, w_down_bf = _qk_proj(h, w_in, l, qk_gains.reshape(2, 1, LANES), aw, w_down)
        conv, v = _conv_v_proj(h, w_in, conv_w, l, aw, cwid, seq)
        attn = _attention(qk, v, tables, lam_vecs, subln_g[l].reshape(1, HEAD_DV),
                          batch, seq, n_heads, t)
        xf, hm = _out_proj(attn, conv, w_out, l, xf, mlp_norm_g[l])
        act = _up_proj(hm, w_up, l)
        xf, h = _down_proj(act, w_down_bf, xf, attn_norm_g[l + 1] if l + 1 < depth else None)
    return xf.reshape(batch, seq, d)
```

```python
import functools
import math

import numpy as np
import jax
import jax.numpy as jnp
from jax import lax
from jax.experimental import pallas as pl
from jax.experimental.pallas import tpu as pltpu

HEAD_DV = 128
HEAD_DK = HEAD_DV // 2
CONV_K = 3
NUM_BUCKETS = 32
MAX_DISTANCE = 128
MAX_EXACT = NUM_BUCKETS // 2
EPS = 1e-6
NEG = -1e30
LOG2E = math.log2(math.e)

LANES = 128
SUBLANES = 8
VMEM_LIMIT_BYTES = 60 * 1024 * 1024

F32 = jnp.float32
BF16 = jnp.bfloat16


def _tile(dim, pref):
    if dim <= pref:
        return dim
    t = (pref // LANES) * LANES
    while dim % t:
        t -= LANES
    return t


def _params(n_axes):
    return pltpu.CompilerParams(
        dimension_semantics=("arbitrary",) * n_axes,
        vmem_limit_bytes=VMEM_LIMIT_BYTES,
    )


def _rmsnorm_kernel(x_ref, g_ref, o_ref):
    x = x_ref[...]
    ms = jnp.mean(x * x, axis=-1, keepdims=True)
    o_ref[...] = (x * lax.rsqrt(ms + EPS) * g_ref[...]).astype(o_ref.dtype)


def _rmsnorm(x, g):
    m, d = x.shape
    tm = _tile(m, 512)
    return pl.pallas_call(
        _rmsnorm_kernel,
        grid=(m // tm,),
        in_specs=[pl.BlockSpec((tm, d), lambda i: (i, 0)),
                  pl.BlockSpec((1, d), lambda i: (0, 0))],
        out_specs=pl.BlockSpec((tm, d), lambda i: (i, 0)),
        out_shape=jax.ShapeDtypeStruct((m, d), BF16),
        compiler_params=_params(1),
        name="rmsnorm",
    )(x, g.reshape(1, d))


def _cast_weight(w_ref, wbf_ref):
    k = w_ref.shape[0]
    ck = _tile(k, 256)
    for r in range(0, k, ck):
        wbf_ref[r:r + ck, :] = w_ref[r:r + ck, :].astype(BF16)


def _segment_ones(width):
    seg = np.arange(width) // HEAD_DK
    return jnp.asarray(seg[:, None] == seg[None, :], dtype=BF16)


def _qk_kernel(h_ref, w_ref, g_ref, e_ref, wd_ref, o_ref, wdo_ref, wbf_ref):
    @pl.when(pl.program_id(1) == 0)
    def _():
        _cast_weight(w_ref, wbf_ref)

    wdo_ref[...] = wd_ref[...].astype(BF16)
    acc = jnp.dot(h_ref[...], wbf_ref[...], preferred_element_type=F32)
    g = g_ref[...]
    gw = e_ref.shape[0]
    for c in range(0, acc.shape[1], gw):
        a = acc[:, c:c + gw]
        ssq = jnp.dot((a * a).astype(BF16), e_ref[...], preferred_element_type=F32)
        inv = lax.rsqrt(ssq * (1.0 / HEAD_DK) + EPS)
        for cc in range(0, gw, LANES):
            o_ref[:, c + cc:c + cc + LANES] = (
                a[:, cc:cc + LANES] * inv[:, cc:cc + LANES] * g).astype(o_ref.dtype)


def _qk_proj(h, w_in, layer, gains, aw, w_down):
    m, d = h.shape
    bm = _tile(m, 1024)
    nm = m // bm
    gw = _tile(aw, 2 * LANES)
    f = w_down.shape[1]
    slab = f // (2 * nm)
    assert f % (2 * nm) == 0 and slab % (2 * SUBLANES) == 0
    return pl.pallas_call(
        _qk_kernel,
        grid=(2, nm),
        in_specs=[pl.BlockSpec((bm, d), lambda j, i: (i, 0)),
                  pl.BlockSpec((None, d, aw), lambda j, i: (layer, 0, j)),
                  pl.BlockSpec((None, 1, LANES), lambda j, i: (j, 0, 0)),
                  pl.BlockSpec((gw, gw), lambda j, i: (0, 0)),
                  pl.BlockSpec((None, slab, d), lambda j, i: (layer, j * nm + i, 0))],
        out_specs=[pl.BlockSpec((bm, aw), lambda j, i: (i, j)),
                   pl.BlockSpec((slab, d), lambda j, i: (j * nm + i, 0))],
        out_shape=[jax.ShapeDtypeStruct((m, 2 * aw), BF16), jax.ShapeDtypeStruct((f, d), BF16)],
        scratch_shapes=[pltpu.VMEM((d, aw), BF16)],
        compiler_params=_params(2),
        name="qk_proj",
    )(h, w_in, gains, _segment_ones(gw), w_down)


def _conv_kernel(h_ref, wb_ref, wc_ref, wi_ref, wv_ref, cw_ref, o_ref, v_ref, wbf_ref, ubuf_ref,
                 *, tiles_per_seq):
    i = pl.program_id(1)
    bm = h_ref.shape[0]

    @pl.when(i == 0)
    def _():
        _cast_weight(wb_ref, wbf_ref.at[0])
        _cast_weight(wc_ref, wbf_ref.at[1])
        _cast_weight(wi_ref, wbf_ref.at[2])
        _cast_weight(wv_ref, wbf_ref.at[3])

    @pl.when(i % tiles_per_seq == 0)
    def _():
        ubuf_ref[0:SUBLANES, :] = jnp.zeros((SUBLANES, ubuf_ref.shape[1]), F32)

    h = h_ref[...]
    gate_c = jnp.dot(h, wbf_ref[1], preferred_element_type=F32)
    conv_in = jnp.dot(h, wbf_ref[2], preferred_element_type=F32)
    ubuf_ref[SUBLANES:SUBLANES + bm, :] = gate_c * conv_in
    gate_b = jnp.dot(h, wbf_ref[0], preferred_element_type=F32)
    cw = cw_ref[...]
    conv = (cw[0:1] * ubuf_ref[SUBLANES - 2:SUBLANES - 2 + bm, :]
            + cw[1:2] * ubuf_ref[SUBLANES - 1:SUBLANES - 1 + bm, :]
            + cw[2:3] * ubuf_ref[SUBLANES:SUBLANES + bm, :])
    o_ref[...] = (gate_b * conv).astype(o_ref.dtype)
    ubuf_ref[0:SUBLANES, :] = ubuf_ref[bm:bm + SUBLANES, :]
    v_ref[...] = jnp.dot(h, wbf_ref[3], preferred_element_type=F32).astype(v_ref.dtype)


def _conv_v_proj(h, w_in, conv_w, layer, aw, cwid, seq):
    m, d = h.shape
    assert aw == cwid
    bm = _tile(seq, 1024)
    bn = _tile(cwid, 512)
    off = 3 * aw // bn
    nb = cwid // bn
    return pl.pallas_call(
        functools.partial(_conv_kernel, tiles_per_seq=seq // bm),
        grid=(nb, m // bm),
        in_specs=[pl.BlockSpec((bm, d), lambda j, i: (i, 0)),
                  pl.BlockSpec((None, d, bn), lambda j, i: (layer, 0, off + j)),
                  pl.BlockSpec((None, d, bn), lambda j, i: (layer, 0, off + nb + j)),
                  pl.BlockSpec((None, d, bn), lambda j, i: (layer, 0, off + 2 * nb + j)),
                  pl.BlockSpec((None, d, bn), lambda j, i: (layer, 0, off - nb + j)),
                  pl.BlockSpec((None, CONV_K, bn), lambda j, i: (layer, 0, j))],
        out_specs=[pl.BlockSpec((bm, bn), lambda j, i: (i, j)),
                   pl.BlockSpec((bm, bn), lambda j, i: (i, j))],
        out_shape=[jax.ShapeDtypeStruct((m, cwid), BF16), jax.ShapeDtypeStruct((m, aw), BF16)],
        scratch_shapes=[pltpu.VMEM((4, d, bn), BF16),
                        pltpu.VMEM((bm + SUBLANES, bn), F32)],
        compiler_params=_params(2),
        name="conv_v_proj",
    )(h, w_in, w_in, w_in, w_in, conv_w)


def _bucket(n):
    nf = np.maximum(n, MAX_EXACT).astype(np.float64)
    val = np.log(nf / MAX_EXACT) / math.log(MAX_DISTANCE / MAX_EXACT) * (NUM_BUCKETS - MAX_EXACT)
    large = np.minimum(MAX_EXACT + val.astype(np.int64), NUM_BUCKETS - 1)
    return np.where(n < MAX_EXACT, n, large)


def _bucket_tables(t):
    assert _bucket(np.array([t + 1]))[0] == NUM_BUCKETS - 1
    r = np.arange(t)[:, None]
    c = np.arange(t)[None, :]
    d0 = r - c
    diag = np.where(d0 >= 0, _bucket(np.maximum(d0, 0)), -1)
    prev = _bucket(t + r - c)
    return np.stack([diag, prev]).astype(np.int32)


def _bias_table_kernel(rb_ref, bk_ref, o_ref):
    h = pl.program_id(0)
    t = bk_ref.shape[1]
    for mode in range(2):
        bk = bk_ref[mode]
        for mp in range(2):
            col = 2 * h + mp
            far = rb_ref[NUM_BUCKETS - 1, col]
            acc = jnp.where(bk < 0, NEG, 0.0).astype(F32)
            for b in range(NUM_BUCKETS - 1):
                acc = jnp.where(bk == b, (rb_ref[b, col] - far) * LOG2E, acc)
            o_ref[0, mp, :, (1 - mode) * t:(2 - mode) * t] = acc


def _bias_tables(rel_bias, n_heads, t):
    buckets = jnp.asarray(_bucket_tables(t))
    return pl.pallas_call(
        _bias_table_kernel,
        grid=(n_heads,),
        in_specs=[pl.BlockSpec(memory_space=pltpu.SMEM),
                  pl.BlockSpec((2, t, t), lambda h: (0, 0, 0))],
        out_specs=pl.BlockSpec((1, 2, t, 2 * t), lambda h: (h, 0, 0, 0)),
        out_shape=jax.ShapeDtypeStruct((n_heads, 2, t, 2 * t), F32),
        compiler_params=_params(1),
        name="bias_tables",
    )(rel_bias, buckets)


def _fold_lanes(x, op):
    return functools.reduce(op, [x[:, c:c + LANES] for c in range(0, x.shape[1], LANES)])


def _attn_kernel(q_ref, k_ref, v_ref, tb_ref, lam_ref, g_ref, o_ref, v1_ref, *, t):
    seq = q_ref.shape[0]
    heads = q_ref.shape[1] // HEAD_DV
    for hh in range(heads):
        v1_ref[hh, :, 0:HEAD_DV] = v_ref[:, hh * HEAD_DV:(hh + 1) * HEAD_DV]
        v1_ref[hh, :, HEAD_DV:] = jnp.ones((seq, LANES), BF16)
    lo = lax.broadcasted_iota(jnp.int32, (1, LANES), 1) < HEAD_DK
    lv = lam_ref[...]
    lam = (jnp.exp(jnp.sum(lv[0:1] * lv[1:2], axis=-1, keepdims=True))
           - jnp.exp(jnp.sum(lv[2:3] * lv[3:4], axis=-1, keepdims=True)) + lv[4:5, 0:1])
    out_gain = lv[5:6, 0:1]
    g = g_ref[...]
    contract_last = (((1,), (1,)), ((), ()))

    def tile(hh, i):
        cols = slice(hh * HEAD_DV, (hh + 1) * HEAD_DV)
        q = q_ref[i * t:(i + 1) * t, cols]
        zero = jnp.zeros_like(q)
        q2 = jnp.concatenate([jnp.where(lo, q, zero), jnp.where(lo, zero, q)], axis=0)

        pieces = []
        if i >= 2:
            pieces.append((0, (i - 1) * t, False))
        near = min(i, 1) * t
        pieces.append((i * t - near, near + t, True))

        scores = []
        for start, width, biased in pieces:
            s = lax.dot_general(q2, k_ref[start:start + width, cols], contract_last,
                                preferred_element_type=F32)
            if biased:
                s = s + tb_ref[hh, :, :, 2 * t - width:].reshape(2 * t, width)
            scores.append(s)
        m = functools.reduce(jnp.maximum, [_fold_lanes(s, jnp.maximum) for s in scores])
        m = jnp.max(m, axis=-1, keepdims=True)
        acc = None
        for s, (start, width, _) in zip(scores, pieces):
            p = jnp.exp2(s - m).astype(BF16)
            pv = jnp.dot(p, v1_ref[hh, start:start + width, :], preferred_element_type=F32)
            acc = pv if acc is None else acc + pv
        out = acc[:, 0:HEAD_DV] / acc[:, HEAD_DV:]
        o = out[0:t] - lam * out[t:2 * t]
        ms = jnp.mean(o * o, axis=-1, keepdims=True)
        o_ref[i * t:(i + 1) * t, cols] = (o * lax.rsqrt(ms + EPS) * g * out_gain).astype(o_ref.dtype)

    nq = seq // t
    for i in [0] + list(range(nq - 1, 0, -1)):
        for hh in range(heads):
            tile(hh, i)


def _attention(qk, v, tables, lam_vecs, subln_g, batch, seq, n_heads, t):
    hps = 2 if n_heads % 2 == 0 else 1
    w = hps * HEAD_DV
    ng = n_heads // hps
    return pl.pallas_call(
        functools.partial(_attn_kernel, t=t),
        grid=(batch, ng),
        in_specs=[pl.BlockSpec((seq, w), lambda b, h: (b, h)),
                  pl.BlockSpec((seq, w), lambda b, h: (b, ng + h)),
                  pl.BlockSpec((seq, w), lambda b, h: (b, h)),
                  pl.BlockSpec((hps, 2, t, 2 * t), lambda b, h: (h, 0, 0, 0)),
                  pl.BlockSpec((6, HEAD_DK), lambda b, h: (0, 0)),
                  pl.BlockSpec((1, HEAD_DV), lambda b, h: (0, 0))],
        out_specs=pl.BlockSpec((seq, w), lambda b, h: (b, h)),
        out_shape=jax.ShapeDtypeStruct((batch * seq, n_heads * HEAD_DV), BF16),
        scratch_shapes=[pltpu.VMEM((hps, seq, HEAD_DV + LANES), BF16)],
        compiler_params=_params(2),
        name="diff_attention",
    )(qk, qk, v, tables, lam_vecs, subln_g)


def _out_kernel(a_ref, c_ref, w_ref, x_ref, g_ref, xo_ref, h_ref, wbf_ref, *, n_pro):
    s = pl.program_id(0)
    ck = w_ref.shape[0]

    @pl.when(s < n_pro)
    def _():
        wbf_ref[pl.ds(pl.multiple_of(s * ck, ck), ck), :] = w_ref[...].astype(BF16)

    @pl.when(s >= n_pro)
    def _():
        aw = a_ref.shape[1]
        acc = jnp.dot(a_ref[...], wbf_ref[0:aw, :], preferred_element_type=F32)
        acc = acc + jnp.dot(c_ref[...], wbf_ref[aw:, :], preferred_element_type=F32)
        xn = x_ref[...] + acc
        xo_ref[...] = xn
        ms = jnp.mean(xn * xn, axis=-1, keepdims=True)
        h_ref[...] = (xn * lax.rsqrt(ms + EPS) * g_ref[...]).astype(h_ref.dtype)


def _out_proj(attn, conv, w_out, layer, x, g):
    m, d = x.shape
    aw, cwid = attn.shape[1], conv.shape[1]
    bm = _tile(m, 512)
    ck = _tile(aw + cwid, 512)
    n_pro = (aw + cwid) // ck

    def row(s):
        return jnp.maximum(s - n_pro, 0)

    return pl.pallas_call(
        functools.partial(_out_kernel, n_pro=n_pro),
        grid=(n_pro + m // bm,),
        in_specs=[pl.BlockSpec((bm, aw), lambda s: (row(s), 0)),
                  pl.BlockSpec((bm, cwid), lambda s: (row(s), 0)),
                  pl.BlockSpec((None, ck, d), lambda s: (layer, jnp.minimum(s, n_pro - 1), 0)),
                  pl.BlockSpec((bm, d), lambda s: (row(s), 0)),
                  pl.BlockSpec((1, d), lambda s: (0, 0))],
        out_specs=[pl.BlockSpec((bm, d), lambda s: (row(s), 0)),
                   pl.BlockSpec((bm, d), lambda s: (row(s), 0))],
        out_shape=[jax.ShapeDtypeStruct((m, d), F32), jax.ShapeDtypeStruct((m, d), BF16)],
        scratch_shapes=[pltpu.VMEM((aw + cwid, d), BF16)],
        compiler_params=_params(1),
        name="out_proj",
    )(attn, conv, w_out, x, g.reshape(1, d))


def _up_kernel(h_ref, w_ref, o_ref, wbf_ref):
    @pl.when(pl.program_id(1) == 0)
    def _():
        _cast_weight(w_ref, wbf_ref)

    acc = jnp.dot(h_ref[...], wbf_ref[...], preferred_element_type=F32)
    o_ref[...] = jnp.square(jnp.maximum(acc, 0.0)).astype(o_ref.dtype)


def _up_proj(h, w_up, layer):
    m, d = h.shape
    f = w_up.shape[2]
    bm = _tile(m, 2048)
    bn = _tile(f, 1024)
    return pl.pallas_call(
        _up_kernel,
        grid=(f // bn, m // bm),
        in_specs=[pl.BlockSpec((bm, d), lambda j, i: (i, 0)),
                  pl.BlockSpec((None, d, bn), lambda j, i: (layer, 0, j))],
        out_specs=pl.BlockSpec((bm, bn), lambda j, i: (i, j)),
        out_shape=jax.ShapeDtypeStruct((m, f), BF16),
        scratch_shapes=[pltpu.VMEM((d, bn), BF16)],
        compiler_params=_params(2),
        name="up_proj",
    )(h, w_up)


def _down_kernel(a_ref, w_ref, x_ref, *rest, nk, with_norm):
    if with_norm:
        g_ref, xo_ref, h_ref = rest
    else:
        (xo_ref,) = rest
    k = pl.program_id(1)

    def product():
        return jnp.dot(a_ref[...], w_ref[...], preferred_element_type=F32)

    def finish(xn):
        xo_ref[...] = xn
        if with_norm:
            ms = jnp.mean(xn * xn, axis=-1, keepdims=True)
            h_ref[...] = (xn * lax.rsqrt(ms + EPS) * g_ref[...]).astype(h_ref.dtype)

    if nk == 1:
        finish(x_ref[...] + product())
        return

    @pl.when(k == 0)
    def _():
        xo_ref[...] = x_ref[...] + product()

    @pl.when((k > 0) & (k < nk - 1))
    def _():
        xo_ref[...] += product()

    @pl.when(k == nk - 1)
    def _():
        finish(xo_ref[...] + product())


def _down_proj(a, w_down_bf, x, g_next):
    m, d = x.shape
    f = a.shape[1]
    bm = _tile(m, 1024)
    bk = _tile(f, 1024)
    with_norm = g_next is not None
    row = pl.BlockSpec((bm, d), lambda i, k: (i, 0))
    in_specs = [pl.BlockSpec((bm, bk), lambda i, k: (i, k)),
                pl.BlockSpec((bk, d), lambda i, k: (k, 0)),
                row]
    args = [a, w_down_bf, x]
    out_specs = [row]
    out_shape = [jax.ShapeDtypeStruct((m, d), F32)]
    if with_norm:
        in_specs.append(pl.BlockSpec((1, d), lambda i, k: (0, 0)))
        args.append(g_next.reshape(1, d))
        out_specs.append(row)
        out_shape.append(jax.ShapeDtypeStruct((m, d), BF16))
    outs = pl.pallas_call(
        functools.partial(_down_kernel, nk=f // bk, with_norm=with_norm),
        grid=(m // bm, f // bk),
        in_specs=in_specs,
        out_specs=out_specs,
        out_shape=out_shape,
        compiler_params=_params(2),
        name="down_proj",
    )(*args)
    return (outs[0], outs[1]) if with_norm else (outs[0], None)


def kernel(x, w_in, w_out, conv_w, q_norm_g, k_norm_g, lambda_q1, lambda_k1, lambda_q2, lambda_k2,
           subln_g, attn_norm_g, mlp_norm_g, w_up, w_down, rel_bias):
    batch, seq, d = x.shape
    depth = w_in.shape[0]
    aw = d // 2
    cwid = d - aw
    n_heads = aw // HEAD_DV
    assert w_in.shape[2] == 3 * aw + 3 * cwid and rel_bias.shape == (NUM_BUCKETS, 2 * n_heads)
    t = _tile(seq, 256)

    tables = _bias_tables(rel_bias, n_heads, t)
    scale = HEAD_DK ** -0.5 * LOG2E
    xf = x.reshape(batch * seq, d)
    h = _rmsnorm(xf, attn_norm_g[0])
    for l in range(depth):
        lam_init = 0.8 - 0.6 * math.exp(-0.3 * l)
        qk_gains = jnp.stack([jnp.tile(q_norm_g[l] * scale, 2), jnp.tile(k_norm_g[l], 2)])
        lam_vecs = jnp.stack([lambda_q1[l], lambda_k1[l], lambda_q2[l], lambda_k2[l],
                              jnp.full((HEAD_DK,), lam_init, F32),
                              jnp.full((HEAD_DK,), 1.0 - lam_init, F32)])

        qk, w_down_bf = _qk_proj(h, w_in, l, qk_gains.reshape(2, 1, LANES), aw, w_down)
        conv, v = _conv_v_proj(h, w_in, conv_w, l, aw, cwid, seq)
        attn = _attention(qk, v, tables, lam_vecs, subln_g[l].reshape(1, HEAD_DV),
                          batch, seq, n_heads, t)
        xf, hm = _out_proj(attn, conv, w_out, l, xf, mlp_norm_g[l])
        act = _up_proj(hm, w_up, l)
        xf, h = _down_proj(act, w_down_bf, xf, attn_norm_g[l + 1] if l + 1 < depth else None)
    return xf.reshape(batch, seq, d)
```

```python
import functools
import math

import numpy as np
import jax
import jax.numpy as jnp
from jax import lax
from jax.experimental import pallas as pl
from jax.experimental.pallas import tpu as pltpu

HEAD_DV = 128
HEAD_DK = HEAD_DV // 2
CONV_K = 3
NUM_BUCKETS = 32
MAX_DISTANCE = 128
MAX_EXACT = NUM_BUCKETS // 2
EPS = 1e-6
NEG = -1e30
LOG2E = math.log2(math.e)

LANES = 128
SUBLANES = 8
MXU_DIM = 256
VMEM_LIMIT_BYTES = 60 * 1024 * 1024

TILE = dict(
    norm_rows=512,
    cast_rows=256,
    qk_rows=1024,
    conv_rows=1024, conv_cols=512,
    attn=256,
    out_rows=512, out_cast_rows=512,
    up_rows=2048, up_cols=1024,
    down_rows=1024, down_k=1024,
)

F32 = jnp.float32
BF16 = jnp.bfloat16


def _tile(dim, pref):
    if dim <= pref:
        return dim
    t = (pref // LANES) * LANES
    while dim % t:
        t -= LANES
    return t


def _params(n_axes):
    return pltpu.CompilerParams(
        dimension_semantics=("arbitrary",) * n_axes,
        vmem_limit_bytes=VMEM_LIMIT_BYTES,
    )


def _rmsnorm_kernel(x_ref, g_ref, o_ref):
    x = x_ref[...]
    ms = jnp.mean(x * x, axis=-1, keepdims=True)
    o_ref[...] = (x * lax.rsqrt(ms + EPS) * g_ref[...]).astype(o_ref.dtype)


def _rmsnorm(x, g):
    m, d = x.shape
    tm = _tile(m, TILE["norm_rows"])
    return pl.pallas_call(
        _rmsnorm_kernel,
        grid=(m // tm,),
        in_specs=[pl.BlockSpec((tm, d), lambda i: (i, 0)),
                  pl.BlockSpec((1, d), lambda i: (0, 0))],
        out_specs=pl.BlockSpec((tm, d), lambda i: (i, 0)),
        out_shape=jax.ShapeDtypeStruct((m, d), BF16),
        compiler_params=_params(1),
        name="rmsnorm",
    )(x, g.reshape(1, d))


def _cast_weight(w_ref, wbf_ref):
    k = w_ref.shape[0]
    ck = _tile(k, TILE["cast_rows"])
    for r in range(0, k, ck):
        wbf_ref[r:r + ck, :] = w_ref[r:r + ck, :].astype(BF16)


def _segment_ones(width):
    seg = np.arange(width) // HEAD_DK
    return jnp.asarray(seg[:, None] == seg[None, :], dtype=BF16)


def _qk_kernel(h_ref, w_ref, g_ref, e_ref, wd_ref, o_ref, wdo_ref, wbf_ref):
    @pl.when(pl.program_id(1) == 0)
    def _():
        _cast_weight(w_ref, wbf_ref)

    wdo_ref[...] = wd_ref[...].astype(BF16)
    acc = jnp.dot(h_ref[...], wbf_ref[...], preferred_element_type=F32)
    g = g_ref[...]
    gw = e_ref.shape[0]
    for c in range(0, acc.shape[1], gw):
        a = acc[:, c:c + gw]
        ssq = jnp.dot((a * a).astype(BF16), e_ref[...], preferred_element_type=F32)
        inv = lax.rsqrt(ssq * (1.0 / HEAD_DK) + EPS)
        for cc in range(0, gw, LANES):
            o_ref[:, c + cc:c + cc + LANES] = (
                a[:, cc:cc + LANES] * inv[:, cc:cc + LANES] * g).astype(o_ref.dtype)


def _qk_proj(h, w_in, layer, gains, aw, w_down):
    m, d = h.shape
    bm = _tile(m, TILE["qk_rows"])
    nm = m // bm
    gw = _tile(aw, MXU_DIM)
    f = w_down.shape[1]
    slab = f // (2 * nm)
    assert f % (2 * nm) == 0 and slab % (2 * SUBLANES) == 0
    return pl.pallas_call(
        _qk_kernel,
        grid=(2, nm),
        in_specs=[pl.BlockSpec((bm, d), lambda j, i: (i, 0)),
                  pl.BlockSpec((None, d, aw), lambda j, i: (layer, 0, j)),
                  pl.BlockSpec((None, 1, LANES), lambda j, i: (j, 0, 0)),
                  pl.BlockSpec((gw, gw), lambda j, i: (0, 0)),
                  pl.BlockSpec((None, slab, d), lambda j, i: (layer, j * nm + i, 0))],
        out_specs=[pl.BlockSpec((bm, aw), lambda j, i: (i, j)),
                   pl.BlockSpec((slab, d), lambda j, i: (j * nm + i, 0))],
        out_shape=[jax.ShapeDtypeStruct((m, 2 * aw), BF16), jax.ShapeDtypeStruct((f, d), BF16)],
        scratch_shapes=[pltpu.VMEM((d, aw), BF16)],
        compiler_params=_params(2),
        name="qk_proj",
    )(h, w_in, gains, _segment_ones(gw), w_down)


def _conv_kernel(h_ref, wb_ref, wc_ref, wi_ref, wv_ref, cw_ref, o_ref, v_ref, wbf_ref, ubuf_ref,
                 *, tiles_per_seq):
    i = pl.program_id(1)
    bm = h_ref.shape[0]

    @pl.when(i == 0)
    def _():
        _cast_weight(wb_ref, wbf_ref.at[0])
        _cast_weight(wc_ref, wbf_ref.at[1])
        _cast_weight(wi_ref, wbf_ref.at[2])
        _cast_weight(wv_ref, wbf_ref.at[3])

    @pl.when(i % tiles_per_seq == 0)
    def _():
        ubuf_ref[0:SUBLANES, :] = jnp.zeros((SUBLANES, ubuf_ref.shape[1]), F32)

    h = h_ref[...]
    gate_c = jnp.dot(h, wbf_ref[1], preferred_element_type=F32)
    conv_in = jnp.dot(h, wbf_ref[2], preferred_element_type=F32)
    ubuf_ref[SUBLANES:SUBLANES + bm, :] = gate_c * conv_in
    gate_b = jnp.dot(h, wbf_ref[0], preferred_element_type=F32)
    cw = cw_ref[...]
    conv = (cw[0:1] * ubuf_ref[SUBLANES - 2:SUBLANES - 2 + bm, :]
            + cw[1:2] * ubuf_ref[SUBLANES - 1:SUBLANES - 1 + bm, :]
            + cw[2:3] * ubuf_ref[SUBLANES:SUBLANES + bm, :])
    o_ref[...] = (gate_b * conv).astype(o_ref.dtype)
    ubuf_ref[0:SUBLANES, :] = ubuf_ref[bm:bm + SUBLANES, :]
    v_ref[...] = jnp.dot(h, wbf_ref[3], preferred_element_type=F32).astype(v_ref.dtype)


def _conv_v_proj(h, w_in, conv_w, layer, aw, cwid, seq):
    m, d = h.shape
    assert aw == cwid
    bm = _tile(seq, TILE["conv_rows"])
    bn = _tile(cwid, TILE["conv_cols"])
    off = 3 * aw // bn
    nb = cwid // bn
    return pl.pallas_call(
        functools.partial(_conv_kernel, tiles_per_seq=seq // bm),
        grid=(nb, m // bm),
        in_specs=[pl.BlockSpec((bm, d), lambda j, i: (i, 0)),
                  pl.BlockSpec((None, d, bn), lambda j, i: (layer, 0, off + j)),
                  pl.BlockSpec((None, d, bn), lambda j, i: (layer, 0, off + nb + j)),
                  pl.BlockSpec((None, d, bn), lambda j, i: (layer, 0, off + 2 * nb + j)),
                  pl.BlockSpec((None, d, bn), lambda j, i: (layer, 0, off - nb + j)),
                  pl.BlockSpec((None, CONV_K, bn), lambda j, i: (layer, 0, j))],
        out_specs=[pl.BlockSpec((bm, bn), lambda j, i: (i, j)),
                   pl.BlockSpec((bm, bn), lambda j, i: (i, j))],
        out_shape=[jax.ShapeDtypeStruct((m, cwid), BF16), jax.ShapeDtypeStruct((m, aw), BF16)],
        scratch_shapes=[pltpu.VMEM((4, d, bn), BF16),
                        pltpu.VMEM((bm + SUBLANES, bn), F32)],
        compiler_params=_params(2),
        name="conv_v_proj",
    )(h, w_in, w_in, w_in, w_in, conv_w)


def _bucket(n):
    nf = np.maximum(n, MAX_EXACT).astype(np.float64)
    val = np.log(nf / MAX_EXACT) / math.log(MAX_DISTANCE / MAX_EXACT) * (NUM_BUCKETS - MAX_EXACT)
    large = np.minimum(MAX_EXACT + val.astype(np.int64), NUM_BUCKETS - 1)
    return np.where(n < MAX_EXACT, n, large)


def _bucket_tables(t):
    assert _bucket(np.array([t + 1]))[0] == NUM_BUCKETS - 1
    r = np.arange(t)[:, None]
    c = np.arange(t)[None, :]
    d0 = r - c
    diag = np.where(d0 >= 0, _bucket(np.maximum(d0, 0)), -1)
    prev = _bucket(t + r - c)
    return np.stack([diag, prev]).astype(np.int32)


def _bias_table_kernel(rb_ref, bk_ref, o_ref):
    h = pl.program_id(0)
    t = bk_ref.shape[1]
    for mode in range(2):
        bk = bk_ref[mode]
        for mp in range(2):
            col = 2 * h + mp
            far = rb_ref[NUM_BUCKETS - 1, col]
            acc = jnp.where(bk < 0, NEG, 0.0).astype(F32)
            for b in range(NUM_BUCKETS - 1):
                acc = jnp.where(bk == b, (rb_ref[b, col] - far) * LOG2E, acc)
            o_ref[0, mp, :, (1 - mode) * t:(2 - mode) * t] = acc


def _bias_tables(rel_bias, n_heads, t):
    buckets = jnp.asarray(_bucket_tables(t))
    return pl.pallas_call(
        _bias_table_kernel,
        grid=(n_heads,),
        in_specs=[pl.BlockSpec(memory_space=pltpu.SMEM),
                  pl.BlockSpec((2, t, t), lambda h: (0, 0, 0))],
        out_specs=pl.BlockSpec((1, 2, t, 2 * t), lambda h: (h, 0, 0, 0)),
        out_shape=jax.ShapeDtypeStruct((n_heads, 2, t, 2 * t), F32),
        compiler_params=_params(1),
        name="bias_tables",
    )(rel_bias, buckets)


def _fold_lanes(x, op):
    return functools.reduce(op, [x[:, c:c + LANES] for c in range(0, x.shape[1], LANES)])


def _attn_kernel(q_ref, k_ref, v_ref, tb_ref, lam_ref, g_ref, o_ref, v1_ref, *, t):
    seq = q_ref.shape[0]
    heads = q_ref.shape[1] // HEAD_DV
    for hh in range(heads):
        v1_ref[hh, :, 0:HEAD_DV] = v_ref[:, hh * HEAD_DV:(hh + 1) * HEAD_DV]
        v1_ref[hh, :, HEAD_DV:] = jnp.ones((seq, LANES), BF16)
    lo = lax.broadcasted_iota(jnp.int32, (1, LANES), 1) < HEAD_DK
    lv = lam_ref[...]
    lam = (jnp.exp(jnp.sum(lv[0:1] * lv[1:2], axis=-1, keepdims=True))
           - jnp.exp(jnp.sum(lv[2:3] * lv[3:4], axis=-1, keepdims=True)) + lv[4:5, 0:1])
    out_gain = lv[5:6, 0:1]
    g = g_ref[...]
    contract_last = (((1,), (1,)), ((), ()))

    def tile(hh, i):
        cols = slice(hh * HEAD_DV, (hh + 1) * HEAD_DV)
        q = q_ref[i * t:(i + 1) * t, cols]
        zero = jnp.zeros_like(q)
        q2 = jnp.concatenate([jnp.where(lo, q, zero), jnp.where(lo, zero, q)], axis=0)

        pieces = []
        if i >= 2:
            pieces.append((0, (i - 1) * t, False))
        near = min(i, 1) * t
        pieces.append((i * t - near, near + t, True))

        scores = []
        for start, width, biased in pieces:
            s = lax.dot_general(q2, k_ref[start:start + width, cols], contract_last,
                                preferred_element_type=F32)
            if biased:
                s = s + tb_ref[hh, :, :, 2 * t - width:].reshape(2 * t, width)
            scores.append(s)
        m = functools.reduce(jnp.maximum, [_fold_lanes(s, jnp.maximum) for s in scores])
        m = jnp.max(m, axis=-1, keepdims=True)
        acc = None
        for s, (start, width, _) in zip(scores, pieces):
            p = jnp.exp2(s - m).astype(BF16)
            pv = jnp.dot(p, v1_ref[hh, start:start + width, :], preferred_element_type=F32)
            acc = pv if acc is None else acc + pv
        out = acc[:, 0:HEAD_DV] / acc[:, HEAD_DV:]
        o = out[0:t] - lam * out[t:2 * t]
        ms = jnp.mean(o * o, axis=-1, keepdims=True)
        o_ref[i * t:(i + 1) * t, cols] = (o * lax.rsqrt(ms + EPS) * g * out_gain).astype(o_ref.dtype)

    nq = seq // t
    for i in [0] + list(range(nq - 1, 0, -1)):
        for hh in range(heads):
            tile(hh, i)


def _attention(qk, v, tables, lam_vecs, subln_g, batch, seq, n_heads, t):
    hps = 2 if n_heads % 2 == 0 else 1
    w = hps * HEAD_DV
    ng = n_heads // hps
    return pl.pallas_call(
        functools.partial(_attn_kernel, t=t),
        grid=(batch, ng),
        in_specs=[pl.BlockSpec((seq, w), lambda b, h: (b, h)),
                  pl.BlockSpec((seq, w), lambda b, h: (b, ng + h)),
                  pl.BlockSpec((seq, w), lambda b, h: (b, h)),
                  pl.BlockSpec((hps, 2, t, 2 * t), lambda b, h: (h, 0, 0, 0)),
                  pl.BlockSpec((6, HEAD_DK), lambda b, h: (0, 0)),
                  pl.BlockSpec((1, HEAD_DV), lambda b, h: (0, 0))],
        out_specs=pl.BlockSpec((seq, w), lambda b, h: (b, h)),
        out_shape=jax.ShapeDtypeStruct((batch * seq, n_heads * HEAD_DV), BF16),
        scratch_shapes=[pltpu.VMEM((hps, seq, HEAD_DV + LANES), BF16)],
        compiler_params=_params(2),
        name="diff_attention",
    )(qk, qk, v, tables, lam_vecs, subln_g)


def _out_kernel(a_ref, c_ref, w_ref, x_ref, g_ref, xo_ref, h_ref, wbf_ref, *, n_pro):
    s = pl.program_id(0)
    ck = w_ref.shape[0]

    @pl.when(s < n_pro)
    def _():
        wbf_ref[pl.ds(pl.multiple_of(s * ck, ck), ck), :] = w_ref[...].astype(BF16)

    @pl.when(s >= n_pro)
    def _():
        aw = a_ref.shape[1]
        acc = jnp.dot(a_ref[...], wbf_ref[0:aw, :], preferred_element_type=F32)
        acc = acc + jnp.dot(c_ref[...], wbf_ref[aw:, :], preferred_element_type=F32)
        xn = x_ref[...] + acc
        xo_ref[...] = xn
        ms = jnp.mean(xn * xn, axis=-1, keepdims=True)
        h_ref[...] = (xn * lax.rsqrt(ms + EPS) * g_ref[...]).astype(h_ref.dtype)


def _out_proj(attn, conv, w_out, layer, x, g):
    m, d = x.shape
    aw, cwid = attn.shape[1], conv.shape[1]
    bm = _tile(m, TILE["out_rows"])
    ck = _tile(aw + cwid, TILE["out_cast_rows"])
    n_pro = (aw + cwid) // ck

    def row(s):
        return jnp.maximum(s - n_pro, 0)

    return pl.pallas_call(
        functools.partial(_out_kernel, n_pro=n_pro),
        grid=(n_pro + m // bm,),
        in_specs=[pl.BlockSpec((bm, aw), lambda s: (row(s), 0)),
                  pl.BlockSpec((bm, cwid), lambda s: (row(s), 0)),
                  pl.BlockSpec((None, ck, d), lambda s: (layer, jnp.minimum(s, n_pro - 1), 0)),
                  pl.BlockSpec((bm, d), lambda s: (row(s), 0)),
                  pl.BlockSpec((1, d), lambda s: (0, 0))],
        out_specs=[pl.BlockSpec((bm, d), lambda s: (row(s), 0)),
                   pl.BlockSpec((bm, d), lambda s: (row(s), 0))],
        out_shape=[jax.ShapeDtypeStruct((m, d), F32), jax.ShapeDtypeStruct((m, d), BF16)],
        scratch_shapes=[pltpu.VMEM((aw + cwid, d), BF16)],
        compiler_params=_params(1),
        name="out_proj",
    )(attn, conv, w_out, x, g.reshape(1, d))


def _up_kernel(h_ref, w_ref, o_ref, wbf_ref):
    @pl.when(pl.program_id(1) == 0)
    def _():
        _cast_weight(w_ref, wbf_ref)

    acc = jnp.dot(h_ref[...], wbf_ref[...], preferred_element_type=F32)
    o_ref[...] = jnp.square(jnp.maximum(acc, 0.0)).astype(o_ref.dtype)


def _up_proj(h, w_up, layer):
    m, d = h.shape
    f = w_up.shape[2]
    bm = _tile(m, TILE["up_rows"])
    bn = _tile(f, TILE["up_cols"])
    return pl.pallas_call(
        _up_kernel,
        grid=(f // bn, m // bm),
        in_specs=[pl.BlockSpec((bm, d), lambda j, i: (i, 0)),
                  pl.BlockSpec((None, d, bn), lambda j, i: (layer, 0, j))],
        out_specs=pl.BlockSpec((bm, bn), lambda j, i: (i, j)),
        out_shape=jax.ShapeDtypeStruct((m, f), BF16),
        scratch_shapes=[pltpu.VMEM((d, bn), BF16)],
        compiler_params=_params(2),
        name="up_proj",
    )(h, w_up)


def _down_kernel(a_ref, w_ref, x_ref, *rest, with_norm):
    if with_norm:
        g_ref, xo_ref, h_ref = rest
    else:
        (xo_ref,) = rest
    k = pl.program_id(1)
    last = pl.num_programs(1) - 1

    @pl.when(k == 0)
    def _():
        xo_ref[...] = x_ref[...]

    if not with_norm:
        xo_ref[...] += jnp.dot(a_ref[...], w_ref[...], preferred_element_type=F32)
        return

    @pl.when(k < last)
    def _():
        xo_ref[...] += jnp.dot(a_ref[...], w_ref[...], preferred_element_type=F32)

    @pl.when(k == last)
    def _():
        xn = xo_ref[...] + jnp.dot(a_ref[...], w_ref[...], preferred_element_type=F32)
        xo_ref[...] = xn
        ms = jnp.mean(xn * xn, axis=-1, keepdims=True)
        h_ref[...] = (xn * lax.rsqrt(ms + EPS) * g_ref[...]).astype(h_ref.dtype)


def _down_proj(a, w_down_bf, x, g_next):
    m, d = x.shape
    f = a.shape[1]
    bm = _tile(m, TILE["down_rows"])
    bk = _tile(f, TILE["down_k"])
    with_norm = g_next is not None
    row = pl.BlockSpec((bm, d), lambda i, k: (i, 0))
    in_specs = [pl.BlockSpec((bm, bk), lambda i, k: (i, k)),
                pl.BlockSpec((bk, d), lambda i, k: (k, 0)),
                row]
    args = [a, w_down_bf, x]
    out_specs = [row]
    out_shape = [jax.ShapeDtypeStruct((m, d), F32)]
    if with_norm:
        in_specs.append(pl.BlockSpec((1, d), lambda i, k: (0, 0)))
        args.append(g_next.reshape(1, d))
        out_specs.append(row)
        out_shape.append(jax.ShapeDtypeStruct((m, d), BF16))
    outs = pl.pallas_call(
        functools.partial(_down_kernel, with_norm=with_norm),
        grid=(m // bm, f // bk),
        in_specs=in_specs,
        out_specs=out_specs,
        out_shape=out_shape,
        compiler_params=_params(2),
        name="down_proj",
    )(*args)
    return (outs[0], outs[1]) if with_norm else (outs[0], None)


def kernel(x, w_in, w_out, conv_w, q_norm_g, k_norm_g, lambda_q1, lambda_k1, lambda_q2, lambda_k2,
           subln_g, attn_norm_g, mlp_norm_g, w_up, w_down, rel_bias):
    batch, seq, d = x.shape
    depth = w_in.shape[0]
    aw = d // 2
    cwid = d - aw
    n_heads = aw // HEAD_DV
    assert w_in.shape[2] == 3 * aw + 3 * cwid and rel_bias.shape == (NUM_BUCKETS, 2 * n_heads)
    t = _tile(seq, TILE["attn"])

    tables = _bias_tables(rel_bias, n_heads, t)
    scale = HEAD_DK ** -0.5 * LOG2E
    xf = x.reshape(batch * seq, d)
    h = _rmsnorm(xf, attn_norm_g[0])
    for l in range(depth):
        lam_init = 0.8 - 0.6 * math.exp(-0.3 * l)
        qk_gains = jnp.stack([jnp.tile(q_norm_g[l] * scale, 2), jnp.tile(k_norm_g[l], 2)])
        lam_vecs = jnp.stack([lambda_q1[l], lambda_k1[l], lambda_q2[l], lambda_k2[l],
                              jnp.full((HEAD_DK,), lam_init, F32),
                              jnp.full((HEAD_DK,), 1.0 - lam_init, F32)])

        qk, w_down_bf = _qk_proj(h, w_in, l, qk_gains.reshape(2, 1, LANES), aw, w_down)
        conv, v = _conv_v_proj(h, w_in, conv_w, l, aw, cwid, seq)
        attn = _attention(qk, v, tables, lam_vecs, subln_g[l].reshape(1, HEAD_DV),
                          batch, seq, n_heads, t)
        xf, hm = _out_proj(attn, conv, w_out, l, xf, mlp_norm_g[l])
        act = _up_proj(hm, w_up, l)
        xf, h = _down_proj(act, w_down_bf, xf, attn_norm_g[l + 1] if l + 1 < depth else None)
    return xf.reshape(batch, seq, d)
```

```python
import functools
import math

import numpy as np
import jax
import jax.numpy as jnp
from jax import lax
from jax.experimental import pallas as pl
from jax.experimental.pallas import tpu as pltpu

HEAD_DV = 128
HEAD_DK = HEAD_DV // 2
CONV_K = 3
NUM_BUCKETS = 32
MAX_DISTANCE = 128
MAX_EXACT = NUM_BUCKETS // 2
EPS = 1e-6
NEG = -1e30
LOG2E = math.log2(math.e)

LANES = 128
SUBLANES = 8
MXU_DIM = 256
VMEM_LIMIT_BYTES = 60 * 1024 * 1024

TILE = dict(
    norm_rows=512,
    cast_rows=256,
    qk_rows=1024,
    conv_rows=1024, conv_cols=512,
    attn=256,
    out_rows=512, out_cast_rows=512,
    up_rows=2048, up_cols=1024,
    down_rows=1024, down_k=1024,
)

F32 = jnp.float32
BF16 = jnp.bfloat16


def _tile(dim, pref):
    if dim <= pref:
        return dim
    t = (pref // LANES) * LANES
    while dim % t:
        t -= LANES
    return t


def _params(n_axes):
    return pltpu.CompilerParams(
        dimension_semantics=("arbitrary",) * n_axes,
        vmem_limit_bytes=VMEM_LIMIT_BYTES,
    )


def _rmsnorm_kernel(x_ref, g_ref, o_ref):
    x = x_ref[...]
    ms = jnp.mean(x * x, axis=-1, keepdims=True)
    o_ref[...] = (x * lax.rsqrt(ms + EPS) * g_ref[...]).astype(o_ref.dtype)


def _rmsnorm(x, g):
    m, d = x.shape
    tm = _tile(m, TILE["norm_rows"])
    return pl.pallas_call(
        _rmsnorm_kernel,
        grid=(m // tm,),
        in_specs=[pl.BlockSpec((tm, d), lambda i: (i, 0)),
                  pl.BlockSpec((1, d), lambda i: (0, 0))],
        out_specs=pl.BlockSpec((tm, d), lambda i: (i, 0)),
        out_shape=jax.ShapeDtypeStruct((m, d), BF16),
        compiler_params=_params(1),
        name="rmsnorm",
    )(x, g.reshape(1, d))


def _cast_weight(w_ref, wbf_ref):
    k = w_ref.shape[0]
    ck = _tile(k, TILE["cast_rows"])
    for r in range(0, k, ck):
        wbf_ref[r:r + ck, :] = w_ref[r:r + ck, :].astype(BF16)


def _segment_ones(width):
    seg = np.arange(width) // HEAD_DK
    return jnp.asarray(seg[:, None] == seg[None, :], dtype=BF16)


def _qk_kernel(h_ref, w_ref, g_ref, e_ref, wd_ref, o_ref, wdo_ref, wbf_ref):
    @pl.when(pl.program_id(1) == 0)
    def _():
        _cast_weight(w_ref, wbf_ref)

    wdo_ref[...] = wd_ref[...].astype(BF16)
    acc = jnp.dot(h_ref[...], wbf_ref[...], preferred_element_type=F32)
    g = g_ref[...]
    gw = e_ref.shape[0]
    for c in range(0, acc.shape[1], gw):
        a = acc[:, c:c + gw]
        ssq = jnp.dot((a * a).astype(BF16), e_ref[...], preferred_element_type=F32)
        inv = lax.rsqrt(ssq * (1.0 / HEAD_DK) + EPS)
        for cc in range(0, gw, LANES):
            o_ref[:, c + cc:c + cc + LANES] = (
                a[:, cc:cc + LANES] * inv[:, cc:cc + LANES] * g).astype(o_ref.dtype)


def _qk_proj(h, w_in, layer, gains, aw, w_down):
    m, d = h.shape
    bm = _tile(m, TILE["qk_rows"])
    nm = m // bm
    gw = _tile(aw, MXU_DIM)
    f = w_down.shape[1]
    slab = f // (2 * nm)
    assert f % (2 * nm) == 0 and slab % (2 * SUBLANES) == 0
    return pl.pallas_call(
        _qk_kernel,
        grid=(2, nm),
        in_specs=[pl.BlockSpec((bm, d), lambda j, i: (i, 0)),
                  pl.BlockSpec((None, d, aw), lambda j, i: (layer, 0, j)),
                  pl.BlockSpec((None, 1, LANES), lambda j, i: (j, 0, 0)),
                  pl.BlockSpec((gw, gw), lambda j, i: (0, 0)),
                  pl.BlockSpec((None, slab, d), lambda j, i: (layer, j * nm + i, 0))],
        out_specs=[pl.BlockSpec((bm, aw), lambda j, i: (i, j)),
                   pl.BlockSpec((slab, d), lambda j, i: (j * nm + i, 0))],
        out_shape=[jax.ShapeDtypeStruct((m, 2 * aw), BF16), jax.ShapeDtypeStruct((f, d), BF16)],
        scratch_shapes=[pltpu.VMEM((d, aw), BF16)],
        compiler_params=_params(2),
        name="qk_proj",
    )(h, w_in, gains, _segment_ones(gw), w_down)


def _conv_kernel(h_ref, wb_ref, wc_ref, wi_ref, wv_ref, cw_ref, o_ref, v_ref, wbf_ref, ubuf_ref,
                 *, tiles_per_seq):
    i = pl.program_id(1)
    bm = h_ref.shape[0]

    @pl.when(i == 0)
    def _():
        _cast_weight(wb_ref, wbf_ref.at[0])
        _cast_weight(wc_ref, wbf_ref.at[1])
        _cast_weight(wi_ref, wbf_ref.at[2])
        _cast_weight(wv_ref, wbf_ref.at[3])

    @pl.when(i % tiles_per_seq == 0)
    def _():
        ubuf_ref[0:SUBLANES, :] = jnp.zeros((SUBLANES, ubuf_ref.shape[1]), F32)

    h = h_ref[...]
    gate_c = jnp.dot(h, wbf_ref[1], preferred_element_type=F32)
    conv_in = jnp.dot(h, wbf_ref[2], preferred_element_type=F32)
    ubuf_ref[SUBLANES:SUBLANES + bm, :] = gate_c * conv_in
    gate_b = jnp.dot(h, wbf_ref[0], preferred_element_type=F32)
    cw = cw_ref[...]
    conv = (cw[0:1] * ubuf_ref[SUBLANES - 2:SUBLANES - 2 + bm, :]
            + cw[1:2] * ubuf_ref[SUBLANES - 1:SUBLANES - 1 + bm, :]
            + cw[2:3] * ubuf_ref[SUBLANES:SUBLANES + bm, :])
    o_ref[...] = (gate_b * conv).astype(o_ref.dtype)
    ubuf_ref[0:SUBLANES, :] = ubuf_ref[bm:bm + SUBLANES, :]
    v_ref[...] = jnp.dot(h, wbf_ref[3], preferred_element_type=F32).astype(v_ref.dtype)


def _conv_v_proj(h, w_in, conv_w, layer, aw, cwid, seq):
    m, d = h.shape
    assert aw == cwid
    bm = _tile(seq, TILE["conv_rows"])
    bn = _tile(cwid, TILE["conv_cols"])
    off = 3 * aw // bn
    nb = cwid // bn
    return pl.pallas_call(
        functools.partial(_conv_kernel, tiles_per_seq=seq // bm),
        grid=(nb, m // bm),
        in_specs=[pl.BlockSpec((bm, d), lambda j, i: (i, 0)),
                  pl.BlockSpec((None, d, bn), lambda j, i: (layer, 0, off + j)),
                  pl.BlockSpec((None, d, bn), lambda j, i: (layer, 0, off + nb + j)),
                  pl.BlockSpec((None, d, bn), lambda j, i: (layer, 0, off + 2 * nb + j)),
                  pl.BlockSpec((None, d, bn), lambda j, i: (layer, 0, off - nb + j)),
                  pl.BlockSpec((None, CONV_K, bn), lambda j, i: (layer, 0, j))],
        out_specs=[pl.BlockSpec((bm, bn), lambda j, i: (i, j)),
                   pl.BlockSpec((bm, bn), lambda j, i: (i, j))],
        out_shape=[jax.ShapeDtypeStruct((m, cwid), BF16), jax.ShapeDtypeStruct((m, aw), BF16)],
        scratch_shapes=[pltpu.VMEM((4, d, bn), BF16),
                        pltpu.VMEM((bm + SUBLANES, bn), F32)],
        compiler_params=_params(2),
        name="conv_v_proj",
    )(h, w_in, w_in, w_in, w_in, conv_w)


def _bucket(n):
    nf = np.maximum(n, MAX_EXACT).astype(np.float64)
    val = np.log(nf / MAX_EXACT) / math.log(MAX_DISTANCE / MAX_EXACT) * (NUM_BUCKETS - MAX_EXACT)
    large = np.minimum(MAX_EXACT + val.astype(np.int64), NUM_BUCKETS - 1)
    return np.where(n < MAX_EXACT, n, large)


def _bucket_blocks():
    n = LANES
    assert _bucket(np.array([n + 1]))[0] == NUM_BUCKETS - 1
    r = np.arange(n)[:, None]
    c = np.arange(n)[None, :]
    d0 = r - c
    diag = np.where(d0 >= 0, _bucket(np.maximum(d0, 0)), -1)
    sub = _bucket(n + r - c)
    return np.stack([diag, sub]).astype(np.int32)


def _bias_table_kernel(rb_ref, bk_ref, o_ref):
    h = pl.program_id(0)
    n = LANES
    t = o_ref.shape[2]
    zero = jnp.zeros((n, n), F32)
    masked = jnp.full((n, n), NEG, F32)
    for mp in range(2):
        col = 2 * h + mp
        far = rb_ref[NUM_BUCKETS - 1, col]
        blocks = []
        for kind in range(2):
            bk = bk_ref[kind]
            acc = jnp.where(bk < 0, NEG, 0.0).astype(F32)
            for b in range(NUM_BUCKETS - 1):
                acc = jnp.where(bk == b, (rb_ref[b, col] - far) * LOG2E, acc)
            blocks.append(acc)
        diag, sub = blocks
        for r in range(t // n):
            d = t // n + r
            for c in range(2 * t // n):
                blk = diag if c == d else sub if c == d - 1 else masked if c > d else zero
                o_ref[0, mp, r * n:(r + 1) * n, c * n:(c + 1) * n] = blk


def _bias_tables(rel_bias, n_heads, t):
    assert t % LANES == 0
    buckets = jnp.asarray(_bucket_blocks())
    return pl.pallas_call(
        _bias_table_kernel,
        grid=(n_heads,),
        in_specs=[pl.BlockSpec(memory_space=pltpu.SMEM),
                  pl.BlockSpec((2, LANES, LANES), lambda h: (0, 0, 0))],
        out_specs=pl.BlockSpec((1, 2, t, 2 * t), lambda h: (h, 0, 0, 0)),
        out_shape=jax.ShapeDtypeStruct((n_heads, 2, t, 2 * t), F32),
        compiler_params=_params(1),
        name="bias_tables",
    )(rel_bias, buckets)


def _fold_lanes(x, op):
    return functools.reduce(op, [x[:, c:c + LANES] for c in range(0, x.shape[1], LANES)])


def _attn_kernel(q_ref, k_ref, v_ref, tb_ref, lam_ref, g_ref, o_ref, v1_ref, *, t):
    seq = q_ref.shape[0]
    heads = q_ref.shape[1] // HEAD_DV
    for hh in range(heads):
        v1_ref[hh, :, 0:HEAD_DV] = v_ref[:, hh * HEAD_DV:(hh + 1) * HEAD_DV]
        v1_ref[hh, :, HEAD_DV:] = jnp.ones((seq, LANES), BF16)
    lo = lax.broadcasted_iota(jnp.int32, (1, LANES), 1) < HEAD_DK
    lv = lam_ref[...]
    lam = (jnp.exp(jnp.sum(lv[0:1] * lv[1:2], axis=-1, keepdims=True))
           - jnp.exp(jnp.sum(lv[2:3] * lv[3:4], axis=-1, keepdims=True)) + lv[4:5, 0:1])
    out_gain = lv[5:6, 0:1]
    g = g_ref[...]
    contract_last = (((1,), (1,)), ((), ()))

    def tile(hh, i):
        cols = slice(hh * HEAD_DV, (hh + 1) * HEAD_DV)
        q = q_ref[i * t:(i + 1) * t, cols]
        zero = jnp.zeros_like(q)
        q2 = jnp.concatenate([jnp.where(lo, q, zero), jnp.where(lo, zero, q)], axis=0)

        pieces = []
        if i >= 2:
            pieces.append((0, (i - 1) * t, False))
        near = min(i, 1) * t
        pieces.append((i * t - near, near + t, True))

        scores = []
        for start, width, biased in pieces:
            s = lax.dot_general(q2, k_ref[start:start + width, cols], contract_last,
                                preferred_element_type=F32)
            if biased:
                s = s + tb_ref[hh, :, :, 2 * t - width:].reshape(2 * t, width)
            scores.append(s)
        m = functools.reduce(jnp.maximum, [_fold_lanes(s, jnp.maximum) for s in scores])
        m = jnp.max(m, axis=-1, keepdims=True)
        acc = None
        for s, (start, width, _) in zip(scores, pieces):
            p = jnp.exp2(s - m).astype(BF16)
            pv = jnp.dot(p, v1_ref[hh, start:start + width, :], preferred_element_type=F32)
            acc = pv if acc is None else acc + pv
        out = acc[:, 0:HEAD_DV] / acc[:, HEAD_DV:]
        o = out[0:t] - lam * out[t:2 * t]
        ms = jnp.mean(o * o, axis=-1, keepdims=True)
        o_ref[i * t:(i + 1) * t, cols] = (o * lax.rsqrt(ms + EPS) * g * out_gain).astype(o_ref.dtype)

    nq = seq // t
    for i in [0] + list(range(nq - 1, 0, -1)):
        for hh in range(heads):
            tile(hh, i)


def _attention(qk, v, tables, lam_vecs, subln_g, batch, seq, n_heads, t):
    hps = 2 if n_heads % 2 == 0 else 1
    w = hps * HEAD_DV
    ng = n_heads // hps
    return pl.pallas_call(
        functools.partial(_attn_kernel, t=t),
        grid=(batch, ng),
        in_specs=[pl.BlockSpec((seq, w), lambda b, h: (b, h)),
                  pl.BlockSpec((seq, w), lambda b, h: (b, ng + h)),
                  pl.BlockSpec((seq, w), lambda b, h: (b, h)),
                  pl.BlockSpec((hps, 2, t, 2 * t), lambda b, h: (h, 0, 0, 0)),
                  pl.BlockSpec((6, HEAD_DK), lambda b, h: (0, 0)),
                  pl.BlockSpec((1, HEAD_DV), lambda b, h: (0, 0))],
        out_specs=pl.BlockSpec((seq, w), lambda b, h: (b, h)),
        out_shape=jax.ShapeDtypeStruct((batch * seq, n_heads * HEAD_DV), BF16),
        scratch_shapes=[pltpu.VMEM((hps, seq, HEAD_DV + LANES), BF16)],
        compiler_params=_params(2),
        name="diff_attention",
    )(qk, qk, v, tables, lam_vecs, subln_g)


def _out_kernel(a_ref, c_ref, w_ref, x_ref, g_ref, xo_ref, h_ref, wbf_ref, *, n_pro):
    s = pl.program_id(0)
    ck = w_ref.shape[0]

    @pl.when(s < n_pro)
    def _():
        wbf_ref[pl.ds(pl.multiple_of(s * ck, ck), ck), :] = w_ref[...].astype(BF16)

    @pl.when(s >= n_pro)
    def _():
        aw = a_ref.shape[1]
        acc = jnp.dot(a_ref[...], wbf_ref[0:aw, :], preferred_element_type=F32)
        acc = acc + jnp.dot(c_ref[...], wbf_ref[aw:, :], preferred_element_type=F32)
        xn = x_ref[...] + acc
        xo_ref[...] = xn
        ms = jnp.mean(xn * xn, axis=-1, keepdims=True)
        h_ref[...] = (xn * lax.rsqrt(ms + EPS) * g_ref[...]).astype(h_ref.dtype)


def _out_proj(attn, conv, w_out, layer, x, g):
    m, d = x.shape
    aw, cwid = attn.shape[1], conv.shape[1]
    bm = _tile(m, TILE["out_rows"])
    ck = _tile(aw + cwid, TILE["out_cast_rows"])
    n_pro = (aw + cwid) // ck

    def row(s):
        return jnp.maximum(s - n_pro, 0)

    return pl.pallas_call(
        functools.partial(_out_kernel, n_pro=n_pro),
        grid=(n_pro + m // bm,),
        in_specs=[pl.BlockSpec((bm, aw), lambda s: (row(s), 0)),
                  pl.BlockSpec((bm, cwid), lambda s: (row(s), 0)),
                  pl.BlockSpec((None, ck, d), lambda s: (layer, jnp.minimum(s, n_pro - 1), 0)),
                  pl.BlockSpec((bm, d), lambda s: (row(s), 0)),
                  pl.BlockSpec((1, d), lambda s: (0, 0))],
        out_specs=[pl.BlockSpec((bm, d), lambda s: (row(s), 0)),
                   pl.BlockSpec((bm, d), lambda s: (row(s), 0))],
        out_shape=[jax.ShapeDtypeStruct((m, d), F32), jax.ShapeDtypeStruct((m, d), BF16)],
        scratch_shapes=[pltpu.VMEM((aw + cwid, d), BF16)],
        compiler_params=_params(1),
        name="out_proj",
    )(attn, conv, w_out, x, g.reshape(1, d))


def _up_kernel(h_ref, w_ref, o_ref, wbf_ref):
    @pl.when(pl.program_id(1) == 0)
    def _():
        _cast_weight(w_ref, wbf_ref)

    acc = jnp.dot(h_ref[...], wbf_ref[...], preferred_element_type=F32)
    o_ref[...] = jnp.square(jnp.maximum(acc, 0.0)).astype(o_ref.dtype)


def _up_proj(h, w_up, layer):
    m, d = h.shape
    f = w_up.shape[2]
    bm = _tile(m, TILE["up_rows"])
    bn = _tile(f, TILE["up_cols"])
    return pl.pallas_call(
        _up_kernel,
        grid=(f // bn, m // bm),
        in_specs=[pl.BlockSpec((bm, d), lambda j, i: (i, 0)),
                  pl.BlockSpec((None, d, bn), lambda j, i: (layer, 0, j))],
        out_specs=pl.BlockSpec((bm, bn), lambda j, i: (i, j)),
        out_shape=jax.ShapeDtypeStruct((m, f), BF16),
        scratch_shapes=[pltpu.VMEM((d, bn), BF16)],
        compiler_params=_params(2),
        name="up_proj",
    )(h, w_up)


def _down_kernel(a_ref, w_ref, x_ref, *rest, with_norm):
    if with_norm:
        g_ref, xo_ref, h_ref = rest
    else:
        (xo_ref,) = rest
    k = pl.program_id(1)
    last = pl.num_programs(1) - 1

    @pl.when(k == 0)
    def _():
        xo_ref[...] = x_ref[...]

    if not with_norm:
        xo_ref[...] += jnp.dot(a_ref[...], w_ref[...], preferred_element_type=F32)
        return

    @pl.when(k < last)
    def _():
        xo_ref[...] += jnp.dot(a_ref[...], w_ref[...], preferred_element_type=F32)

    @pl.when(k == last)
    def _():
        xn = xo_ref[...] + jnp.dot(a_ref[...], w_ref[...], preferred_element_type=F32)
        xo_ref[...] = xn
        ms = jnp.mean(xn * xn, axis=-1, keepdims=True)
        h_ref[...] = (xn * lax.rsqrt(ms + EPS) * g_ref[...]).astype(h_ref.dtype)


def _down_proj(a, w_down_bf, x, g_next):
    m, d = x.shape
    f = a.shape[1]
    bm = _tile(m, TILE["down_rows"])
    bk = _tile(f, TILE["down_k"])
    with_norm = g_next is not None
    row = pl.BlockSpec((bm, d), lambda i, k: (i, 0))
    in_specs = [pl.BlockSpec((bm, bk), lambda i, k: (i, k)),
                pl.BlockSpec((bk, d), lambda i, k: (k, 0)),
                row]
    args = [a, w_down_bf, x]
    out_specs = [row]
    out_shape = [jax.ShapeDtypeStruct((m, d), F32)]
    if with_norm:
        in_specs.append(pl.BlockSpec((1, d), lambda i, k: (0, 0)))
        args.append(g_next.reshape(1, d))
        out_specs.append(row)
        out_shape.append(jax.ShapeDtypeStruct((m, d), BF16))
    outs = pl.pallas_call(
        functools.partial(_down_kernel, with_norm=with_norm),
        grid=(m // bm, f // bk),
        in_specs=in_specs,
        out_specs=out_specs,
        out_shape=out_shape,
        compiler_params=_params(2),
        name="down_proj",
    )(*args)
    return (outs[0], outs[1]) if with_norm else (outs[0], None)


def kernel(x, w_in, w_out, conv_w, q_norm_g, k_norm_g, lambda_q1, lambda_k1, lambda_q2, lambda_k2,
           subln_g, attn_norm_g, mlp_norm_g, w_up, w_down, rel_bias):
    batch, seq, d = x.shape
    depth = w_in.shape[0]
    aw = d // 2
    cwid = d - aw
    n_heads = aw // HEAD_DV
    assert w_in.shape[2] == 3 * aw + 3 * cwid and rel_bias.shape == (NUM_BUCKETS, 2 * n_heads)
    t = _tile(seq, TILE["attn"])

    tables = _bias_tables(rel_bias, n_heads, t)
    scale = HEAD_DK ** -0.5 * LOG2E
    xf = x.reshape(batch * seq, d)
    h = _rmsnorm(xf, attn_norm_g[0])
    for l in range(depth):
        lam_init = 0.8 - 0.6 * math.exp(-0.3 * l)
        qk_gains = jnp.stack([jnp.tile(q_norm_g[l] * scale, 2), jnp.tile(k_norm_g[l], 2)])
        lam_vecs = jnp.stack([lambda_q1[l], lambda_k1[l], lambda_q2[l], lambda_k2[l],
                              jnp.full((HEAD_DK,), lam_init, F32),
                              jnp.full((HEAD_DK,), 1.0 - lam_init, F32)])

        qk, w_down_bf = _qk_proj(h, w_in, l, qk_gains.reshape(2, 1, LANES), aw, w_down)
        conv, v = _conv_v_proj(h, w_in, conv_w, l, aw, cwid, seq)
        attn = _attention(qk, v, tables, lam_vecs, subln_g[l].reshape(1, HEAD_DV),
                          batch, seq, n_heads, t)
        xf, hm = _out_proj(attn, conv, w_out, l, xf, mlp_norm_g[l])
        act = _up_proj(hm, w_up, l)
        xf, h = _down_proj(act, w_down_bf, xf, attn_norm_g[l + 1] if l + 1 < depth else None)
    return xf.reshape(batch, seq, d)
```

```python
import functools
import math

import numpy as np
import jax
import jax.numpy as jnp
from jax import lax
from jax.experimental import pallas as pl
from jax.experimental.pallas import tpu as pltpu

HEAD_DV = 128
HEAD_DK = HEAD_DV // 2
CONV_K = 3
NUM_BUCKETS = 32
MAX_DISTANCE = 128
MAX_EXACT = NUM_BUCKETS // 2
EPS = 1e-6
NEG = -1e30
LOG2E = math.log2(math.e)

LANES = 128
SUBLANES = 8
MXU_DIM = 256
VMEM_LIMIT_BYTES = 60 * 1024 * 1024

TILE = dict(
    norm_rows=512,
    cast_rows=256,
    qk_rows=1024,
    conv_rows=1024, conv_cols=512,
    attn=128,
    out_rows=512, out_cast_rows=512,
    up_rows=2048, up_cols=1024,
    down_rows=1024, down_k=1024,
)

F32 = jnp.float32
BF16 = jnp.bfloat16


def _tile(dim, pref):
    if dim <= pref:
        return dim
    t = (pref // LANES) * LANES
    while dim % t:
        t -= LANES
    return t


def _params(n_axes):
    return pltpu.CompilerParams(
        dimension_semantics=("arbitrary",) * n_axes,
        vmem_limit_bytes=VMEM_LIMIT_BYTES,
    )


def _rmsnorm_kernel(x_ref, g_ref, o_ref):
    x = x_ref[...]
    ms = jnp.mean(x * x, axis=-1, keepdims=True)
    o_ref[...] = (x * lax.rsqrt(ms + EPS) * g_ref[...]).astype(o_ref.dtype)


def _rmsnorm(x, g):
    m, d = x.shape
    tm = _tile(m, TILE["norm_rows"])
    return pl.pallas_call(
        _rmsnorm_kernel,
        grid=(m // tm,),
        in_specs=[pl.BlockSpec((tm, d), lambda i: (i, 0)),
                  pl.BlockSpec((1, d), lambda i: (0, 0))],
        out_specs=pl.BlockSpec((tm, d), lambda i: (i, 0)),
        out_shape=jax.ShapeDtypeStruct((m, d), BF16),
        compiler_params=_params(1),
        name="rmsnorm",
    )(x, g.reshape(1, d))


def _cast_weight(w_ref, wbf_ref):
    k = w_ref.shape[0]
    ck = _tile(k, TILE["cast_rows"])
    for r in range(0, k, ck):
        wbf_ref[r:r + ck, :] = w_ref[r:r + ck, :].astype(BF16)


def _segment_ones(width):
    seg = np.arange(width) // HEAD_DK
    return jnp.asarray(seg[:, None] == seg[None, :], dtype=BF16)


def _qk_kernel(h_ref, w_ref, g_ref, e_ref, wd_ref, o_ref, wdo_ref, wbf_ref):
    @pl.when(pl.program_id(1) == 0)
    def _():
        _cast_weight(w_ref, wbf_ref)

    wdo_ref[...] = wd_ref[...].astype(BF16)
    acc = jnp.dot(h_ref[...], wbf_ref[...], preferred_element_type=F32)
    g = g_ref[...]
    gw = e_ref.shape[0]
    for c in range(0, acc.shape[1], gw):
        a = acc[:, c:c + gw]
        ssq = jnp.dot((a * a).astype(BF16), e_ref[...], preferred_element_type=F32)
        inv = lax.rsqrt(ssq * (1.0 / HEAD_DK) + EPS)
        for cc in range(0, gw, LANES):
            o_ref[:, c + cc:c + cc + LANES] = (
                a[:, cc:cc + LANES] * inv[:, cc:cc + LANES] * g).astype(o_ref.dtype)


def _qk_proj(h, w_in, layer, gains, aw, w_down):
    m, d = h.shape
    bm = _tile(m, TILE["qk_rows"])
    nm = m // bm
    gw = _tile(aw, MXU_DIM)
    f = w_down.shape[1]
    slab = f // (2 * nm)
    assert f % (2 * nm) == 0 and slab % (2 * SUBLANES) == 0
    return pl.pallas_call(
        _qk_kernel,
        grid=(2, nm),
        in_specs=[pl.BlockSpec((bm, d), lambda j, i: (i, 0)),
                  pl.BlockSpec((None, d, aw), lambda j, i: (layer, 0, j)),
                  pl.BlockSpec((None, 1, LANES), lambda j, i: (j, 0, 0)),
                  pl.BlockSpec((gw, gw), lambda j, i: (0, 0)),
                  pl.BlockSpec((None, slab, d), lambda j, i: (layer, j * nm + i, 0))],
        out_specs=[pl.BlockSpec((bm, aw), lambda j, i: (i, j)),
                   pl.BlockSpec((slab, d), lambda j, i: (j * nm + i, 0))],
        out_shape=[jax.ShapeDtypeStruct((m, 2 * aw), BF16), jax.ShapeDtypeStruct((f, d), BF16)],
        scratch_shapes=[pltpu.VMEM((d, aw), BF16)],
        compiler_params=_params(2),
        name="qk_proj",
    )(h, w_in, gains, _segment_ones(gw), w_down)


def _conv_kernel(h_ref, wb_ref, wc_ref, wi_ref, wv_ref, cw_ref, o_ref, v_ref, wbf_ref, ubuf_ref,
                 *, tiles_per_seq):
    i = pl.program_id(1)
    bm = h_ref.shape[0]

    @pl.when(i == 0)
    def _():
        _cast_weight(wb_ref, wbf_ref.at[0])
        _cast_weight(wc_ref, wbf_ref.at[1])
        _cast_weight(wi_ref, wbf_ref.at[2])
        _cast_weight(wv_ref, wbf_ref.at[3])

    @pl.when(i % tiles_per_seq == 0)
    def _():
        ubuf_ref[0:SUBLANES, :] = jnp.zeros((SUBLANES, ubuf_ref.shape[1]), F32)

    h = h_ref[...]
    gate_c = jnp.dot(h, wbf_ref[1], preferred_element_type=F32)
    conv_in = jnp.dot(h, wbf_ref[2], preferred_element_type=F32)
    ubuf_ref[SUBLANES:SUBLANES + bm, :] = gate_c * conv_in
    gate_b = jnp.dot(h, wbf_ref[0], preferred_element_type=F32)
    cw = cw_ref[...]
    conv = (cw[0:1] * ubuf_ref[SUBLANES - 2:SUBLANES - 2 + bm, :]
            + cw[1:2] * ubuf_ref[SUBLANES - 1:SUBLANES - 1 + bm, :]
            + cw[2:3] * ubuf_ref[SUBLANES:SUBLANES + bm, :])
    o_ref[...] = (gate_b * conv).astype(o_ref.dtype)
    ubuf_ref[0:SUBLANES, :] = ubuf_ref[bm:bm + SUBLANES, :]
    v_ref[...] = jnp.dot(h, wbf_ref[3], preferred_element_type=F32).astype(v_ref.dtype)


def _conv_v_proj(h, w_in, conv_w, layer, aw, cwid, seq):
    m, d = h.shape
    assert aw == cwid
    bm = _tile(seq, TILE["conv_rows"])
    bn = _tile(cwid, TILE["conv_cols"])
    off = 3 * aw // bn
    nb = cwid // bn
    return pl.pallas_call(
        functools.partial(_conv_kernel, tiles_per_seq=seq // bm),
        grid=(nb, m // bm),
        in_specs=[pl.BlockSpec((bm, d), lambda j, i: (i, 0)),
                  pl.BlockSpec((None, d, bn), lambda j, i: (layer, 0, off + j)),
                  pl.BlockSpec((None, d, bn), lambda j, i: (layer, 0, off + nb + j)),
                  pl.BlockSpec((None, d, bn), lambda j, i: (layer, 0, off + 2 * nb + j)),
                  pl.BlockSpec((None, d, bn), lambda j, i: (layer, 0, off - nb + j)),
                  pl.BlockSpec((None, CONV_K, bn), lambda j, i: (layer, 0, j))],
        out_specs=[pl.BlockSpec((bm, bn), lambda j, i: (i, j)),
                   pl.BlockSpec((bm, bn), lambda j, i: (i, j))],
        out_shape=[jax.ShapeDtypeStruct((m, cwid), BF16), jax.ShapeDtypeStruct((m, aw), BF16)],
        scratch_shapes=[pltpu.VMEM((4, d, bn), BF16),
                        pltpu.VMEM((bm + SUBLANES, bn), F32)],
        compiler_params=_params(2),
        name="conv_v_proj",
    )(h, w_in, w_in, w_in, w_in, conv_w)


def _bucket(n):
    nf = np.maximum(n, MAX_EXACT).astype(np.float64)
    val = np.log(nf / MAX_EXACT) / math.log(MAX_DISTANCE / MAX_EXACT) * (NUM_BUCKETS - MAX_EXACT)
    large = np.minimum(MAX_EXACT + val.astype(np.int64), NUM_BUCKETS - 1)
    return np.where(n < MAX_EXACT, n, large)


def _bucket_blocks():
    n = LANES
    assert _bucket(np.array([n + 1]))[0] == NUM_BUCKETS - 1
    r = np.arange(n)[:, None]
    c = np.arange(n)[None, :]
    d0 = r - c
    diag = np.where(d0 >= 0, _bucket(np.maximum(d0, 0)), -1)
    sub = _bucket(n + r - c)
    return np.stack([diag, sub]).astype(np.int32)


def _bias_table_kernel(rb_ref, bk_ref, o_ref):
    h = pl.program_id(0)
    n = LANES
    t = o_ref.shape[2]
    zero = jnp.zeros((n, n), F32)
    masked = jnp.full((n, n), NEG, F32)
    for mp in range(2):
        col = 2 * h + mp
        far = rb_ref[NUM_BUCKETS - 1, col]
        blocks = []
        for kind in range(2):
            bk = bk_ref[kind]
            acc = jnp.where(bk < 0, NEG, 0.0).astype(F32)
            for b in range(NUM_BUCKETS - 1):
                acc = jnp.where(bk == b, (rb_ref[b, col] - far) * LOG2E, acc)
            blocks.append(acc)
        diag, sub = blocks
        for r in range(t // n):
            d = t // n + r
            for c in range(2 * t // n):
                blk = diag if c == d else sub if c == d - 1 else masked if c > d else zero
                o_ref[0, mp, r * n:(r + 1) * n, c * n:(c + 1) * n] = blk


def _bias_tables(rel_bias, n_heads, t):
    assert t % LANES == 0
    buckets = jnp.asarray(_bucket_blocks())
    return pl.pallas_call(
        _bias_table_kernel,
        grid=(n_heads,),
        in_specs=[pl.BlockSpec(memory_space=pltpu.SMEM),
                  pl.BlockSpec((2, LANES, LANES), lambda h: (0, 0, 0))],
        out_specs=pl.BlockSpec((1, 2, t, 2 * t), lambda h: (h, 0, 0, 0)),
        out_shape=jax.ShapeDtypeStruct((n_heads, 2, t, 2 * t), F32),
        compiler_params=_params(1),
        name="bias_tables",
    )(rel_bias, buckets)


def _fold_lanes(x, op):
    return functools.reduce(op, [x[:, c:c + LANES] for c in range(0, x.shape[1], LANES)])


def _attn_kernel(q_ref, k_ref, v_ref, tb_ref, lam_ref, g_ref, o_ref, v1_ref, *, t):
    seq = q_ref.shape[0]
    heads = q_ref.shape[1] // HEAD_DV
    for hh in range(heads):
        v1_ref[hh, :, 0:HEAD_DV] = v_ref[:, hh * HEAD_DV:(hh + 1) * HEAD_DV]
        v1_ref[hh, :, HEAD_DV:] = jnp.ones((seq, LANES), BF16)
    lo = lax.broadcasted_iota(jnp.int32, (1, LANES), 1) < HEAD_DK
    lv = lam_ref[...]
    lam = (jnp.exp(jnp.sum(lv[0:1] * lv[1:2], axis=-1, keepdims=True))
           - jnp.exp(jnp.sum(lv[2:3] * lv[3:4], axis=-1, keepdims=True)) + lv[4:5, 0:1])
    out_gain = lv[5:6, 0:1]
    g = g_ref[...]
    contract_last = (((1,), (1,)), ((), ()))

    def tile(hh, i):
        cols = slice(hh * HEAD_DV, (hh + 1) * HEAD_DV)
        q = q_ref[i * t:(i + 1) * t, cols]
        zero = jnp.zeros_like(q)
        q2 = jnp.concatenate([jnp.where(lo, q, zero), jnp.where(lo, zero, q)], axis=0)

        pieces = []
        if i >= 2:
            pieces.append((0, (i - 1) * t, False))
        near = min(i, 1) * t
        pieces.append((i * t - near, near + t, True))

        scores = []
        for start, width, biased in pieces:
            s = lax.dot_general(q2, k_ref[start:start + width, cols], contract_last,
                                preferred_element_type=F32)
            if biased:
                s = s + tb_ref[hh, :, :, 2 * t - width:].reshape(2 * t, width)
            scores.append(s)
        m = functools.reduce(jnp.maximum, [_fold_lanes(s, jnp.maximum) for s in scores])
        m = jnp.max(m, axis=-1, keepdims=True)
        acc = None
        for s, (start, width, _) in zip(scores, pieces):
            p = jnp.exp2(s - m).astype(BF16)
            pv = jnp.dot(p, v1_ref[hh, start:start + width, :], preferred_element_type=F32)
            acc = pv if acc is None else acc + pv
        out = acc[:, 0:HEAD_DV] / acc[:, HEAD_DV:]
        o = out[0:t] - lam * out[t:2 * t]
        ms = jnp.mean(o * o, axis=-1, keepdims=True)
        o_ref[i * t:(i + 1) * t, cols] = (o * lax.rsqrt(ms + EPS) * g * out_gain).astype(o_ref.dtype)

    nq = seq // t
    for i in [0] + list(range(nq - 1, 0, -1)):
        for hh in range(heads):
            tile(hh, i)


def _attention(qk, v, tables, lam_vecs, subln_g, batch, seq, n_heads, t):
    hps = 2 if n_heads % 2 == 0 else 1
    w = hps * HEAD_DV
    ng = n_heads // hps
    return pl.pallas_call(
        functools.partial(_attn_kernel, t=t),
        grid=(batch, ng),
        in_specs=[pl.BlockSpec((seq, w), lambda b, h: (b, h)),
                  pl.BlockSpec((seq, w), lambda b, h: (b, ng + h)),
                  pl.BlockSpec((seq, w), lambda b, h: (b, h)),
                  pl.BlockSpec((hps, 2, t, 2 * t), lambda b, h: (h, 0, 0, 0)),
                  pl.BlockSpec((6, HEAD_DK), lambda b, h: (0, 0)),
                  pl.BlockSpec((1, HEAD_DV), lambda b, h: (0, 0))],
        out_specs=pl.BlockSpec((seq, w), lambda b, h: (b, h)),
        out_shape=jax.ShapeDtypeStruct((batch * seq, n_heads * HEAD_DV), BF16),
        scratch_shapes=[pltpu.VMEM((hps, seq, HEAD_DV + LANES), BF16)],
        compiler_params=_params(2),
        name="diff_attention",
    )(qk, qk, v, tables, lam_vecs, subln_g)


def _out_kernel(a_ref, c_ref, w_ref, x_ref, g_ref, xo_ref, h_ref, wbf_ref, *, n_pro):
    s = pl.program_id(0)
    ck = w_ref.shape[0]

    @pl.when(s < n_pro)
    def _():
        wbf_ref[pl.ds(pl.multiple_of(s * ck, ck), ck), :] = w_ref[...].astype(BF16)

    @pl.when(s >= n_pro)
    def _():
        aw = a_ref.shape[1]
        acc = jnp.dot(a_ref[...], wbf_ref[0:aw, :], preferred_element_type=F32)
        acc = acc + jnp.dot(c_ref[...], wbf_ref[aw:, :], preferred_element_type=F32)
        xn = x_ref[...] + acc
        xo_ref[...] = xn
        ms = jnp.mean(xn * xn, axis=-1, keepdims=True)
        h_ref[...] = (xn * lax.rsqrt(ms + EPS) * g_ref[...]).astype(h_ref.dtype)


def _out_proj(attn, conv, w_out, layer, x, g):
    m, d = x.shape
    aw, cwid = attn.shape[1], conv.shape[1]
    bm = _tile(m, TILE["out_rows"])
    ck = _tile(aw + cwid, TILE["out_cast_rows"])
    n_pro = (aw + cwid) // ck

    def row(s):
        return jnp.maximum(s - n_pro, 0)

    return pl.pallas_call(
        functools.partial(_out_kernel, n_pro=n_pro),
        grid=(n_pro + m // bm,),
        in_specs=[pl.BlockSpec((bm, aw), lambda s: (row(s), 0)),
                  pl.BlockSpec((bm, cwid), lambda s: (row(s), 0)),
                  pl.BlockSpec((None, ck, d), lambda s: (layer, jnp.minimum(s, n_pro - 1), 0)),
                  pl.BlockSpec((bm, d), lambda s: (row(s), 0)),
                  pl.BlockSpec((1, d), lambda s: (0, 0))],
        out_specs=[pl.BlockSpec((bm, d), lambda s: (row(s), 0)),
                   pl.BlockSpec((bm, d), lambda s: (row(s), 0))],
        out_shape=[jax.ShapeDtypeStruct((m, d), F32), jax.ShapeDtypeStruct((m, d), BF16)],
        scratch_shapes=[pltpu.VMEM((aw + cwid, d), BF16)],
        compiler_params=_params(1),
        name="out_proj",
    )(attn, conv, w_out, x, g.reshape(1, d))


def _up_kernel(h_ref, w_ref, o_ref, wbf_ref):
    @pl.when(pl.program_id(1) == 0)
    def _():
        _cast_weight(w_ref, wbf_ref)

    acc = jnp.dot(h_ref[...], wbf_ref[...], preferred_element_type=F32)
    o_ref[...] = jnp.square(jnp.maximum(acc, 0.0)).astype(o_ref.dtype)


def _up_proj(h, w_up, layer):
    m, d = h.shape
    f = w_up.shape[2]
    bm = _tile(m, TILE["up_rows"])
    bn = _tile(f, TILE["up_cols"])
    return pl.pallas_call(
        _up_kernel,
        grid=(f // bn, m // bm),
        in_specs=[pl.BlockSpec((bm, d), lambda j, i: (i, 0)),
                  pl.BlockSpec((None, d, bn), lambda j, i: (layer, 0, j))],
        out_specs=pl.BlockSpec((bm, bn), lambda j, i: (i, j)),
        out_shape=jax.ShapeDtypeStruct((m, f), BF16),
        scratch_shapes=[pltpu.VMEM((d, bn), BF16)],
        compiler_params=_params(2),
        name="up_proj",
    )(h, w_up)


def _down_kernel(a_ref, w_ref, x_ref, *rest, with_norm):
    if with_norm:
        g_ref, xo_ref, h_ref = rest
    else:
        (xo_ref,) = rest
    k = pl.program_id(1)
    last = pl.num_programs(1) - 1

    @pl.when(k == 0)
    def _():
        xo_ref[...] = x_ref[...]

    if not with_norm:
        xo_ref[...] += jnp.dot(a_ref[...], w_ref[...], preferred_element_type=F32)
        return

    @pl.when(k < last)
    def _():
        xo_ref[...] += jnp.dot(a_ref[...], w_ref[...], preferred_element_type=F32)

    @pl.when(k == last)
    def _():
        xn = xo_ref[...] + jnp.dot(a_ref[...], w_ref[...], preferred_element_type=F32)
        xo_ref[...] = xn
        ms = jnp.mean(xn * xn, axis=-1, keepdims=True)
        h_ref[...] = (xn * lax.rsqrt(ms + EPS) * g_ref[...]).astype(h_ref.dtype)


def _down_proj(a, w_down_bf, x, g_next):
    m, d = x.shape
    f = a.shape[1]
    bm = _tile(m, TILE["down_rows"])
    bk = _tile(f, TILE["down_k"])
    with_norm = g_next is not None
    row = pl.BlockSpec((bm, d), lambda i, k: (i, 0))
    in_specs = [pl.BlockSpec((bm, bk), lambda i, k: (i, k)),
                pl.BlockSpec((bk, d), lambda i, k: (k, 0)),
                row]
    args = [a, w_down_bf, x]
    out_specs = [row]
    out_shape = [jax.ShapeDtypeStruct((m, d), F32)]
    if with_norm:
        in_specs.append(pl.BlockSpec((1, d), lambda i, k: (0, 0)))
        args.append(g_next.reshape(1, d))
        out_specs.append(row)
        out_shape.append(jax.ShapeDtypeStruct((m, d), BF16))
    outs = pl.pallas_call(
        functools.partial(_down_kernel, with_norm=with_norm),
        grid=(m // bm, f // bk),
        in_specs=in_specs,
        out_specs=out_specs,
        out_shape=out_shape,
        compiler_params=_params(2),
        name="down_proj",
    )(*args)
    return (outs[0], outs[1]) if with_norm else (outs[0], None)


def kernel(x, w_in, w_out, conv_w, q_norm_g, k_norm_g, lambda_q1, lambda_k1, lambda_q2, lambda_k2,
           subln_g, attn_norm_g, mlp_norm_g, w_up, w_down, rel_bias):
    batch, seq, d = x.shape
    depth = w_in.shape[0]
    aw = d // 2
    cwid = d - aw
    n_heads = aw // HEAD_DV
    assert w_in.shape[2] == 3 * aw + 3 * cwid and rel_bias.shape == (NUM_BUCKETS, 2 * n_heads)
    t = _tile(seq, TILE["attn"])

    tables = _bias_tables(rel_bias, n_heads, t)
    scale = HEAD_DK ** -0.5 * LOG2E
    xf = x.reshape(batch * seq, d)
    h = _rmsnorm(xf, attn_norm_g[0])
    for l in range(depth):
        lam_init = 0.8 - 0.6 * math.exp(-0.3 * l)
        qk_gains = jnp.stack([jnp.tile(q_norm_g[l] * scale, 2), jnp.tile(k_norm_g[l], 2)])
        lam_vecs = jnp.stack([lambda_q1[l], lambda_k1[l], lambda_q2[l], lambda_k2[l],
                              jnp.full((HEAD_DK,), lam_init, F32),
                              jnp.full((HEAD_DK,), 1.0 - lam_init, F32)])

        qk, w_down_bf = _qk_proj(h, w_in, l, qk_gains.reshape(2, 1, LANES), aw, w_down)
        conv, v = _conv_v_proj(h, w_in, conv_w, l, aw, cwid, seq)
        attn = _attention(qk, v, tables, lam_vecs, subln_g[l].reshape(1, HEAD_DV),
                          batch, seq, n_heads, t)
        xf, hm = _out_proj(attn, conv, w_out, l, xf, mlp_norm_g[l])
        act = _up_proj(hm, w_up, l)
        xf, h = _down_proj(act, w_down_bf, xf, attn_norm_g[l + 1] if l + 1 < depth else None)
    return xf.reshape(batch, seq, d)
```

```python
import functools
import math

import numpy as np
import jax
import jax.numpy as jnp
from jax import lax
from jax.experimental import pallas as pl
from jax.experimental.pallas import tpu as pltpu

HEAD_DV = 128
HEAD_DK = HEAD_DV // 2
CONV_K = 3
NUM_BUCKETS = 32
MAX_DISTANCE = 128
MAX_EXACT = NUM_BUCKETS // 2
EPS = 1e-6
NEG = -1e30
LOG2E = math.log2(math.e)

LANES = 128
SUBLANES = 8
MXU_DIM = 256
VMEM_LIMIT_BYTES = 60 * 1024 * 1024

TILE = dict(
    norm_rows=512,
    cast_rows=256,
    qk_rows=1024,
    conv_rows=1024, conv_cols=512,
    attn=128,
    out_rows=512, out_cast_rows=512,
    up_rows=2048, up_cols=1024,
    down_rows=1024, down_k=1024,
)

F32 = jnp.float32
BF16 = jnp.bfloat16


def _tile(dim, pref):
    if dim <= pref:
        return dim
    t = (pref // LANES) * LANES
    while dim % t:
        t -= LANES
    return t


def _params(n_axes):
    return pltpu.CompilerParams(
        dimension_semantics=("arbitrary",) * n_axes,
        vmem_limit_bytes=VMEM_LIMIT_BYTES,
    )


def _rmsnorm_kernel(x_ref, g_ref, o_ref):
    x = x_ref[...]
    ms = jnp.mean(x * x, axis=-1, keepdims=True)
    o_ref[...] = (x * lax.rsqrt(ms + EPS) * g_ref[...]).astype(o_ref.dtype)


def _rmsnorm(x, g):
    m, d = x.shape
    tm = _tile(m, TILE["norm_rows"])
    return pl.pallas_call(
        _rmsnorm_kernel,
        grid=(m // tm,),
        in_specs=[pl.BlockSpec((tm, d), lambda i: (i, 0)),
                  pl.BlockSpec((1, d), lambda i: (0, 0))],
        out_specs=pl.BlockSpec((tm, d), lambda i: (i, 0)),
        out_shape=jax.ShapeDtypeStruct((m, d), BF16),
        compiler_params=_params(1),
        name="rmsnorm",
    )(x, g.reshape(1, d))


def _cast_weight(w_ref, wbf_ref):
    k = w_ref.shape[0]
    ck = _tile(k, TILE["cast_rows"])
    for r in range(0, k, ck):
        wbf_ref[r:r + ck, :] = w_ref[r:r + ck, :].astype(BF16)


def _segment_ones(width):
    seg = np.arange(width) // HEAD_DK
    return jnp.asarray(seg[:, None] == seg[None, :], dtype=BF16)


def _qk_kernel(h_ref, w_ref, g_ref, e_ref, wd_ref, o_ref, wdo_ref, wbf_ref):
    @pl.when(pl.program_id(1) == 0)
    def _():
        _cast_weight(w_ref, wbf_ref)

    wdo_ref[...] = wd_ref[...].astype(BF16)
    acc = jnp.dot(h_ref[...], wbf_ref[...], preferred_element_type=F32)
    g = g_ref[...]
    gw = e_ref.shape[0]
    for c in range(0, acc.shape[1], gw):
        a = acc[:, c:c + gw]
        ssq = jnp.dot((a * a).astype(BF16), e_ref[...], preferred_element_type=F32)
        inv = lax.rsqrt(ssq * (1.0 / HEAD_DK) + EPS)
        for cc in range(0, gw, LANES):
            o_ref[:, c + cc:c + cc + LANES] = (
                a[:, cc:cc + LANES] * inv[:, cc:cc + LANES] * g).astype(o_ref.dtype)


def _qk_proj(h, w_in, layer, gains, aw, w_down):
    m, d = h.shape
    bm = _tile(m, TILE["qk_rows"])
    nm = m // bm
    gw = _tile(aw, MXU_DIM)
    f = w_down.shape[1]
    slab = f // (2 * nm)
    assert f % (2 * nm) == 0 and slab % (2 * SUBLANES) == 0
    return pl.pallas_call(
        _qk_kernel,
        grid=(2, nm),
        in_specs=[pl.BlockSpec((bm, d), lambda j, i: (i, 0)),
                  pl.BlockSpec((None, d, aw), lambda j, i: (layer, 0, j)),
                  pl.BlockSpec((None, 1, LANES), lambda j, i: (j, 0, 0)),
                  pl.BlockSpec((gw, gw), lambda j, i: (0, 0)),
                  pl.BlockSpec((None, slab, d), lambda j, i: (layer, j * nm + i, 0))],
        out_specs=[pl.BlockSpec((bm, aw), lambda j, i: (i, j)),
                   pl.BlockSpec((slab, d), lambda j, i: (j * nm + i, 0))],
        out_shape=[jax.ShapeDtypeStruct((m, 2 * aw), BF16), jax.ShapeDtypeStruct((f, d), BF16)],
        scratch_shapes=[pltpu.VMEM((d, aw), BF16)],
        compiler_params=_params(2),
        name="qk_proj",
    )(h, w_in, gains, _segment_ones(gw), w_down)


def _conv_kernel(h_ref, wb_ref, wc_ref, wi_ref, wv_ref, cw_ref, o_ref, v_ref, wbf_ref, ubuf_ref,
                 *, tiles_per_seq):
    i = pl.program_id(1)
    bm = h_ref.shape[0]

    @pl.when(i == 0)
    def _():
        _cast_weight(wb_ref, wbf_ref.at[0])
        _cast_weight(wc_ref, wbf_ref.at[1])
        _cast_weight(wi_ref, wbf_ref.at[2])
        _cast_weight(wv_ref, wbf_ref.at[3])

    @pl.when(i % tiles_per_seq == 0)
    def _():
        ubuf_ref[0:SUBLANES, :] = jnp.zeros((SUBLANES, ubuf_ref.shape[1]), F32)

    h = h_ref[...]
    gate_c = jnp.dot(h, wbf_ref[1], preferred_element_type=F32)
    conv_in = jnp.dot(h, wbf_ref[2], preferred_element_type=F32)
    ubuf_ref[SUBLANES:SUBLANES + bm, :] = gate_c * conv_in
    gate_b = jnp.dot(h, wbf_ref[0], preferred_element_type=F32)
    cw = cw_ref[...]
    conv = (cw[0:1] * ubuf_ref[SUBLANES - 2:SUBLANES - 2 + bm, :]
            + cw[1:2] * ubuf_ref[SUBLANES - 1:SUBLANES - 1 + bm, :]
            + cw[2:3] * ubuf_ref[SUBLANES:SUBLANES + bm, :])
    o_ref[...] = (gate_b * conv).astype(o_ref.dtype)
    ubuf_ref[0:SUBLANES, :] = ubuf_ref[bm:bm + SUBLANES, :]
    v_ref[...] = jnp.dot(h, wbf_ref[3], preferred_element_type=F32).astype(v_ref.dtype)


def _conv_v_proj(h, w_in, conv_w, layer, aw, cwid, seq):
    m, d = h.shape
    assert aw == cwid
    bm = _tile(seq, TILE["conv_rows"])
    bn = _tile(cwid, TILE["conv_cols"])
    off = 3 * aw // bn
    nb = cwid // bn
    return pl.pallas_call(
        functools.partial(_conv_kernel, tiles_per_seq=seq // bm),
        grid=(nb, m // bm),
        in_specs=[pl.BlockSpec((bm, d), lambda j, i: (i, 0)),
                  pl.BlockSpec((None, d, bn), lambda j, i: (layer, 0, off + j)),
                  pl.BlockSpec((None, d, bn), lambda j, i: (layer, 0, off + nb + j)),
                  pl.BlockSpec((None, d, bn), lambda j, i: (layer, 0, off + 2 * nb + j)),
                  pl.BlockSpec((None, d, bn), lambda j, i: (layer, 0, off - nb + j)),
                  pl.BlockSpec((None, CONV_K, bn), lambda j, i: (layer, 0, j))],
        out_specs=[pl.BlockSpec((bm, bn), lambda j, i: (i, j)),
                   pl.BlockSpec((bm, bn), lambda j, i: (i, j))],
        out_shape=[jax.ShapeDtypeStruct((m, cwid), BF16), jax.ShapeDtypeStruct((m, aw), BF16)],
        scratch_shapes=[pltpu.VMEM((4, d, bn), BF16),
                        pltpu.VMEM((bm + SUBLANES, bn), F32)],
        compiler_params=_params(2),
        name="conv_v_proj",
    )(h, w_in, w_in, w_in, w_in, conv_w)


def _bucket(n):
    nf = np.maximum(n, MAX_EXACT).astype(np.float64)
    val = np.log(nf / MAX_EXACT) / math.log(MAX_DISTANCE / MAX_EXACT) * (NUM_BUCKETS - MAX_EXACT)
    large = np.minimum(MAX_EXACT + val.astype(np.int64), NUM_BUCKETS - 1)
    return np.where(n < MAX_EXACT, n, large)


def _bucket_blocks():
    n = LANES
    assert _bucket(np.array([n + 1]))[0] == NUM_BUCKETS - 1
    r = np.arange(n)[:, None]
    c = np.arange(n)[None, :]
    d0 = r - c
    diag = np.where(d0 >= 0, _bucket(np.maximum(d0, 0)), -1)
    sub = _bucket(n + r - c)
    return np.stack([diag, sub]).astype(np.int32)


def _bias_table_kernel(rb_ref, bk_ref, o_ref):
    h = pl.program_id(0)
    n = LANES
    t = o_ref.shape[2]
    zero = jnp.zeros((n, n), F32)
    masked = jnp.full((n, n), NEG, F32)
    for mp in range(2):
        col = 2 * h + mp
        far = rb_ref[NUM_BUCKETS - 1, col]
        blocks = []
        for kind in range(2):
            bk = bk_ref[kind]
            acc = jnp.where(bk < 0, NEG, 0.0).astype(F32)
            for b in range(NUM_BUCKETS - 1):
                acc = jnp.where(bk == b, (rb_ref[b, col] - far) * LOG2E, acc)
            blocks.append(acc)
        diag, sub = blocks
        for r in range(t // n):
            d = t // n + r
            for c in range(2 * t // n):
                blk = diag if c == d else sub if c == d - 1 else masked if c > d else zero
                o_ref[0, mp, r * n:(r + 1) * n, c * n:(c + 1) * n] = blk


def _bias_tables(rel_bias, n_heads, t):
    assert t % LANES == 0
    buckets = jnp.asarray(_bucket_blocks())
    return pl.pallas_call(
        _bias_table_kernel,
        grid=(n_heads,),
        in_specs=[pl.BlockSpec(memory_space=pltpu.SMEM),
                  pl.BlockSpec((2, LANES, LANES), lambda h: (0, 0, 0))],
        out_specs=pl.BlockSpec((1, 2, t, 2 * t), lambda h: (h, 0, 0, 0)),
        out_shape=jax.ShapeDtypeStruct((n_heads, 2, t, 2 * t), F32),
        compiler_params=_params(1),
        name="bias_tables",
    )(rel_bias, buckets)


def _fold_lanes(x, op):
    return functools.reduce(op, [x[:, c:c + LANES] for c in range(0, x.shape[1], LANES)])


def _attn_kernel(q_ref, k_ref, v_ref, tb_ref, lam_ref, g_ref, o_ref, v1_ref, *, t):
    seq = q_ref.shape[0]
    heads = q_ref.shape[1] // HEAD_DV
    for hh in range(heads):
        v1_ref[hh, :, 0:HEAD_DV] = v_ref[:, hh * HEAD_DV:(hh + 1) * HEAD_DV]
        v1_ref[hh, :, HEAD_DV:] = jnp.ones((seq, LANES), BF16)
    lo = lax.broadcasted_iota(jnp.int32, (1, LANES), 1) < HEAD_DK
    lv = lam_ref[...]
    lam = (jnp.exp(jnp.sum(lv[0:1] * lv[1:2], axis=-1, keepdims=True))
           - jnp.exp(jnp.sum(lv[2:3] * lv[3:4], axis=-1, keepdims=True)) + lv[4:5, 0:1])
    out_gain = lv[5:6, 0:1]
    g = g_ref[...]
    contract_last = (((1,), (1,)), ((), ()))

    def tile(hh, i):
        cols = slice(hh * HEAD_DV, (hh + 1) * HEAD_DV)
        q = q_ref[i * t:(i + 1) * t, cols]
        zero = jnp.zeros_like(q)
        q2 = jnp.concatenate([jnp.where(lo, q, zero), jnp.where(lo, zero, q)], axis=0)

        pieces = []
        if i >= 2:
            pieces.append((0, (i - 1) * t, False))
        near = min(i, 1) * t
        pieces.append((i * t - near, near + t, True))

        scores = []
        for start, width, biased in pieces:
            s = lax.dot_general(q2, k_ref[start:start + width, cols], contract_last,
                                preferred_element_type=F32)
            if biased:
                s = s + tb_ref[hh, :, :, 2 * t - width:].reshape(2 * t, width)
            scores.append(s)
        m = functools.reduce(jnp.maximum, [_fold_lanes(s, jnp.maximum) for s in scores])
        m = jnp.max(m, axis=-1, keepdims=True)
        acc = None
        for s, (start, width, _) in zip(scores, pieces):
            p = jnp.exp2(s - m).astype(BF16)
            pv = jnp.dot(p, v1_ref[hh, start:start + width, :], preferred_element_type=F32)
            acc = pv if acc is None else acc + pv
        out = acc[:, 0:HEAD_DV] / acc[:, HEAD_DV:]
        o = out[0:t] - lam * out[t:2 * t]
        ms = jnp.mean(o * o, axis=-1, keepdims=True)
        o_ref[i * t:(i + 1) * t, cols] = (o * lax.rsqrt(ms + EPS) * g * out_gain).astype(o_ref.dtype)

    nq = seq // t
    for i in [0] + list(range(nq - 1, 0, -1)):
        for hh in range(heads):
            tile(hh, i)


def _attention(qk, v, tables, lam_vecs, subln_g, batch, seq, n_heads, t):
    hps = 4 if n_heads % 4 == 0 else 1
    w = hps * HEAD_DV
    ng = n_heads // hps
    return pl.pallas_call(
        functools.partial(_attn_kernel, t=t),
        grid=(batch, ng),
        in_specs=[pl.BlockSpec((seq, w), lambda b, h: (b, h)),
                  pl.BlockSpec((seq, w), lambda b, h: (b, ng + h)),
                  pl.BlockSpec((seq, w), lambda b, h: (b, h)),
                  pl.BlockSpec((hps, 2, t, 2 * t), lambda b, h: (h, 0, 0, 0)),
                  pl.BlockSpec((6, HEAD_DK), lambda b, h: (0, 0)),
                  pl.BlockSpec((1, HEAD_DV), lambda b, h: (0, 0))],
        out_specs=pl.BlockSpec((seq, w), lambda b, h: (b, h)),
        out_shape=jax.ShapeDtypeStruct((batch * seq, n_heads * HEAD_DV), BF16),
        scratch_shapes=[pltpu.VMEM((hps, seq, HEAD_DV + LANES), BF16)],
        compiler_params=_params(2),
        name="diff_attention",
    )(qk, qk, v, tables, lam_vecs, subln_g)


def _out_kernel(a_ref, c_ref, w_ref, x_ref, g_ref, xo_ref, h_ref, wbf_ref, *, n_pro):
    s = pl.program_id(0)
    ck = w_ref.shape[0]

    @pl.when(s < n_pro)
    def _():
        wbf_ref[pl.ds(pl.multiple_of(s * ck, ck), ck), :] = w_ref[...].astype(BF16)

    @pl.when(s >= n_pro)
    def _():
        aw = a_ref.shape[1]
        acc = jnp.dot(a_ref[...], wbf_ref[0:aw, :], preferred_element_type=F32)
        acc = acc + jnp.dot(c_ref[...], wbf_ref[aw:, :], preferred_element_type=F32)
        xn = x_ref[...] + acc
        xo_ref[...] = xn
        ms = jnp.mean(xn * xn, axis=-1, keepdims=True)
        h_ref[...] = (xn * lax.rsqrt(ms + EPS) * g_ref[...]).astype(h_ref.dtype)


def _out_proj(attn, conv, w_out, layer, x, g):
    m, d = x.shape
    aw, cwid = attn.shape[1], conv.shape[1]
    bm = _tile(m, TILE["out_rows"])
    ck = _tile(aw + cwid, TILE["out_cast_rows"])
    n_pro = (aw + cwid) // ck

    def row(s):
        return jnp.maximum(s - n_pro, 0)

    return pl.pallas_call(
        functools.partial(_out_kernel, n_pro=n_pro),
        grid=(n_pro + m // bm,),
        in_specs=[pl.BlockSpec((bm, aw), lambda s: (row(s), 0)),
                  pl.BlockSpec((bm, cwid), lambda s: (row(s), 0)),
                  pl.BlockSpec((None, ck, d), lambda s: (layer, jnp.minimum(s, n_pro - 1), 0)),
                  pl.BlockSpec((bm, d), lambda s: (row(s), 0)),
                  pl.BlockSpec((1, d), lambda s: (0, 0))],
        out_specs=[pl.BlockSpec((bm, d), lambda s: (row(s), 0)),
                   pl.BlockSpec((bm, d), lambda s: (row(s), 0))],
        out_shape=[jax.ShapeDtypeStruct((m, d), F32), jax.ShapeDtypeStruct((m, d), BF16)],
        scratch_shapes=[pltpu.VMEM((aw + cwid, d), BF16)],
        compiler_params=_params(1),
        name="out_proj",
    )(attn, conv, w_out, x, g.reshape(1, d))


def _up_kernel(h_ref, w_ref, o_ref, wbf_ref):
    @pl.when(pl.program_id(1) == 0)
    def _():
        _cast_weight(w_ref, wbf_ref)

    acc = jnp.dot(h_ref[...], wbf_ref[...], preferred_element_type=F32)
    o_ref[...] = jnp.square(jnp.maximum(acc, 0.0)).astype(o_ref.dtype)


def _up_proj(h, w_up, layer):
    m, d = h.shape
    f = w_up.shape[2]
    bm = _tile(m, TILE["up_rows"])
    bn = _tile(f, TILE["up_cols"])
    return pl.pallas_call(
        _up_kernel,
        grid=(f // bn, m // bm),
        in_specs=[pl.BlockSpec((bm, d), lambda j, i: (i, 0)),
                  pl.BlockSpec((None, d, bn), lambda j, i: (layer, 0, j))],
        out_specs=pl.BlockSpec((bm, bn), lambda j, i: (i, j)),
        out_shape=jax.ShapeDtypeStruct((m, f), BF16),
        scratch_shapes=[pltpu.VMEM((d, bn), BF16)],
        compiler_params=_params(2),
        name="up_proj",
    )(h, w_up)


def _down_kernel(a_ref, w_ref, x_ref, *rest, with_norm):
    if with_norm:
        g_ref, xo_ref, h_ref = rest
    else:
        (xo_ref,) = rest
    k = pl.program_id(1)
    last = pl.num_programs(1) - 1

    @pl.when(k == 0)
    def _():
        xo_ref[...] = x_ref[...]

    if not with_norm:
        xo_ref[...] += jnp.dot(a_ref[...], w_ref[...], preferred_element_type=F32)
        return

    @pl.when(k < last)
    def _():
        xo_ref[...] += jnp.dot(a_ref[...], w_ref[...], preferred_element_type=F32)

    @pl.when(k == last)
    def _():
        xn = xo_ref[...] + jnp.dot(a_ref[...], w_ref[...], preferred_element_type=F32)
        xo_ref[...] = xn
        ms = jnp.mean(xn * xn, axis=-1, keepdims=True)
        h_ref[...] = (xn * lax.rsqrt(ms + EPS) * g_ref[...]).astype(h_ref.dtype)


def _down_proj(a, w_down_bf, x, g_next):
    m, d = x.shape
    f = a.shape[1]
    bm = _tile(m, TILE["down_rows"])
    bk = _tile(f, TILE["down_k"])
    with_norm = g_next is not None
    row = pl.BlockSpec((bm, d), lambda i, k: (i, 0))
    in_specs = [pl.BlockSpec((bm, bk), lambda i, k: (i, k)),
                pl.BlockSpec((bk, d), lambda i, k: (k, 0)),
                row]
    args = [a, w_down_bf, x]
    out_specs = [row]
    out_shape = [jax.ShapeDtypeStruct((m, d), F32)]
    if with_norm:
        in_specs.append(pl.BlockSpec((1, d), lambda i, k: (0, 0)))
        args.append(g_next.reshape(1, d))
        out_specs.append(row)
        out_shape.append(jax.ShapeDtypeStruct((m, d), BF16))
    outs = pl.pallas_call(
        functools.partial(_down_kernel, with_norm=with_norm),
        grid=(m // bm, f // bk),
        in_specs=in_specs,
        out_specs=out_specs,
        out_shape=out_shape,
        compiler_params=_params(2),
        name="down_proj",
    )(*args)
    return (outs[0], outs[1]) if with_norm else (outs[0], None)


def kernel(x, w_in, w_out, conv_w, q_norm_g, k_norm_g, lambda_q1, lambda_k1, lambda_q2, lambda_k2,
           subln_g, attn_norm_g, mlp_norm_g, w_up, w_down, rel_bias):
    batch, seq, d = x.shape
    depth = w_in.shape[0]
    aw = d // 2
    cwid = d - aw
    n_heads = aw // HEAD_DV
    assert w_in.shape[2] == 3 * aw + 3 * cwid and rel_bias.shape == (NUM_BUCKETS, 2 * n_heads)
    t = _tile(seq, TILE["attn"])

    tables = _bias_tables(rel_bias, n_heads, t)
    scale = HEAD_DK ** -0.5 * LOG2E
    xf = x.reshape(batch * seq, d)
    h = _rmsnorm(xf, attn_norm_g[0])
    for l in range(depth):
        lam_init = 0.8 - 0.6 * math.exp(-0.3 * l)
        qk_gains = jnp.stack([jnp.tile(q_norm_g[l] * scale, 2), jnp.tile(k_norm_g[l], 2)])
        lam_vecs = jnp.stack([lambda_q1[l], lambda_k1[l], lambda_q2[l], lambda_k2[l],
                              jnp.full((HEAD_DK,), lam_init, F32),
                              jnp.full((HEAD_DK,), 1.0 - lam_init, F32)])

        qk, w_down_bf = _qk_proj(h, w_in, l, qk_gains.reshape(2, 1, LANES), aw, w_down)
        conv, v = _conv_v_proj(h, w_in, conv_w, l, aw, cwid, seq)
        attn = _attention(qk, v, tables, lam_vecs, subln_g[l].reshape(1, HEAD_DV),
                          batch, seq, n_heads, t)
        xf, hm = _out_proj(attn, conv, w_out, l, xf, mlp_norm_g[l])
        act = _up_proj(hm, w_up, l)
        xf, h = _down_proj(act, w_down_bf, xf, attn_norm_g[l + 1] if l + 1 < depth else None)
    return xf.reshape(batch, seq, d)
```

```python
import functools
import math

import numpy as np
import jax
import jax.numpy as jnp
from jax import lax
from jax.experimental import pallas as pl
from jax.experimental.pallas import tpu as pltpu

HEAD_DV = 128
HEAD_DK = HEAD_DV // 2
CONV_K = 3
NUM_BUCKETS = 32
MAX_DISTANCE = 128
MAX_EXACT = NUM_BUCKETS // 2
EPS = 1e-6
NEG = -1e30
LOG2E = math.log2(math.e)

LANES = 128
SUBLANES = 8
MXU_DIM = 256
VMEM_LIMIT_BYTES = 60 * 1024 * 1024

TILE = dict(
    norm_rows=512,
    cast_rows=256,
    qk_rows=1024,
    conv_rows=1024, conv_cols=512,
    attn=128,
    out_rows=512, out_cast_rows=512,
    up_rows=2048, up_cols=1024,
    down_rows=1024, down_k=1024,
)

F32 = jnp.float32
BF16 = jnp.bfloat16


def _tile(dim, pref):
    if dim <= pref:
        return dim
    t = (pref // LANES) * LANES
    while dim % t:
        t -= LANES
    return t


def _params(n_axes):
    return pltpu.CompilerParams(
        dimension_semantics=("arbitrary",) * n_axes,
        vmem_limit_bytes=VMEM_LIMIT_BYTES,
    )


def _rmsnorm_kernel(x_ref, g_ref, o_ref):
    x = x_ref[...]
    ms = jnp.mean(x * x, axis=-1, keepdims=True)
    o_ref[...] = (x * lax.rsqrt(ms + EPS) * g_ref[...]).astype(o_ref.dtype)


def _rmsnorm(x, g):
    m, d = x.shape
    tm = _tile(m, TILE["norm_rows"])
    return pl.pallas_call(
        _rmsnorm_kernel,
        grid=(m // tm,),
        in_specs=[pl.BlockSpec((tm, d), lambda i: (i, 0)),
                  pl.BlockSpec((1, d), lambda i: (0, 0))],
        out_specs=pl.BlockSpec((tm, d), lambda i: (i, 0)),
        out_shape=jax.ShapeDtypeStruct((m, d), BF16),
        compiler_params=_params(1),
        name="rmsnorm",
    )(x, g.reshape(1, d))


def _cast_weight(w_ref, wbf_ref):
    k = w_ref.shape[0]
    ck = _tile(k, TILE["cast_rows"])
    for r in range(0, k, ck):
        wbf_ref[r:r + ck, :] = w_ref[r:r + ck, :].astype(BF16)


def _segment_ones(width):
    seg = np.arange(width) // HEAD_DK
    return jnp.asarray(seg[:, None] == seg[None, :], dtype=BF16)


def _qk_kernel(h_ref, w_ref, g_ref, e_ref, wd_ref, o_ref, wdo_ref, wbf_ref):
    @pl.when(pl.program_id(1) == 0)
    def _():
        _cast_weight(w_ref, wbf_ref)

    wdo_ref[...] = wd_ref[...].astype(BF16)
    acc = jnp.dot(h_ref[...], wbf_ref[...], preferred_element_type=F32)
    g = g_ref[...]
    gw = e_ref.shape[0]
    for c in range(0, acc.shape[1], gw):
        a = acc[:, c:c + gw]
        ssq = jnp.dot((a * a).astype(BF16), e_ref[...], preferred_element_type=F32)
        inv = lax.rsqrt(ssq * (1.0 / HEAD_DK) + EPS)
        for cc in range(0, gw, LANES):
            o_ref[:, c + cc:c + cc + LANES] = (
                a[:, cc:cc + LANES] * inv[:, cc:cc + LANES] * g).astype(o_ref.dtype)


def _qk_proj(h, w_in, layer, gains, aw, w_down):
    m, d = h.shape
    bm = _tile(m, TILE["qk_rows"])
    nm = m // bm
    gw = _tile(aw, MXU_DIM)
    f = w_down.shape[1]
    slab = f // (2 * nm)
    assert f % (2 * nm) == 0 and slab % (2 * SUBLANES) == 0
    return pl.pallas_call(
        _qk_kernel,
        grid=(2, nm),
        in_specs=[pl.BlockSpec((bm, d), lambda j, i: (i, 0)),
                  pl.BlockSpec((None, d, aw), lambda j, i: (layer, 0, j)),
                  pl.BlockSpec((None, 1, LANES), lambda j, i: (j, 0, 0)),
                  pl.BlockSpec((gw, gw), lambda j, i: (0, 0)),
                  pl.BlockSpec((None, slab, d), lambda j, i: (layer, j * nm + i, 0))],
        out_specs=[pl.BlockSpec((bm, aw), lambda j, i: (i, j)),
                   pl.BlockSpec((slab, d), lambda j, i: (j * nm + i, 0))],
        out_shape=[jax.ShapeDtypeStruct((m, 2 * aw), BF16), jax.ShapeDtypeStruct((f, d), BF16)],
        scratch_shapes=[pltpu.VMEM((d, aw), BF16)],
        compiler_params=_params(2),
        name="qk_proj",
    )(h, w_in, gains, _segment_ones(gw), w_down)


def _conv_kernel(h_ref, wb_ref, wc_ref, wi_ref, wv_ref, cw_ref, o_ref, v_ref, wbf_ref, ubuf_ref,
                 *, tiles_per_seq):
    i = pl.program_id(1)
    bm = h_ref.shape[0]

    @pl.when(i == 0)
    def _():
        _cast_weight(wb_ref, wbf_ref.at[0])
        _cast_weight(wc_ref, wbf_ref.at[1])
        _cast_weight(wi_ref, wbf_ref.at[2])
        _cast_weight(wv_ref, wbf_ref.at[3])

    @pl.when(i % tiles_per_seq == 0)
    def _():
        ubuf_ref[0:SUBLANES, :] = jnp.zeros((SUBLANES, ubuf_ref.shape[1]), F32)

    h = h_ref[...]
    gate_c = jnp.dot(h, wbf_ref[1], preferred_element_type=F32)
    conv_in = jnp.dot(h, wbf_ref[2], preferred_element_type=F32)
    ubuf_ref[SUBLANES:SUBLANES + bm, :] = gate_c * conv_in
    gate_b = jnp.dot(h, wbf_ref[0], preferred_element_type=F32)
    cw = cw_ref[...]
    conv = (cw[0:1] * ubuf_ref[SUBLANES - 2:SUBLANES - 2 + bm, :]
            + cw[1:2] * ubuf_ref[SUBLANES - 1:SUBLANES - 1 + bm, :]
            + cw[2:3] * ubuf_ref[SUBLANES:SUBLANES + bm, :])
    o_ref[...] = (gate_b * conv).astype(o_ref.dtype)
    ubuf_ref[0:SUBLANES, :] = ubuf_ref[bm:bm + SUBLANES, :]
    v_ref[...] = jnp.dot(h, wbf_ref[3], preferred_element_type=F32).astype(v_ref.dtype)


def _conv_v_proj(h, w_in, conv_w, layer, aw, cwid, seq):
    m, d = h.shape
    assert aw == cwid
    bm = _tile(seq, TILE["conv_rows"])
    bn = _tile(cwid, TILE["conv_cols"])
    off = 3 * aw // bn
    nb = cwid // bn
    return pl.pallas_call(
        functools.partial(_conv_kernel, tiles_per_seq=seq // bm),
        grid=(nb, m // bm),
        in_specs=[pl.BlockSpec((bm, d), lambda j, i: (i, 0)),
                  pl.BlockSpec((None, d, bn), lambda j, i: (layer, 0, off + j)),
                  pl.BlockSpec((None, d, bn), lambda j, i: (layer, 0, off + nb + j)),
                  pl.BlockSpec((None, d, bn), lambda j, i: (layer, 0, off + 2 * nb + j)),
                  pl.BlockSpec((None, d, bn), lambda j, i: (layer, 0, off - nb + j)),
                  pl.BlockSpec((None, CONV_K, bn), lambda j, i: (layer, 0, j))],
        out_specs=[pl.BlockSpec((bm, bn), lambda j, i: (i, j)),
                   pl.BlockSpec((bm, bn), lambda j, i: (i, j))],
        out_shape=[jax.ShapeDtypeStruct((m, cwid), BF16), jax.ShapeDtypeStruct((m, aw), BF16)],
        scratch_shapes=[pltpu.VMEM((4, d, bn), BF16),
                        pltpu.VMEM((bm + SUBLANES, bn), F32)],
        compiler_params=_params(2),
        name="conv_v_proj",
    )(h, w_in, w_in, w_in, w_in, conv_w)


def _bucket(n):
    nf = np.maximum(n, MAX_EXACT).astype(np.float64)
    val = np.log(nf / MAX_EXACT) / math.log(MAX_DISTANCE / MAX_EXACT) * (NUM_BUCKETS - MAX_EXACT)
    large = np.minimum(MAX_EXACT + val.astype(np.int64), NUM_BUCKETS - 1)
    return np.where(n < MAX_EXACT, n, large)


def _bucket_blocks():
    n = LANES
    assert _bucket(np.array([n + 1]))[0] == NUM_BUCKETS - 1
    r = np.arange(n)[:, None]
    c = np.arange(n)[None, :]
    d0 = r - c
    diag = np.where(d0 >= 0, _bucket(np.maximum(d0, 0)), -1)
    sub = _bucket(n + r - c)
    return np.stack([diag, sub]).astype(np.int32)


def _bias_table_kernel(rb_ref, bk_ref, o_ref):
    h = pl.program_id(0)
    n = LANES
    t = o_ref.shape[2]
    zero = jnp.zeros((n, n), F32)
    masked = jnp.full((n, n), NEG, F32)
    for mp in range(2):
        col = 2 * h + mp
        far = rb_ref[NUM_BUCKETS - 1, col]
        blocks = []
        for kind in range(2):
            bk = bk_ref[kind]
            acc = jnp.where(bk < 0, NEG, 0.0).astype(F32)
            for b in range(NUM_BUCKETS - 1):
                acc = jnp.where(bk == b, (rb_ref[b, col] - far) * LOG2E, acc)
            blocks.append(acc)
        diag, sub = blocks
        for r in range(t // n):
            d = t // n + r
            for c in range(2 * t // n):
                blk = diag if c == d else sub if c == d - 1 else masked if c > d else zero
                o_ref[0, mp, r * n:(r + 1) * n, c * n:(c + 1) * n] = blk


def _bias_tables(rel_bias, n_heads, t):
    assert t % LANES == 0
    buckets = jnp.asarray(_bucket_blocks())
    return pl.pallas_call(
        _bias_table_kernel,
        grid=(n_heads,),
        in_specs=[pl.BlockSpec(memory_space=pltpu.SMEM),
                  pl.BlockSpec((2, LANES, LANES), lambda h: (0, 0, 0))],
        out_specs=pl.BlockSpec((1, 2, t, 2 * t), lambda h: (h, 0, 0, 0)),
        out_shape=jax.ShapeDtypeStruct((n_heads, 2, t, 2 * t), F32),
        compiler_params=_params(1),
        name="bias_tables",
    )(rel_bias, buckets)


def _fold_lanes(x, op):
    return functools.reduce(op, [x[:, c:c + LANES] for c in range(0, x.shape[1], LANES)])


def _attn_kernel(q_ref, k_ref, v_ref, tb_ref, lam_ref, g_ref, o_ref, v1_ref, *, t):
    seq = q_ref.shape[0]
    heads = q_ref.shape[1] // HEAD_DV
    for hh in range(heads):
        v1_ref[hh, :, 0:HEAD_DV] = v_ref[:, hh * HEAD_DV:(hh + 1) * HEAD_DV]
        v1_ref[hh, :, HEAD_DV:] = jnp.ones((seq, LANES), BF16)
    lo = lax.broadcasted_iota(jnp.int32, (1, LANES), 1) < HEAD_DK
    lv = lam_ref[...]
    lam = (jnp.exp(jnp.sum(lv[0:1] * lv[1:2], axis=-1, keepdims=True))
           - jnp.exp(jnp.sum(lv[2:3] * lv[3:4], axis=-1, keepdims=True)) + lv[4:5, 0:1])
    out_gain = lv[5:6, 0:1]
    g = g_ref[...]
    contract_last = (((1,), (1,)), ((), ()))

    def tile(hh, i):
        cols = slice(hh * HEAD_DV, (hh + 1) * HEAD_DV)
        q = q_ref[i * t:(i + 1) * t, cols]
        zero = jnp.zeros_like(q)
        q2 = jnp.concatenate([jnp.where(lo, q, zero), jnp.where(lo, zero, q)], axis=0)

        pieces = []
        if i >= 2:
            pieces.append((0, (i - 1) * t, False))
        near = min(i, 1) * t
        pieces.append((i * t - near, near + t, True))

        scores = []
        for start, width, biased in pieces:
            s = lax.dot_general(q2, k_ref[start:start + width, cols], contract_last,
                                preferred_element_type=F32)
            if biased:
                s = s + tb_ref[hh, :, :, 2 * t - width:].reshape(2 * t, width)
            scores.append(s)
        m = functools.reduce(jnp.maximum, [_fold_lanes(s, jnp.maximum) for s in scores])
        m = jnp.max(m, axis=-1, keepdims=True)
        acc = None
        for s, (start, width, _) in zip(scores, pieces):
            p = jnp.exp2(s - m).astype(BF16)
            pv = jnp.dot(p, v1_ref[hh, start:start + width, :], preferred_element_type=F32)
            acc = pv if acc is None else acc + pv
        out = acc[:, 0:HEAD_DV] / acc[:, HEAD_DV:]
        o = out[0:t] - lam * out[t:2 * t]
        ms = jnp.mean(o * o, axis=-1, keepdims=True)
        o_ref[i * t:(i + 1) * t, cols] = (o * lax.rsqrt(ms + EPS) * g * out_gain).astype(o_ref.dtype)

    nq = seq // t
    for i in [1, 0] + list(range(nq - 1, 1, -1)):
        for hh in range(heads):
            tile(hh, i)


def _attention(qk, v, tables, lam_vecs, subln_g, batch, seq, n_heads, t):
    hps = 2 if n_heads % 2 == 0 else 1
    w = hps * HEAD_DV
    ng = n_heads // hps
    return pl.pallas_call(
        functools.partial(_attn_kernel, t=t),
        grid=(batch, ng),
        in_specs=[pl.BlockSpec((seq, w), lambda b, h: (b, h)),
                  pl.BlockSpec((seq, w), lambda b, h: (b, ng + h)),
                  pl.BlockSpec((seq, w), lambda b, h: (b, h)),
                  pl.BlockSpec((hps, 2, t, 2 * t), lambda b, h: (h, 0, 0, 0)),
                  pl.BlockSpec((6, HEAD_DK), lambda b, h: (0, 0)),
                  pl.BlockSpec((1, HEAD_DV), lambda b, h: (0, 0))],
        out_specs=pl.BlockSpec((seq, w), lambda b, h: (b, h)),
        out_shape=jax.ShapeDtypeStruct((batch * seq, n_heads * HEAD_DV), BF16),
        scratch_shapes=[pltpu.VMEM((hps, seq, HEAD_DV + LANES), BF16)],
        compiler_params=_params(2),
        name="diff_attention",
    )(qk, qk, v, tables, lam_vecs, subln_g)


def _out_kernel(a_ref, c_ref, w_ref, x_ref, g_ref, xo_ref, h_ref, wbf_ref, *, n_pro):
    s = pl.program_id(0)
    ck = w_ref.shape[0]

    @pl.when(s < n_pro)
    def _():
        wbf_ref[pl.ds(pl.multiple_of(s * ck, ck), ck), :] = w_ref[...].astype(BF16)

    @pl.when(s >= n_pro)
    def _():
        aw = a_ref.shape[1]
        acc = jnp.dot(a_ref[...], wbf_ref[0:aw, :], preferred_element_type=F32)
        acc = acc + jnp.dot(c_ref[...], wbf_ref[aw:, :], preferred_element_type=F32)
        xn = x_ref[...] + acc
        xo_ref[...] = xn
        ms = jnp.mean(xn * xn, axis=-1, keepdims=True)
        h_ref[...] = (xn * lax.rsqrt(ms + EPS) * g_ref[...]).astype(h_ref.dtype)


def _out_proj(attn, conv, w_out, layer, x, g):
    m, d = x.shape
    aw, cwid = attn.shape[1], conv.shape[1]
    bm = _tile(m, TILE["out_rows"])
    ck = _tile(aw + cwid, TILE["out_cast_rows"])
    n_pro = (aw + cwid) // ck

    def row(s):
        return jnp.maximum(s - n_pro, 0)

    return pl.pallas_call(
        functools.partial(_out_kernel, n_pro=n_pro),
        grid=(n_pro + m // bm,),
        in_specs=[pl.BlockSpec((bm, aw), lambda s: (row(s), 0)),
                  pl.BlockSpec((bm, cwid), lambda s: (row(s), 0)),
                  pl.BlockSpec((None, ck, d), lambda s: (layer, jnp.minimum(s, n_pro - 1), 0)),
                  pl.BlockSpec((bm, d), lambda s: (row(s), 0)),
                  pl.BlockSpec((1, d), lambda s: (0, 0))],
        out_specs=[pl.BlockSpec((bm, d), lambda s: (row(s), 0)),
                   pl.BlockSpec((bm, d), lambda s: (row(s), 0))],
        out_shape=[jax.ShapeDtypeStruct((m, d), F32), jax.ShapeDtypeStruct((m, d), BF16)],
        scratch_shapes=[pltpu.VMEM((aw + cwid, d), BF16)],
        compiler_params=_params(1),
        name="out_proj",
    )(attn, conv, w_out, x, g.reshape(1, d))


def _up_kernel(h_ref, w_ref, o_ref, wbf_ref):
    @pl.when(pl.program_id(1) == 0)
    def _():
        _cast_weight(w_ref, wbf_ref)

    acc = jnp.dot(h_ref[...], wbf_ref[...], preferred_element_type=F32)
    o_ref[...] = jnp.square(jnp.maximum(acc, 0.0)).astype(o_ref.dtype)


def _up_proj(h, w_up, layer):
    m, d = h.shape
    f = w_up.shape[2]
    bm = _tile(m, TILE["up_rows"])
    bn = _tile(f, TILE["up_cols"])
    return pl.pallas_call(
        _up_kernel,
        grid=(f // bn, m // bm),
        in_specs=[pl.BlockSpec((bm, d), lambda j, i: (i, 0)),
                  pl.BlockSpec((None, d, bn), lambda j, i: (layer, 0, j))],
        out_specs=pl.BlockSpec((bm, bn), lambda j, i: (i, j)),
        out_shape=jax.ShapeDtypeStruct((m, f), BF16),
        scratch_shapes=[pltpu.VMEM((d, bn), BF16)],
        compiler_params=_params(2),
        name="up_proj",
    )(h, w_up)


def _down_kernel(a_ref, w_ref, x_ref, *rest, with_norm):
    if with_norm:
        g_ref, xo_ref, h_ref = rest
    else:
        (xo_ref,) = rest
    k = pl.program_id(1)
    last = pl.num_programs(1) - 1

    @pl.when(k == 0)
    def _():
        xo_ref[...] = x_ref[...]

    if not with_norm:
        xo_ref[...] += jnp.dot(a_ref[...], w_ref[...], preferred_element_type=F32)
        return

    @pl.when(k < last)
    def _():
        xo_ref[...] += jnp.dot(a_ref[...], w_ref[...], preferred_element_type=F32)

    @pl.when(k == last)
    def _():
        xn = xo_ref[...] + jnp.dot(a_ref[...], w_ref[...], preferred_element_type=F32)
        xo_ref[...] = xn
        ms = jnp.mean(xn * xn, axis=-1, keepdims=True)
        h_ref[...] = (xn * lax.rsqrt(ms + EPS) * g_ref[...]).astype(h_ref.dtype)


def _down_proj(a, w_down_bf, x, g_next):
    m, d = x.shape
    f = a.shape[1]
    bm = _tile(m, TILE["down_rows"])
    bk = _tile(f, TILE["down_k"])
    with_norm = g_next is not None
    row = pl.BlockSpec((bm, d), lambda i, k: (i, 0))
    in_specs = [pl.BlockSpec((bm, bk), lambda i, k: (i, k)),
                pl.BlockSpec((bk, d), lambda i, k: (k, 0)),
                row]
    args = [a, w_down_bf, x]
    out_specs = [row]
    out_shape = [jax.ShapeDtypeStruct((m, d), F32)]
    if with_norm:
        in_specs.append(pl.BlockSpec((1, d), lambda i, k: (0, 0)))
        args.append(g_next.reshape(1, d))
        out_specs.append(row)
        out_shape.append(jax.ShapeDtypeStruct((m, d), BF16))
    outs = pl.pallas_call(
        functools.partial(_down_kernel, with_norm=with_norm),
        grid=(m // bm, f // bk),
        in_specs=in_specs,
        out_specs=out_specs,
        out_shape=out_shape,
        compiler_params=_params(2),
        name="down_proj",
    )(*args)
    return (outs[0], outs[1]) if with_norm else (outs[0], None)


def kernel(x, w_in, w_out, conv_w, q_norm_g, k_norm_g, lambda_q1, lambda_k1, lambda_q2, lambda_k2,
           subln_g, attn_norm_g, mlp_norm_g, w_up, w_down, rel_bias):
    batch, seq, d = x.shape
    depth = w_in.shape[0]
    aw = d // 2
    cwid = d - aw
    n_heads = aw // HEAD_DV
    assert w_in.shape[2] == 3 * aw + 3 * cwid and rel_bias.shape == (NUM_BUCKETS, 2 * n_heads)
    t = _tile(seq, TILE["attn"])

    tables = _bias_tables(rel_bias, n_heads, t)
    scale = HEAD_DK ** -0.5 * LOG2E
    xf = x.reshape(batch * seq, d)
    h = _rmsnorm(xf, attn_norm_g[0])
    for l in range(depth):
        lam_init = 0.8 - 0.6 * math.exp(-0.3 * l)
        qk_gains = jnp.stack([jnp.tile(q_norm_g[l] * scale, 2), jnp.tile(k_norm_g[l], 2)])
        lam_vecs = jnp.stack([lambda_q1[l], lambda_k1[l], lambda_q2[l], lambda_k2[l],
                              jnp.full((HEAD_DK,), lam_init, F32),
                              jnp.full((HEAD_DK,), 1.0 - lam_init, F32)])

        qk, w_down_bf = _qk_proj(h, w_in, l, qk_gains.reshape(2, 1, LANES), aw, w_down)
        conv, v = _conv_v_proj(h, w_in, conv_w, l, aw, cwid, seq)
        attn = _attention(qk, v, tables, lam_vecs, subln_g[l].reshape(1, HEAD_DV),
                          batch, seq, n_heads, t)
        xf, hm = _out_proj(attn, conv, w_out, l, xf, mlp_norm_g[l])
        act = _up_proj(hm, w_up, l)
        xf, h = _down_proj(act, w_down_bf, xf, attn_norm_g[l + 1] if l + 1 < depth else None)
    return xf.reshape(batch, seq, d)
```

```python
import functools
import math

import numpy as np
import jax
import jax.numpy as jnp
from jax import lax
from jax.experimental import pallas as pl
from jax.experimental.pallas import tpu as pltpu

HEAD_DV = 128
HEAD_DK = HEAD_DV // 2
CONV_K = 3
NUM_BUCKETS = 32
MAX_DISTANCE = 128
MAX_EXACT = NUM_BUCKETS // 2
EPS = 1e-6
NEG = -1e30
LOG2E = math.log2(math.e)

LANES = 128
SUBLANES = 8
MXU_DIM = 256
VMEM_LIMIT_BYTES = 60 * 1024 * 1024

TILE = dict(
    norm_rows=512,
    cast_rows=256,
    qk_rows=1024,
    conv_rows=1024, conv_cols=512,
    attn=128,
    out_rows=512, out_cast_rows=512,
    up_rows=2048, up_cols=1024,
    down_rows=1024, down_k=1024,
)

F32 = jnp.float32
BF16 = jnp.bfloat16


def _tile(dim, pref):
    if dim <= pref:
        return dim
    t = (pref // LANES) * LANES
    while dim % t:
        t -= LANES
    return t


def _params(n_axes):
    return pltpu.CompilerParams(
        dimension_semantics=("arbitrary",) * n_axes,
        vmem_limit_bytes=VMEM_LIMIT_BYTES,
    )


def _rmsnorm_kernel(x_ref, g_ref, o_ref):
    x = x_ref[...]
    ms = jnp.mean(x * x, axis=-1, keepdims=True)
    o_ref[...] = (x * lax.rsqrt(ms + EPS) * g_ref[...]).astype(o_ref.dtype)


def _rmsnorm(x, g):
    m, d = x.shape
    tm = _tile(m, TILE["norm_rows"])
    return pl.pallas_call(
        _rmsnorm_kernel,
        grid=(m // tm,),
        in_specs=[pl.BlockSpec((tm, d), lambda i: (i, 0)),
                  pl.BlockSpec((1, d), lambda i: (0, 0))],
        out_specs=pl.BlockSpec((tm, d), lambda i: (i, 0)),
        out_shape=jax.ShapeDtypeStruct((m, d), BF16),
        compiler_params=_params(1),
        name="rmsnorm",
    )(x, g.reshape(1, d))


def _cast_weight(w_ref, wbf_ref):
    k = w_ref.shape[0]
    ck = _tile(k, TILE["cast_rows"])
    for r in range(0, k, ck):
        wbf_ref[r:r + ck, :] = w_ref[r:r + ck, :].astype(BF16)


def _segment_ones(width):
    seg = np.arange(width) // HEAD_DK
    return jnp.asarray(seg[:, None] == seg[None, :], dtype=BF16)


def _qk_kernel(h_ref, w_ref, g_ref, e_ref, wd_ref, o_ref, wdo_ref, wbf_ref):
    @pl.when(pl.program_id(1) == 0)
    def _():
        _cast_weight(w_ref, wbf_ref)

    wdo_ref[...] = wd_ref[...].astype(BF16)
    acc = jnp.dot(h_ref[...], wbf_ref[...], preferred_element_type=F32)
    g = g_ref[...]
    gw = e_ref.shape[0]
    for c in range(0, acc.shape[1], gw):
        a = acc[:, c:c + gw]
        ssq = jnp.dot((a * a).astype(BF16), e_ref[...], preferred_element_type=F32)
        inv = lax.rsqrt(ssq * (1.0 / HEAD_DK) + EPS)
        for cc in range(0, gw, LANES):
            o_ref[:, c + cc:c + cc + LANES] = (
                a[:, cc:cc + LANES] * inv[:, cc:cc + LANES] * g).astype(o_ref.dtype)


def _qk_proj(h, w_in, layer, gains, aw, w_down):
    m, d = h.shape
    bm = _tile(m, TILE["qk_rows"])
    nm = m // bm
    gw = _tile(aw, MXU_DIM)
    f = w_down.shape[1]
    slab = f // (2 * nm)
    assert f % (2 * nm) == 0 and slab % (2 * SUBLANES) == 0
    return pl.pallas_call(
        _qk_kernel,
        grid=(2, nm),
        in_specs=[pl.BlockSpec((bm, d), lambda j, i: (i, 0)),
                  pl.BlockSpec((None, d, aw), lambda j, i: (layer, 0, j)),
                  pl.BlockSpec((None, 1, LANES), lambda j, i: (j, 0, 0)),
                  pl.BlockSpec((gw, gw), lambda j, i: (0, 0)),
                  pl.BlockSpec((None, slab, d), lambda j, i: (layer, j * nm + i, 0))],
        out_specs=[pl.BlockSpec((bm, aw), lambda j, i: (i, j)),
                   pl.BlockSpec((slab, d), lambda j, i: (j * nm + i, 0))],
        out_shape=[jax.ShapeDtypeStruct((m, 2 * aw), BF16), jax.ShapeDtypeStruct((f, d), BF16)],
        scratch_shapes=[pltpu.VMEM((d, aw), BF16)],
        compiler_params=_params(2),
        name="qk_proj",
    )(h, w_in, gains, _segment_ones(gw), w_down)


def _conv_kernel(h_ref, wb_ref, wc_ref, wi_ref, wv_ref, cw_ref, o_ref, v_ref, wbf_ref, ubuf_ref,
                 *, tiles_per_seq):
    i = pl.program_id(1)
    bm = h_ref.shape[0]

    @pl.when(i == 0)
    def _():
        _cast_weight(wb_ref, wbf_ref.at[0])
        _cast_weight(wc_ref, wbf_ref.at[1])
        _cast_weight(wi_ref, wbf_ref.at[2])
        _cast_weight(wv_ref, wbf_ref.at[3])

    @pl.when(i % tiles_per_seq == 0)
    def _():
        ubuf_ref[0:SUBLANES, :] = jnp.zeros((SUBLANES, ubuf_ref.shape[1]), F32)

    h = h_ref[...]
    gate_c = jnp.dot(h, wbf_ref[1], preferred_element_type=F32)
    conv_in = jnp.dot(h, wbf_ref[2], preferred_element_type=F32)
    ubuf_ref[SUBLANES:SUBLANES + bm, :] = gate_c * conv_in
    gate_b = jnp.dot(h, wbf_ref[0], preferred_element_type=F32)
    cw = cw_ref[...]
    conv = (cw[0:1] * ubuf_ref[SUBLANES - 2:SUBLANES - 2 + bm, :]
            + cw[1:2] * ubuf_ref[SUBLANES - 1:SUBLANES - 1 + bm, :]
            + cw[2:3] * ubuf_ref[SUBLANES:SUBLANES + bm, :])
    o_ref[...] = (gate_b * conv).astype(o_ref.dtype)
    ubuf_ref[0:SUBLANES, :] = ubuf_ref[bm:bm + SUBLANES, :]
    v_ref[...] = jnp.dot(h, wbf_ref[3], preferred_element_type=F32).astype(v_ref.dtype)


def _conv_v_proj(h, w_in, conv_w, layer, aw, cwid, seq):
    m, d = h.shape
    assert aw == cwid
    bm = _tile(seq, TILE["conv_rows"])
    bn = _tile(cwid, TILE["conv_cols"])
    off = 3 * aw // bn
    nb = cwid // bn
    return pl.pallas_call(
        functools.partial(_conv_kernel, tiles_per_seq=seq // bm),
        grid=(nb, m // bm),
        in_specs=[pl.BlockSpec((bm, d), lambda j, i: (i, 0)),
                  pl.BlockSpec((None, d, bn), lambda j, i: (layer, 0, off + j)),
                  pl.BlockSpec((None, d, bn), lambda j, i: (layer, 0, off + nb + j)),
                  pl.BlockSpec((None, d, bn), lambda j, i: (layer, 0, off + 2 * nb + j)),
                  pl.BlockSpec((None, d, bn), lambda j, i: (layer, 0, off - nb + j)),
                  pl.BlockSpec((None, CONV_K, bn), lambda j, i: (layer, 0, j))],
        out_specs=[pl.BlockSpec((bm, bn), lambda j, i: (i, j)),
                   pl.BlockSpec((bm, bn), lambda j, i: (i, j))],
        out_shape=[jax.ShapeDtypeStruct((m, cwid), BF16), jax.ShapeDtypeStruct((m, aw), BF16)],
        scratch_shapes=[pltpu.VMEM((4, d, bn), BF16),
                        pltpu.VMEM((bm + SUBLANES, bn), F32)],
        compiler_params=_params(2),
        name="conv_v_proj",
    )(h, w_in, w_in, w_in, w_in, conv_w)


def _bucket(n):
    nf = np.maximum(n, MAX_EXACT).astype(np.float64)
    val = np.log(nf / MAX_EXACT) / math.log(MAX_DISTANCE / MAX_EXACT) * (NUM_BUCKETS - MAX_EXACT)
    large = np.minimum(MAX_EXACT + val.astype(np.int64), NUM_BUCKETS - 1)
    return np.where(n < MAX_EXACT, n, large)


def _bucket_blocks():
    n = LANES
    assert _bucket(np.array([n + 1]))[0] == NUM_BUCKETS - 1
    r = np.arange(n)[:, None]
    c = np.arange(n)[None, :]
    d0 = r - c
    diag = np.where(d0 >= 0, _bucket(np.maximum(d0, 0)), -1)
    sub = _bucket(n + r - c)
    return np.stack([diag, sub]).astype(np.int32)


def _bias_table_kernel(rb_ref, bk_ref, o_ref):
    h = pl.program_id(0)
    n = LANES
    t = o_ref.shape[2]
    zero = jnp.zeros((n, n), F32)
    masked = jnp.full((n, n), NEG, F32)
    for mp in range(2):
        col = 2 * h + mp
        far = rb_ref[NUM_BUCKETS - 1, col]
        blocks = []
        for kind in range(2):
            bk = bk_ref[kind]
            acc = jnp.where(bk < 0, NEG, 0.0).astype(F32)
            for b in range(NUM_BUCKETS - 1):
                acc = jnp.where(bk == b, (rb_ref[b, col] - far) * LOG2E, acc)
            blocks.append(acc)
        diag, sub = blocks
        for r in range(t // n):
            d = t // n + r
            for c in range(2 * t // n):
                blk = diag if c == d else sub if c == d - 1 else masked if c > d else zero
                o_ref[0, mp, r * n:(r + 1) * n, c * n:(c + 1) * n] = blk


def _bias_tables(rel_bias, n_heads, t):
    assert t % LANES == 0
    buckets = jnp.asarray(_bucket_blocks())
    return pl.pallas_call(
        _bias_table_kernel,
        grid=(n_heads,),
        in_specs=[pl.BlockSpec(memory_space=pltpu.SMEM),
                  pl.BlockSpec((2, LANES, LANES), lambda h: (0, 0, 0))],
        out_specs=pl.BlockSpec((1, 2, t, 2 * t), lambda h: (h, 0, 0, 0)),
        out_shape=jax.ShapeDtypeStruct((n_heads, 2, t, 2 * t), F32),
        compiler_params=_params(1),
        name="bias_tables",
    )(rel_bias, buckets)


def _fold_lanes(x, op):
    return functools.reduce(op, [x[:, c:c + LANES] for c in range(0, x.shape[1], LANES)])


def _attn_kernel(q_ref, k_ref, v_ref, tb_ref, lam_ref, g_ref, o_ref, v1_ref, *, t):
    seq = q_ref.shape[0]
    heads = q_ref.shape[1] // HEAD_DV
    for hh in range(heads):
        v1_ref[hh, :, 0:HEAD_DV] = v_ref[:, hh * HEAD_DV:(hh + 1) * HEAD_DV]
        v1_ref[hh, :, HEAD_DV:] = jnp.ones((seq, LANES), BF16)
    lo = lax.broadcasted_iota(jnp.int32, (1, LANES), 1) < HEAD_DK
    lv = lam_ref[...]
    lam = (jnp.exp(jnp.sum(lv[0:1] * lv[1:2], axis=-1, keepdims=True))
           - jnp.exp(jnp.sum(lv[2:3] * lv[3:4], axis=-1, keepdims=True)) + lv[4:5, 0:1])
    out_gain = lv[5:6, 0:1]
    g = g_ref[...]
    contract_last = (((1,), (1,)), ((), ()))

    def tile(hh, i):
        cols = slice(hh * HEAD_DV, (hh + 1) * HEAD_DV)
        q = q_ref[i * t:(i + 1) * t, cols]
        zero = jnp.zeros_like(q)
        q2 = jnp.concatenate([jnp.where(lo, q, zero), jnp.where(lo, zero, q)], axis=0)

        pieces = []
        if i >= 2:
            pieces.append((0, (i - 1) * t, False))
        near = min(i, 1) * t
        pieces.append((i * t - near, near + t, True))

        scores = []
        for start, width, biased in pieces:
            s = lax.dot_general(q2, k_ref[start:start + width, cols], contract_last,
                                preferred_element_type=F32)
            if biased:
                s = s + tb_ref[hh, :, :, 2 * t - width:].reshape(2 * t, width)
            scores.append(s)
        m = functools.reduce(jnp.maximum, [_fold_lanes(s, jnp.maximum) for s in scores])
        m = jnp.max(m, axis=-1, keepdims=True)
        acc = None
        for s, (start, width, _) in zip(scores, pieces):
            p = jnp.exp2(s - m).astype(BF16)
            pv = jnp.dot(p, v1_ref[hh, start:start + width, :], preferred_element_type=F32)
            acc = pv if acc is None else acc + pv
        out = acc[:, 0:HEAD_DV] / acc[:, HEAD_DV:]
        o = out[0:t] - lam * out[t:2 * t]
        ms = jnp.mean(o * o, axis=-1, keepdims=True)
        o_ref[i * t:(i + 1) * t, cols] = (o * lax.rsqrt(ms + EPS) * g * out_gain).astype(o_ref.dtype)

    nq = seq // t
    for i in [1, 0] + list(range(nq - 1, 1, -1)):
        for hh in range(heads):
            tile(hh, i)


def _attention(qk, v, tables, lam_vecs, subln_g, batch, seq, n_heads, t):
    hps = 2 if n_heads % 2 == 0 else 1
    w = hps * HEAD_DV
    ng = n_heads // hps
    return pl.pallas_call(
        functools.partial(_attn_kernel, t=t),
        grid=(batch, ng),
        in_specs=[pl.BlockSpec((seq, w), lambda b, h: (b, h)),
                  pl.BlockSpec((seq, w), lambda b, h: (b, ng + h)),
                  pl.BlockSpec((seq, w), lambda b, h: (b, h)),
                  pl.BlockSpec((hps, 2, t, 2 * t), lambda b, h: (h, 0, 0, 0)),
                  pl.BlockSpec((6, HEAD_DK), lambda b, h: (0, 0)),
                  pl.BlockSpec((1, HEAD_DV), lambda b, h: (0, 0))],
        out_specs=pl.BlockSpec((seq, w), lambda b, h: (b, h)),
        out_shape=jax.ShapeDtypeStruct((batch * seq, n_heads * HEAD_DV), BF16),
        scratch_shapes=[pltpu.VMEM((hps, seq, HEAD_DV + LANES), BF16)],
        compiler_params=_params(2),
        name="diff_attention",
    )(qk, qk, v, tables, lam_vecs, subln_g)


def _out_kernel(a_ref, c_ref, w_ref, x_ref, g_ref, xo_ref, h_ref, wbf_ref, *, n_pro):
    s = pl.program_id(0)
    ck = w_ref.shape[0]

    @pl.when(s < n_pro)
    def _():
        wbf_ref[pl.ds(pl.multiple_of(s * ck, ck), ck), :] = w_ref[...].astype(BF16)

    @pl.when(s >= n_pro)
    def _():
        aw = a_ref.shape[1]
        acc = jnp.dot(a_ref[...], wbf_ref[0:aw, :], preferred_element_type=F32)
        acc = acc + jnp.dot(c_ref[...], wbf_ref[aw:, :], preferred_element_type=F32)
        xn = x_ref[...] + acc
        xo_ref[...] = xn
        ms = jnp.mean(xn * xn, axis=-1, keepdims=True)
        h_ref[...] = (xn * lax.rsqrt(ms + EPS) * g_ref[...]).astype(h_ref.dtype)


def _out_proj(attn, conv, w_out, layer, x, g):
    m, d = x.shape
    aw, cwid = attn.shape[1], conv.shape[1]
    bm = _tile(m, TILE["out_rows"])
    ck = _tile(aw + cwid, TILE["out_cast_rows"])
    n_pro = (aw + cwid) // ck

    def row(s):
        return jnp.maximum(s - n_pro, 0)

    return pl.pallas_call(
        functools.partial(_out_kernel, n_pro=n_pro),
        grid=(n_pro + m // bm,),
        in_specs=[pl.BlockSpec((bm, aw), lambda s: (row(s), 0)),
                  pl.BlockSpec((bm, cwid), lambda s: (row(s), 0)),
                  pl.BlockSpec((None, ck, d), lambda s: (layer, jnp.minimum(s, n_pro - 1), 0)),
                  pl.BlockSpec((bm, d), lambda s: (row(s), 0)),
                  pl.BlockSpec((1, d), lambda s: (0, 0))],
        out_specs=[pl.BlockSpec((bm, d), lambda s: (row(s), 0)),
                   pl.BlockSpec((bm, d), lambda s: (row(s), 0))],
        out_shape=[jax.ShapeDtypeStruct((m, d), F32), jax.ShapeDtypeStruct((m, d), BF16)],
        scratch_shapes=[pltpu.VMEM((aw + cwid, d), BF16)],
        compiler_params=_params(1),
        name="out_proj",
    )(attn, conv, w_out, x, g.reshape(1, d))


def _up_kernel(h_ref, w_ref, o_ref, wbf_ref):
    def finish(acc):
        o_ref[...] = jnp.square(jnp.maximum(acc, 0.0)).astype(o_ref.dtype)

    @pl.when(pl.program_id(1) == 0)
    def _():
        k = w_ref.shape[0]
        ck = k // 4
        acc = None
        for c in range(0, k, ck):
            wbf_ref[c:c + ck, :] = w_ref[c:c + ck, :].astype(BF16)
            part = jnp.dot(h_ref[:, c:c + ck], wbf_ref[c:c + ck, :], preferred_element_type=F32)
            acc = part if acc is None else acc + part
        finish(acc)

    @pl.when(pl.program_id(1) != 0)
    def _():
        finish(jnp.dot(h_ref[...], wbf_ref[...], preferred_element_type=F32))


def _up_proj(h, w_up, layer):
    m, d = h.shape
    f = w_up.shape[2]
    bm = _tile(m, TILE["up_rows"])
    bn = _tile(f, TILE["up_cols"])
    return pl.pallas_call(
        _up_kernel,
        grid=(f // bn, m // bm),
        in_specs=[pl.BlockSpec((bm, d), lambda j, i: (i, 0)),
                  pl.BlockSpec((None, d, bn), lambda j, i: (layer, 0, j))],
        out_specs=pl.BlockSpec((bm, bn), lambda j, i: (i, j)),
        out_shape=jax.ShapeDtypeStruct((m, f), BF16),
        scratch_shapes=[pltpu.VMEM((d, bn), BF16)],
        compiler_params=_params(2),
        name="up_proj",
    )(h, w_up)


def _down_kernel(a_ref, w_ref, x_ref, *rest, with_norm):
    if with_norm:
        g_ref, xo_ref, h_ref = rest
    else:
        (xo_ref,) = rest
    k = pl.program_id(1)
    last = pl.num_programs(1) - 1

    @pl.when(k == 0)
    def _():
        xo_ref[...] = x_ref[...]

    if not with_norm:
        xo_ref[...] += jnp.dot(a_ref[...], w_ref[...], preferred_element_type=F32)
        return

    @pl.when(k < last)
    def _():
        xo_ref[...] += jnp.dot(a_ref[...], w_ref[...], preferred_element_type=F32)

    @pl.when(k == last)
    def _():
        xn = xo_ref[...] + jnp.dot(a_ref[...], w_ref[...], preferred_element_type=F32)
        xo_ref[...] = xn
        ms = jnp.mean(xn * xn, axis=-1, keepdims=True)
        h_ref[...] = (xn * lax.rsqrt(ms + EPS) * g_ref[...]).astype(h_ref.dtype)


def _down_proj(a, w_down_bf, x, g_next):
    m, d = x.shape
    f = a.shape[1]
    bm = _tile(m, TILE["down_rows"])
    bk = _tile(f, TILE["down_k"])
    with_norm = g_next is not None
    row = pl.BlockSpec((bm, d), lambda i, k: (i, 0))
    in_specs = [pl.BlockSpec((bm, bk), lambda i, k: (i, k)),
                pl.BlockSpec((bk, d), lambda i, k: (k, 0)),
                row]
    args = [a, w_down_bf, x]
    out_specs = [row]
    out_shape = [jax.ShapeDtypeStruct((m, d), F32)]
    if with_norm:
        in_specs.append(pl.BlockSpec((1, d), lambda i, k: (0, 0)))
        args.append(g_next.reshape(1, d))
        out_specs.append(row)
        out_shape.append(jax.ShapeDtypeStruct((m, d), BF16))
    outs = pl.pallas_call(
        functools.partial(_down_kernel, with_norm=with_norm),
        grid=(m // bm, f // bk),
        in_specs=in_specs,
        out_specs=out_specs,
        out_shape=out_shape,
        compiler_params=_params(2),
        name="down_proj",
    )(*args)
    return (outs[0], outs[1]) if with_norm else (outs[0], None)


def kernel(x, w_in, w_out, conv_w, q_norm_g, k_norm_g, lambda_q1, lambda_k1, lambda_q2, lambda_k2,
           subln_g, attn_norm_g, mlp_norm_g, w_up, w_down, rel_bias):
    batch, seq, d = x.shape
    depth = w_in.shape[0]
    aw = d // 2
    cwid = d - aw
    n_heads = aw // HEAD_DV
    assert w_in.shape[2] == 3 * aw + 3 * cwid and rel_bias.shape == (NUM_BUCKETS, 2 * n_heads)
    t = _tile(seq, TILE["attn"])

    tables = _bias_tables(rel_bias, n_heads, t)
    scale = HEAD_DK ** -0.5 * LOG2E
    xf = x.reshape(batch * seq, d)
    h = _rmsnorm(xf, attn_norm_g[0])
    for l in range(depth):
        lam_init = 0.8 - 0.6 * math.exp(-0.3 * l)
        qk_gains = jnp.stack([jnp.tile(q_norm_g[l] * scale, 2), jnp.tile(k_norm_g[l], 2)])
        lam_vecs = jnp.stack([lambda_q1[l], lambda_k1[l], lambda_q2[l], lambda_k2[l],
                              jnp.full((HEAD_DK,), lam_init, F32),
                              jnp.full((HEAD_DK,), 1.0 - lam_init, F32)])

        qk, w_down_bf = _qk_proj(h, w_in, l, qk_gains.reshape(2, 1, LANES), aw, w_down)
        conv, v = _conv_v_proj(h, w_in, conv_w, l, aw, cwid, seq)
        attn = _attention(qk, v, tables, lam_vecs, subln_g[l].reshape(1, HEAD_DV),
                          batch, seq, n_heads, t)
        xf, hm = _out_proj(attn, conv, w_out, l, xf, mlp_norm_g[l])
        act = _up_proj(hm, w_up, l)
        xf, h = _down_proj(act, w_down_bf, xf, attn_norm_g[l + 1] if l + 1 < depth else None)
    return xf.reshape(batch, seq, d)
```

```python
import functools
import math

import numpy as np
import jax
import jax.numpy as jnp
from jax import lax
from jax.experimental import pallas as pl
from jax.experimental.pallas import tpu as pltpu

HEAD_DV = 128
HEAD_DK = HEAD_DV // 2
CONV_K = 3
NUM_BUCKETS = 32
MAX_DISTANCE = 128
MAX_EXACT = NUM_BUCKETS // 2
EPS = 1e-6
NEG = -1e30
LOG2E = math.log2(math.e)

LANES = 128
SUBLANES = 8
MXU_DIM = 256
VMEM_LIMIT_BYTES = 60 * 1024 * 1024

TILE = dict(
    norm_rows=512,
    cast_rows=256,
    qk_rows=1024,
    conv_rows=1024, conv_cols=512,
    attn=128,
    out_rows=512, out_cast_rows=512,
    up_rows=2048, up_cols=1024,
    down_rows=1024, down_k=1024,
)

F32 = jnp.float32
BF16 = jnp.bfloat16


def _tile(dim, pref):
    if dim <= pref:
        return dim
    t = (pref // LANES) * LANES
    while dim % t:
        t -= LANES
    return t


def _params(n_axes):
    return pltpu.CompilerParams(
        dimension_semantics=("arbitrary",) * n_axes,
        vmem_limit_bytes=VMEM_LIMIT_BYTES,
    )


def _rmsnorm_kernel(x_ref, g_ref, o_ref):
    x = x_ref[...]
    ms = jnp.mean(x * x, axis=-1, keepdims=True)
    o_ref[...] = (x * lax.rsqrt(ms + EPS) * g_ref[...]).astype(o_ref.dtype)


def _rmsnorm(x, g):
    m, d = x.shape
    tm = _tile(m, TILE["norm_rows"])
    return pl.pallas_call(
        _rmsnorm_kernel,
        grid=(m // tm,),
        in_specs=[pl.BlockSpec((tm, d), lambda i: (i, 0)),
                  pl.BlockSpec((1, d), lambda i: (0, 0))],
        out_specs=pl.BlockSpec((tm, d), lambda i: (i, 0)),
        out_shape=jax.ShapeDtypeStruct((m, d), BF16),
        compiler_params=_params(1),
        name="rmsnorm",
    )(x, g.reshape(1, d))


def _cast_weight(w_ref, wbf_ref):
    k = w_ref.shape[0]
    ck = _tile(k, TILE["cast_rows"])
    for r in range(0, k, ck):
        wbf_ref[r:r + ck, :] = w_ref[r:r + ck, :].astype(BF16)


def _segment_ones(width):
    seg = np.arange(width) // HEAD_DK
    return jnp.asarray(seg[:, None] == seg[None, :], dtype=BF16)


def _qk_kernel(h_ref, w_ref, g_ref, e_ref, wd_ref, o_ref, wdo_ref, wbf_ref):
    @pl.when(pl.program_id(1) == 0)
    def _():
        _cast_weight(w_ref, wbf_ref)

    wdo_ref[...] = wd_ref[...].astype(BF16)
    acc = jnp.dot(h_ref[...], wbf_ref[...], preferred_element_type=F32)
    g = g_ref[...]
    gw = e_ref.shape[0]
    for c in range(0, acc.shape[1], gw):
        a = acc[:, c:c + gw]
        ssq = jnp.dot((a * a).astype(BF16), e_ref[...], preferred_element_type=F32)
        inv = lax.rsqrt(ssq * (1.0 / HEAD_DK) + EPS)
        for cc in range(0, gw, LANES):
            o_ref[:, c + cc:c + cc + LANES] = (
                a[:, cc:cc + LANES] * inv[:, cc:cc + LANES] * g).astype(o_ref.dtype)


def _qk_proj(h, w_in, layer, gains, aw, w_down):
    m, d = h.shape
    bm = _tile(m, TILE["qk_rows"])
    nm = m // bm
    gw = _tile(aw, MXU_DIM)
    f = w_down.shape[1]
    slab = f // (2 * nm)
    assert f % (2 * nm) == 0 and slab % (2 * SUBLANES) == 0
    return pl.pallas_call(
        _qk_kernel,
        grid=(2, nm),
        in_specs=[pl.BlockSpec((bm, d), lambda j, i: (i, 0)),
                  pl.BlockSpec((None, d, aw), lambda j, i: (layer, 0, j)),
                  pl.BlockSpec((None, 1, LANES), lambda j, i: (j, 0, 0)),
                  pl.BlockSpec((gw, gw), lambda j, i: (0, 0)),
                  pl.BlockSpec((None, slab, d), lambda j, i: (layer, j * nm + i, 0))],
        out_specs=[pl.BlockSpec((bm, aw), lambda j, i: (i, j)),
                   pl.BlockSpec((slab, d), lambda j, i: (j * nm + i, 0))],
        out_shape=[jax.ShapeDtypeStruct((m, 2 * aw), BF16), jax.ShapeDtypeStruct((f, d), BF16)],
        scratch_shapes=[pltpu.VMEM((d, aw), BF16)],
        compiler_params=_params(2),
        name="qk_proj",
    )(h, w_in, gains, _segment_ones(gw), w_down)


def _conv_kernel(h_ref, wb_ref, wc_ref, wi_ref, wv_ref, cw_ref, o_ref, v_ref, wbf_ref, ubuf_ref,
                 *, tiles_per_seq):
    i = pl.program_id(1)
    bm = h_ref.shape[0]

    @pl.when(i == 0)
    def _():
        _cast_weight(wb_ref, wbf_ref.at[0])
        _cast_weight(wc_ref, wbf_ref.at[1])
        _cast_weight(wi_ref, wbf_ref.at[2])
        _cast_weight(wv_ref, wbf_ref.at[3])

    @pl.when(i % tiles_per_seq == 0)
    def _():
        ubuf_ref[0:SUBLANES, :] = jnp.zeros((SUBLANES, ubuf_ref.shape[1]), F32)

    h = h_ref[...]
    gate_c = jnp.dot(h, wbf_ref[1], preferred_element_type=F32)
    conv_in = jnp.dot(h, wbf_ref[2], preferred_element_type=F32)
    ubuf_ref[SUBLANES:SUBLANES + bm, :] = gate_c * conv_in
    gate_b = jnp.dot(h, wbf_ref[0], preferred_element_type=F32)
    cw = cw_ref[...]
    conv = (cw[0:1] * ubuf_ref[SUBLANES - 2:SUBLANES - 2 + bm, :]
            + cw[1:2] * ubuf_ref[SUBLANES - 1:SUBLANES - 1 + bm, :]
            + cw[2:3] * ubuf_ref[SUBLANES:SUBLANES + bm, :])
    o_ref[...] = (gate_b * conv).astype(o_ref.dtype)
    ubuf_ref[0:SUBLANES, :] = ubuf_ref[bm:bm + SUBLANES, :]
    v_ref[...] = jnp.dot(h, wbf_ref[3], preferred_element_type=F32).astype(v_ref.dtype)


def _conv_v_proj(h, w_in, conv_w, layer, aw, cwid, seq):
    m, d = h.shape
    assert aw == cwid
    bm = _tile(seq, TILE["conv_rows"])
    bn = _tile(cwid, TILE["conv_cols"])
    off = 3 * aw // bn
    nb = cwid // bn
    return pl.pallas_call(
        functools.partial(_conv_kernel, tiles_per_seq=seq // bm),
        grid=(nb, m // bm),
        in_specs=[pl.BlockSpec((bm, d), lambda j, i: (i, 0)),
                  pl.BlockSpec((None, d, bn), lambda j, i: (layer, 0, off + j)),
                  pl.BlockSpec((None, d, bn), lambda j, i: (layer, 0, off + nb + j)),
                  pl.BlockSpec((None, d, bn), lambda j, i: (layer, 0, off + 2 * nb + j)),
                  pl.BlockSpec((None, d, bn), lambda j, i: (layer, 0, off - nb + j)),
                  pl.BlockSpec((None, CONV_K, bn), lambda j, i: (layer, 0, j))],
        out_specs=[pl.BlockSpec((bm, bn), lambda j, i: (i, j)),
                   pl.BlockSpec((bm, bn), lambda j, i: (i, j))],
        out_shape=[jax.ShapeDtypeStruct((m, cwid), BF16), jax.ShapeDtypeStruct((m, aw), BF16)],
        scratch_shapes=[pltpu.VMEM((4, d, bn), BF16),
                        pltpu.VMEM((bm + SUBLANES, bn), F32)],
        compiler_params=_params(2),
        name="conv_v_proj",
    )(h, w_in, w_in, w_in, w_in, conv_w)


def _bucket(n):
    nf = np.maximum(n, MAX_EXACT).astype(np.float64)
    val = np.log(nf / MAX_EXACT) / math.log(MAX_DISTANCE / MAX_EXACT) * (NUM_BUCKETS - MAX_EXACT)
    large = np.minimum(MAX_EXACT + val.astype(np.int64), NUM_BUCKETS - 1)
    return np.where(n < MAX_EXACT, n, large)


def _bucket_blocks():
    n = LANES
    assert _bucket(np.array([n + 1]))[0] == NUM_BUCKETS - 1
    r = np.arange(n)[:, None]
    c = np.arange(n)[None, :]
    d0 = r - c
    diag = np.where(d0 >= 0, _bucket(np.maximum(d0, 0)), -1)
    sub = _bucket(n + r - c)
    return np.stack([diag, sub]).astype(np.int32)


def _bias_table_kernel(rb_ref, bk_ref, o_ref):
    h = pl.program_id(0)
    n = LANES
    t = o_ref.shape[2]
    zero = jnp.zeros((n, n), F32)
    masked = jnp.full((n, n), NEG, F32)
    for mp in range(2):
        col = 2 * h + mp
        far = rb_ref[NUM_BUCKETS - 1, col]
        blocks = []
        for kind in range(2):
            bk = bk_ref[kind]
            acc = jnp.where(bk < 0, NEG, 0.0).astype(F32)
            for b in range(NUM_BUCKETS - 1):
                acc = jnp.where(bk == b, (rb_ref[b, col] - far) * LOG2E, acc)
            blocks.append(acc)
        diag, sub = blocks
        for r in range(t // n):
            d = t // n + r
            for c in range(2 * t // n):
                blk = diag if c == d else sub if c == d - 1 else masked if c > d else zero
                o_ref[0, mp, r * n:(r + 1) * n, c * n:(c + 1) * n] = blk


def _bias_tables(rel_bias, n_heads, t):
    assert t % LANES == 0
    buckets = jnp.asarray(_bucket_blocks())
    return pl.pallas_call(
        _bias_table_kernel,
        grid=(n_heads,),
        in_specs=[pl.BlockSpec(memory_space=pltpu.SMEM),
                  pl.BlockSpec((2, LANES, LANES), lambda h: (0, 0, 0))],
        out_specs=pl.BlockSpec((1, 2, t, 2 * t), lambda h: (h, 0, 0, 0)),
        out_shape=jax.ShapeDtypeStruct((n_heads, 2, t, 2 * t), F32),
        compiler_params=_params(1),
        name="bias_tables",
    )(rel_bias, buckets)


def _fold_lanes(x, op):
    return functools.reduce(op, [x[:, c:c + LANES] for c in range(0, x.shape[1], LANES)])


def _attn_kernel(q_ref, k_ref, v_ref, tb_ref, lam_ref, g_ref, o_ref, v1_ref, *, t):
    seq = q_ref.shape[0]
    heads = q_ref.shape[1] // HEAD_DV
    for hh in range(heads):
        v1_ref[hh, :, 0:HEAD_DV] = v_ref[:, hh * HEAD_DV:(hh + 1) * HEAD_DV]
        v1_ref[hh, :, HEAD_DV:] = jnp.ones((seq, LANES), BF16)
    lo = lax.broadcasted_iota(jnp.int32, (1, LANES), 1) < HEAD_DK
    lv = lam_ref[...]
    lam = (jnp.exp(jnp.sum(lv[0:1] * lv[1:2], axis=-1, keepdims=True))
           - jnp.exp(jnp.sum(lv[2:3] * lv[3:4], axis=-1, keepdims=True)) + lv[4:5, 0:1])
    out_gain = lv[5:6, 0:1]
    g = g_ref[...]
    contract_last = (((1,), (1,)), ((), ()))

    def tile(hh, i):
        cols = slice(hh * HEAD_DV, (hh + 1) * HEAD_DV)
        q = q_ref[i * t:(i + 1) * t, cols]
        zero = jnp.zeros_like(q)
        q2 = jnp.concatenate([jnp.where(lo, q, zero), jnp.where(lo, zero, q)], axis=0)

        pieces = []
        if i >= 2:
            pieces.append((0, (i - 1) * t, False))
        near = min(i, 1) * t
        pieces.append((i * t - near, near + t, True))

        scores = []
        for start, width, biased in pieces:
            s = lax.dot_general(q2, k_ref[start:start + width, cols], contract_last,
                                preferred_element_type=F32)
            if biased:
                s = s + tb_ref[hh, :, :, 2 * t - width:].reshape(2 * t, width)
            scores.append(s)
        m = functools.reduce(jnp.maximum, [_fold_lanes(s, jnp.maximum) for s in scores])
        m = jnp.max(m, axis=-1, keepdims=True)
        acc = None
        for s, (start, width, _) in zip(scores, pieces):
            p = jnp.exp2(s - m).astype(BF16)
            pv = jnp.dot(p, v1_ref[hh, start:start + width, :], preferred_element_type=F32)
            acc = pv if acc is None else acc + pv
        out = acc[:, 0:HEAD_DV] / acc[:, HEAD_DV:]
        o = out[0:t] - lam * out[t:2 * t]
        ms = jnp.mean(o * o, axis=-1, keepdims=True)
        o_ref[i * t:(i + 1) * t, cols] = (o * lax.rsqrt(ms + EPS) * g * out_gain).astype(o_ref.dtype)

    nq = seq // t
    for i in [1, 0] + list(range(nq - 1, 1, -1)):
        for hh in range(heads):
            tile(hh, i)


def _attention(qk, v, tables, lam_vecs, subln_g, batch, seq, n_heads, t):
    hps = 2 if n_heads % 2 == 0 else 1
    w = hps * HEAD_DV
    ng = n_heads // hps
    return pl.pallas_call(
        functools.partial(_attn_kernel, t=t),
        grid=(ng, batch),
        in_specs=[pl.BlockSpec((seq, w), lambda h, b: (b, h)),
                  pl.BlockSpec((seq, w), lambda h, b: (b, ng + h)),
                  pl.BlockSpec((seq, w), lambda h, b: (b, h)),
                  pl.BlockSpec((hps, 2, t, 2 * t), lambda h, b: (h, 0, 0, 0)),
                  pl.BlockSpec((6, HEAD_DK), lambda h, b: (0, 0)),
                  pl.BlockSpec((1, HEAD_DV), lambda h, b: (0, 0))],
        out_specs=pl.BlockSpec((seq, w), lambda h, b: (b, h)),
        out_shape=jax.ShapeDtypeStruct((batch * seq, n_heads * HEAD_DV), BF16),
        scratch_shapes=[pltpu.VMEM((hps, seq, HEAD_DV + LANES), BF16)],
        compiler_params=_params(2),
        name="diff_attention",
    )(qk, qk, v, tables, lam_vecs, subln_g)


def _out_kernel(a_ref, c_ref, w_ref, x_ref, g_ref, xo_ref, h_ref, wbf_ref, *, n_pro):
    s = pl.program_id(0)
    ck = w_ref.shape[0]

    @pl.when(s < n_pro)
    def _():
        wbf_ref[pl.ds(pl.multiple_of(s * ck, ck), ck), :] = w_ref[...].astype(BF16)

    @pl.when(s >= n_pro)
    def _():
        aw = a_ref.shape[1]
        acc = jnp.dot(a_ref[...], wbf_ref[0:aw, :], preferred_element_type=F32)
        acc = acc + jnp.dot(c_ref[...], wbf_ref[aw:, :], preferred_element_type=F32)
        xn = x_ref[...] + acc
        xo_ref[...] = xn
        ms = jnp.mean(xn * xn, axis=-1, keepdims=True)
        h_ref[...] = (xn * lax.rsqrt(ms + EPS) * g_ref[...]).astype(h_ref.dtype)


def _out_proj(attn, conv, w_out, layer, x, g):
    m, d = x.shape
    aw, cwid = attn.shape[1], conv.shape[1]
    bm = _tile(m, TILE["out_rows"])
    ck = _tile(aw + cwid, TILE["out_cast_rows"])
    n_pro = (aw + cwid) // ck

    def row(s):
        return jnp.maximum(s - n_pro, 0)

    return pl.pallas_call(
        functools.partial(_out_kernel, n_pro=n_pro),
        grid=(n_pro + m // bm,),
        in_specs=[pl.BlockSpec((bm, aw), lambda s: (row(s), 0)),
                  pl.BlockSpec((bm, cwid), lambda s: (row(s), 0)),
                  pl.BlockSpec((None, ck, d), lambda s: (layer, jnp.minimum(s, n_pro - 1), 0)),
                  pl.BlockSpec((bm, d), lambda s: (row(s), 0)),
                  pl.BlockSpec((1, d), lambda s: (0, 0))],
        out_specs=[pl.BlockSpec((bm, d), lambda s: (row(s), 0)),
                   pl.BlockSpec((bm, d), lambda s: (row(s), 0))],
        out_shape=[jax.ShapeDtypeStruct((m, d), F32), jax.ShapeDtypeStruct((m, d), BF16)],
        scratch_shapes=[pltpu.VMEM((aw + cwid, d), BF16)],
        compiler_params=_params(1),
        name="out_proj",
    )(attn, conv, w_out, x, g.reshape(1, d))


def _up_kernel(h_ref, w_ref, o_ref, wbf_ref):
    @pl.when(pl.program_id(1) == 0)
    def _():
        _cast_weight(w_ref, wbf_ref)

    acc = jnp.dot(h_ref[...], wbf_ref[...], preferred_element_type=F32)
    o_ref[...] = jnp.square(jnp.maximum(acc, 0.0)).astype(o_ref.dtype)


def _up_proj(h, w_up, layer):
    m, d = h.shape
    f = w_up.shape[2]
    bm = _tile(m, TILE["up_rows"])
    bn = _tile(f, TILE["up_cols"])
    return pl.pallas_call(
        _up_kernel,
        grid=(f // bn, m // bm),
        in_specs=[pl.BlockSpec((bm, d), lambda j, i: (i, 0)),
                  pl.BlockSpec((None, d, bn), lambda j, i: (layer, 0, j))],
        out_specs=pl.BlockSpec((bm, bn), lambda j, i: (i, j)),
        out_shape=jax.ShapeDtypeStruct((m, f), BF16),
        scratch_shapes=[pltpu.VMEM((d, bn), BF16)],
        compiler_params=_params(2),
        name="up_proj",
    )(h, w_up)


def _down_kernel(a_ref, w_ref, x_ref, *rest, with_norm):
    if with_norm:
        g_ref, xo_ref, h_ref = rest
    else:
        (xo_ref,) = rest
    k = pl.program_id(1)
    last = pl.num_programs(1) - 1

    @pl.when(k == 0)
    def _():
        xo_ref[...] = x_ref[...]

    if not with_norm:
        xo_ref[...] += jnp.dot(a_ref[...], w_ref[...], preferred_element_type=F32)
        return

    @pl.when(k < last)
    def _():
        xo_ref[...] += jnp.dot(a_ref[...], w_ref[...], preferred_element_type=F32)

    @pl.when(k == last)
    def _():
        xn = xo_ref[...] + jnp.dot(a_ref[...], w_ref[...], preferred_element_type=F32)
        xo_ref[...] = xn
        ms = jnp.mean(xn * xn, axis=-1, keepdims=True)
        h_ref[...] = (xn * lax.rsqrt(ms + EPS) * g_ref[...]).astype(h_ref.dtype)


def _down_proj(a, w_down_bf, x, g_next):
    m, d = x.shape
    f = a.shape[1]
    bm = _tile(m, TILE["down_rows"])
    bk = _tile(f, TILE["down_k"])
    with_norm = g_next is not None
    row = pl.BlockSpec((bm, d), lambda i, k: (i, 0))
    in_specs = [pl.BlockSpec((bm, bk), lambda i, k: (i, k)),
                pl.BlockSpec((bk, d), lambda i, k: (k, 0)),
                row]
    args = [a, w_down_bf, x]
    out_specs = [row]
    out_shape = [jax.ShapeDtypeStruct((m, d), F32)]
    if with_norm:
        in_specs.append(pl.BlockSpec((1, d), lambda i, k: (0, 0)))
        args.append(g_next.reshape(1, d))
        out_specs.append(row)
        out_shape.append(jax.ShapeDtypeStruct((m, d), BF16))
    outs = pl.pallas_call(
        functools.partial(_down_kernel, with_norm=with_norm),
        grid=(m // bm, f // bk),
        in_specs=in_specs,
        out_specs=out_specs,
        out_shape=out_shape,
        compiler_params=_params(2),
        name="down_proj",
    )(*args)
    return (outs[0], outs[1]) if with_norm else (outs[0], None)


def kernel(x, w_in, w_out, conv_w, q_norm_g, k_norm_g, lambda_q1, lambda_k1, lambda_q2, lambda_k2,
           subln_g, attn_norm_g, mlp_norm_g, w_up, w_down, rel_bias):
    batch, seq, d = x.shape
    depth = w_in.shape[0]
    aw = d // 2
    cwid = d - aw
    n_heads = aw // HEAD_DV
    assert w_in.shape[2] == 3 * aw + 3 * cwid and rel_bias.shape == (NUM_BUCKETS, 2 * n_heads)
    t = _tile(seq, TILE["attn"])

    tables = _bias_tables(rel_bias, n_heads, t)
    scale = HEAD_DK ** -0.5 * LOG2E
    xf = x.reshape(batch * seq, d)
    h = _rmsnorm(xf, attn_norm_g[0])
    for l in range(depth):
        lam_init = 0.8 - 0.6 * math.exp(-0.3 * l)
        qk_gains = jnp.stack([jnp.tile(q_norm_g[l] * scale, 2), jnp.tile(k_norm_g[l], 2)])
        lam_vecs = jnp.stack([lambda_q1[l], lambda_k1[l], lambda_q2[l], lambda_k2[l],
                              jnp.full((HEAD_DK,), lam_init, F32),
                              jnp.full((HEAD_DK,), 1.0 - lam_init, F32)])

        qk, w_down_bf = _qk_proj(h, w_in, l, qk_gains.reshape(2, 1, LANES), aw, w_down)
        conv, v = _conv_v_proj(h, w_in, conv_w, l, aw, cwid, seq)
        attn = _attention(qk, v, tables, lam_vecs, subln_g[l].reshape(1, HEAD_DV),
                          batch, seq, n_heads, t)
        xf, hm = _out_proj(attn, conv, w_out, l, xf, mlp_norm_g[l])
        act = _up_proj(hm, w_up, l)
        xf, h = _down_proj(act, w_down_bf, xf, attn_norm_g[l + 1] if l + 1 < depth else None)
    return xf.reshape(batch, seq, d)
```

```python
import functools
import math

import numpy as np
import jax
import jax.numpy as jnp
from jax import lax
from jax.experimental import pallas as pl
from jax.experimental.pallas import tpu as pltpu

HEAD_DV = 128
HEAD_DK = HEAD_DV // 2
CONV_K = 3
NUM_BUCKETS = 32
MAX_DISTANCE = 128
MAX_EXACT = NUM_BUCKETS // 2
EPS = 1e-6
NEG = -1e30
LOG2E = math.log2(math.e)

LANES = 128
SUBLANES = 8
MXU_DIM = 256
VMEM_LIMIT_BYTES = 61 * 1024 * 1024

TILE = dict(
    norm_rows=512,
    cast_rows=256,
    qk_rows=1024,
    conv_rows=1024, conv_cols=512,
    attn=128,
    out_rows=512, out_cast_rows=512,
    up_rows=2048, up_cols=1024,
    down_rows=1024, down_k=2048,
)

F32 = jnp.float32
BF16 = jnp.bfloat16


def _tile(dim, pref):
    if dim <= pref:
        return dim
    t = (pref // LANES) * LANES
    while dim % t:
        t -= LANES
    return t


def _params(n_axes):
    return pltpu.CompilerParams(
        dimension_semantics=("arbitrary",) * n_axes,
        vmem_limit_bytes=VMEM_LIMIT_BYTES,
    )


def _rmsnorm_kernel(x_ref, g_ref, o_ref):
    x = x_ref[...]
    ms = jnp.mean(x * x, axis=-1, keepdims=True)
    o_ref[...] = (x * lax.rsqrt(ms + EPS) * g_ref[...]).astype(o_ref.dtype)


def _rmsnorm(x, g):
    m, d = x.shape
    tm = _tile(m, TILE["norm_rows"])
    return pl.pallas_call(
        _rmsnorm_kernel,
        grid=(m // tm,),
        in_specs=[pl.BlockSpec((tm, d), lambda i: (i, 0)),
                  pl.BlockSpec((1, d), lambda i: (0, 0))],
        out_specs=pl.BlockSpec((tm, d), lambda i: (i, 0)),
        out_shape=jax.ShapeDtypeStruct((m, d), BF16),
        compiler_params=_params(1),
        name="rmsnorm",
    )(x, g.reshape(1, d))


def _cast_weight(w_ref, wbf_ref):
    k = w_ref.shape[0]
    ck = _tile(k, TILE["cast_rows"])
    for r in range(0, k, ck):
        wbf_ref[r:r + ck, :] = w_ref[r:r + ck, :].astype(BF16)


def _segment_ones(width):
    seg = np.arange(width) // HEAD_DK
    return jnp.asarray(seg[:, None] == seg[None, :], dtype=BF16)


def _qk_kernel(h_ref, w_ref, g_ref, e_ref, wd_ref, o_ref, wdo_ref, wbf_ref):
    @pl.when(pl.program_id(1) == 0)
    def _():
        _cast_weight(w_ref, wbf_ref)

    wdo_ref[...] = wd_ref[...].astype(BF16)
    acc = jnp.dot(h_ref[...], wbf_ref[...], preferred_element_type=F32)
    g = g_ref[...]
    gw = e_ref.shape[0]
    for c in range(0, acc.shape[1], gw):
        a = acc[:, c:c + gw]
        ssq = jnp.dot((a * a).astype(BF16), e_ref[...], preferred_element_type=F32)
        inv = lax.rsqrt(ssq * (1.0 / HEAD_DK) + EPS)
        for cc in range(0, gw, LANES):
            o_ref[:, c + cc:c + cc + LANES] = (
                a[:, cc:cc + LANES] * inv[:, cc:cc + LANES] * g).astype(o_ref.dtype)


def _qk_proj(h, w_in, layer, gains, aw, w_down):
    m, d = h.shape
    bm = _tile(m, TILE["qk_rows"])
    nm = m // bm
    gw = _tile(aw, MXU_DIM)
    f = w_down.shape[1]
    slab = f // (2 * nm)
    assert f % (2 * nm) == 0 and slab % (2 * SUBLANES) == 0
    return pl.pallas_call(
        _qk_kernel,
        grid=(2, nm),
        in_specs=[pl.BlockSpec((bm, d), lambda j, i: (i, 0)),
                  pl.BlockSpec((None, d, aw), lambda j, i: (layer, 0, j)),
                  pl.BlockSpec((None, 1, LANES), lambda j, i: (j, 0, 0)),
                  pl.BlockSpec((gw, gw), lambda j, i: (0, 0)),
                  pl.BlockSpec((None, slab, d), lambda j, i: (layer, j * nm + i, 0))],
        out_specs=[pl.BlockSpec((bm, aw), lambda j, i: (i, j)),
                   pl.BlockSpec((slab, d), lambda j, i: (j * nm + i, 0))],
        out_shape=[jax.ShapeDtypeStruct((m, 2 * aw), BF16), jax.ShapeDtypeStruct((f, d), BF16)],
        scratch_shapes=[pltpu.VMEM((d, aw), BF16)],
        compiler_params=_params(2),
        name="qk_proj",
    )(h, w_in, gains, _segment_ones(gw), w_down)


def _conv_kernel(h_ref, wb_ref, wc_ref, wi_ref, wv_ref, cw_ref, o_ref, v_ref, wbf_ref, ubuf_ref,
                 *, tiles_per_seq):
    i = pl.program_id(1)
    bm = h_ref.shape[0]

    @pl.when(i == 0)
    def _():
        _cast_weight(wb_ref, wbf_ref.at[0])
        _cast_weight(wc_ref, wbf_ref.at[1])
        _cast_weight(wi_ref, wbf_ref.at[2])
        _cast_weight(wv_ref, wbf_ref.at[3])

    @pl.when(i % tiles_per_seq == 0)
    def _():
        ubuf_ref[0:SUBLANES, :] = jnp.zeros((SUBLANES, ubuf_ref.shape[1]), F32)

    h = h_ref[...]
    gate_c = jnp.dot(h, wbf_ref[1], preferred_element_type=F32)
    conv_in = jnp.dot(h, wbf_ref[2], preferred_element_type=F32)
    ubuf_ref[SUBLANES:SUBLANES + bm, :] = gate_c * conv_in
    gate_b = jnp.dot(h, wbf_ref[0], preferred_element_type=F32)
    cw = cw_ref[...]
    conv = (cw[0:1] * ubuf_ref[SUBLANES - 2:SUBLANES - 2 + bm, :]
            + cw[1:2] * ubuf_ref[SUBLANES - 1:SUBLANES - 1 + bm, :]
            + cw[2:3] * ubuf_ref[SUBLANES:SUBLANES + bm, :])
    o_ref[...] = (gate_b * conv).astype(o_ref.dtype)
    ubuf_ref[0:SUBLANES, :] = ubuf_ref[bm:bm + SUBLANES, :]
    v_ref[...] = jnp.dot(h, wbf_ref[3], preferred_element_type=F32).astype(v_ref.dtype)


def _conv_v_proj(h, w_in, conv_w, layer, aw, cwid, seq):
    m, d = h.shape
    assert aw == cwid
    bm = _tile(seq, TILE["conv_rows"])
    bn = _tile(cwid, TILE["conv_cols"])
    off = 3 * aw // bn
    nb = cwid // bn
    return pl.pallas_call(
        functools.partial(_conv_kernel, tiles_per_seq=seq // bm),
        grid=(nb, m // bm),
        in_specs=[pl.BlockSpec((bm, d), lambda j, i: (i, 0)),
                  pl.BlockSpec((None, d, bn), lambda j, i: (layer, 0, off + j)),
                  pl.BlockSpec((None, d, bn), lambda j, i: (layer, 0, off + nb + j)),
                  pl.BlockSpec((None, d, bn), lambda j, i: (layer, 0, off + 2 * nb + j)),
                  pl.BlockSpec((None, d, bn), lambda j, i: (layer, 0, off - nb + j)),
                  pl.BlockSpec((None, CONV_K, bn), lambda j, i: (layer, 0, j))],
        out_specs=[pl.BlockSpec((bm, bn), lambda j, i: (i, j)),
                   pl.BlockSpec((bm, bn), lambda j, i: (i, j))],
        out_shape=[jax.ShapeDtypeStruct((m, cwid), BF16), jax.ShapeDtypeStruct((m, aw), BF16)],
        scratch_shapes=[pltpu.VMEM((4, d, bn), BF16),
                        pltpu.VMEM((bm + SUBLANES, bn), F32)],
        compiler_params=_params(2),
        name="conv_v_proj",
    )(h, w_in, w_in, w_in, w_in, conv_w)


def _bucket(n):
    nf = np.maximum(n, MAX_EXACT).astype(np.float64)
    val = np.log(nf / MAX_EXACT) / math.log(MAX_DISTANCE / MAX_EXACT) * (NUM_BUCKETS - MAX_EXACT)
    large = np.minimum(MAX_EXACT + val.astype(np.int64), NUM_BUCKETS - 1)
    return np.where(n < MAX_EXACT, n, large)


def _bucket_blocks():
    n = LANES
    assert _bucket(np.array([n + 1]))[0] == NUM_BUCKETS - 1
    r = np.arange(n)[:, None]
    c = np.arange(n)[None, :]
    d0 = r - c
    diag = np.where(d0 >= 0, _bucket(np.maximum(d0, 0)), -1)
    sub = _bucket(n + r - c)
    return np.stack([diag, sub]).astype(np.int32)


def _bias_table_kernel(rb_ref, bk_ref, o_ref):
    h = pl.program_id(0)
    n = LANES
    t = o_ref.shape[2]
    zero = jnp.zeros((n, n), F32)
    masked = jnp.full((n, n), NEG, F32)
    for mp in range(2):
        col = 2 * h + mp
        far = rb_ref[NUM_BUCKETS - 1, col]
        blocks = []
        for kind in range(2):
            bk = bk_ref[kind]
            acc = jnp.where(bk < 0, NEG, 0.0).astype(F32)
            for b in range(NUM_BUCKETS - 1):
                acc = jnp.where(bk == b, (rb_ref[b, col] - far) * LOG2E, acc)
            blocks.append(acc)
        diag, sub = blocks
        for r in range(t // n):
            d = t // n + r
            for c in range(2 * t // n):
                blk = diag if c == d else sub if c == d - 1 else masked if c > d else zero
                o_ref[0, mp, r * n:(r + 1) * n, c * n:(c + 1) * n] = blk


def _bias_tables(rel_bias, n_heads, t):
    assert t % LANES == 0
    buckets = jnp.asarray(_bucket_blocks())
    return pl.pallas_call(
        _bias_table_kernel,
        grid=(n_heads,),
        in_specs=[pl.BlockSpec(memory_space=pltpu.SMEM),
                  pl.BlockSpec((2, LANES, LANES), lambda h: (0, 0, 0))],
        out_specs=pl.BlockSpec((1, 2, t, 2 * t), lambda h: (h, 0, 0, 0)),
        out_shape=jax.ShapeDtypeStruct((n_heads, 2, t, 2 * t), F32),
        compiler_params=_params(1),
        name="bias_tables",
    )(rel_bias, buckets)


def _fold_lanes(x, op):
    return functools.reduce(op, [x[:, c:c + LANES] for c in range(0, x.shape[1], LANES)])


def _attn_kernel(q_ref, k_ref, v_ref, tb_ref, lam_ref, g_ref, o_ref, v1_ref, *, t):
    seq = q_ref.shape[0]
    heads = q_ref.shape[1] // HEAD_DV
    for hh in range(heads):
        v1_ref[hh, :, 0:HEAD_DV] = v_ref[:, hh * HEAD_DV:(hh + 1) * HEAD_DV]
        v1_ref[hh, :, HEAD_DV:] = jnp.ones((seq, LANES), BF16)
    lo = lax.broadcasted_iota(jnp.int32, (1, LANES), 1) < HEAD_DK
    lv = lam_ref[...]
    lam = (jnp.exp(jnp.sum(lv[0:1] * lv[1:2], axis=-1, keepdims=True))
           - jnp.exp(jnp.sum(lv[2:3] * lv[3:4], axis=-1, keepdims=True)) + lv[4:5, 0:1])
    out_gain = lv[5:6, 0:1]
    g = g_ref[...]
    contract_last = (((1,), (1,)), ((), ()))

    def tile(hh, i):
        cols = slice(hh * HEAD_DV, (hh + 1) * HEAD_DV)
        q = q_ref[i * t:(i + 1) * t, cols]
        zero = jnp.zeros_like(q)
        q2 = jnp.concatenate([jnp.where(lo, q, zero), jnp.where(lo, zero, q)], axis=0)

        pieces = []
        if i >= 2:
            pieces.append((0, (i - 1) * t, False))
        near = min(i, 1) * t
        pieces.append((i * t - near, near + t, True))

        scores = []
        for start, width, biased in pieces:
            s = lax.dot_general(q2, k_ref[start:start + width, cols], contract_last,
                                preferred_element_type=F32)
            if biased:
                s = s + tb_ref[hh, :, :, 2 * t - width:].reshape(2 * t, width)
            scores.append(s)
        m = functools.reduce(jnp.maximum, [_fold_lanes(s, jnp.maximum) for s in scores])
        m = jnp.max(m, axis=-1, keepdims=True)
        acc = None
        for s, (start, width, _) in zip(scores, pieces):
            p = jnp.exp2(s - m).astype(BF16)
            pv = jnp.dot(p, v1_ref[hh, start:start + width, :], preferred_element_type=F32)
            acc = pv if acc is None else acc + pv
        out = acc[:, 0:HEAD_DV] / acc[:, HEAD_DV:]
        o = out[0:t] - lam * out[t:2 * t]
        ms = jnp.mean(o * o, axis=-1, keepdims=True)
        o_ref[i * t:(i + 1) * t, cols] = (o * lax.rsqrt(ms + EPS) * g * out_gain).astype(o_ref.dtype)

    nq = seq // t
    for i in [1, 0] + list(range(nq - 1, 1, -1)):
        for hh in range(heads):
            tile(hh, i)


def _attention(qk, v, tables, lam_vecs, subln_g, batch, seq, n_heads, t):
    hps = 2 if n_heads % 2 == 0 else 1
    w = hps * HEAD_DV
    ng = n_heads // hps
    return pl.pallas_call(
        functools.partial(_attn_kernel, t=t),
        grid=(ng, batch),
        in_specs=[pl.BlockSpec((seq, w), lambda h, b: (b, h)),
                  pl.BlockSpec((seq, w), lambda h, b: (b, ng + h)),
                  pl.BlockSpec((seq, w), lambda h, b: (b, h)),
                  pl.BlockSpec((hps, 2, t, 2 * t), lambda h, b: (h, 0, 0, 0)),
                  pl.BlockSpec((6, HEAD_DK), lambda h, b: (0, 0)),
                  pl.BlockSpec((1, HEAD_DV), lambda h, b: (0, 0))],
        out_specs=pl.BlockSpec((seq, w), lambda h, b: (b, h)),
        out_shape=jax.ShapeDtypeStruct((batch * seq, n_heads * HEAD_DV), BF16),
        scratch_shapes=[pltpu.VMEM((hps, seq, HEAD_DV + LANES), BF16)],
        compiler_params=_params(2),
        name="diff_attention",
    )(qk, qk, v, tables, lam_vecs, subln_g)


def _out_kernel(a_ref, c_ref, w_ref, x_ref, g_ref, xo_ref, h_ref, wbf_ref, *, n_pro):
    s = pl.program_id(0)
    ck = w_ref.shape[0]

    @pl.when(s < n_pro)
    def _():
        wbf_ref[pl.ds(pl.multiple_of(s * ck, ck), ck), :] = w_ref[...].astype(BF16)

    @pl.when(s >= n_pro)
    def _():
        aw = a_ref.shape[1]
        acc = jnp.dot(a_ref[...], wbf_ref[0:aw, :], preferred_element_type=F32)
        acc = acc + jnp.dot(c_ref[...], wbf_ref[aw:, :], preferred_element_type=F32)
        xn = x_ref[...] + acc
        xo_ref[...] = xn
        ms = jnp.mean(xn * xn, axis=-1, keepdims=True)
        h_ref[...] = (xn * lax.rsqrt(ms + EPS) * g_ref[...]).astype(h_ref.dtype)


def _out_proj(attn, conv, w_out, layer, x, g):
    m, d = x.shape
    aw, cwid = attn.shape[1], conv.shape[1]
    bm = _tile(m, TILE["out_rows"])
    ck = _tile(aw + cwid, TILE["out_cast_rows"])
    n_pro = (aw + cwid) // ck

    def row(s):
        return jnp.maximum(s - n_pro, 0)

    return pl.pallas_call(
        functools.partial(_out_kernel, n_pro=n_pro),
        grid=(n_pro + m // bm,),
        in_specs=[pl.BlockSpec((bm, aw), lambda s: (row(s), 0)),
                  pl.BlockSpec((bm, cwid), lambda s: (row(s), 0)),
                  pl.BlockSpec((None, ck, d), lambda s: (layer, jnp.minimum(s, n_pro - 1), 0)),
                  pl.BlockSpec((bm, d), lambda s: (row(s), 0)),
                  pl.BlockSpec((1, d), lambda s: (0, 0))],
        out_specs=[pl.BlockSpec((bm, d), lambda s: (row(s), 0)),
                   pl.BlockSpec((bm, d), lambda s: (row(s), 0))],
        out_shape=[jax.ShapeDtypeStruct((m, d), F32), jax.ShapeDtypeStruct((m, d), BF16)],
        scratch_shapes=[pltpu.VMEM((aw + cwid, d), BF16)],
        compiler_params=_params(1),
        name="out_proj",
    )(attn, conv, w_out, x, g.reshape(1, d))


def _up_kernel(h_ref, w_ref, o_ref, wbf_ref):
    @pl.when(pl.program_id(1) == 0)
    def _():
        _cast_weight(w_ref, wbf_ref)

    acc = jnp.dot(h_ref[...], wbf_ref[...], preferred_element_type=F32)
    o_ref[...] = jnp.square(jnp.maximum(acc, 0.0)).astype(o_ref.dtype)


def _up_proj(h, w_up, layer):
    m, d = h.shape
    f = w_up.shape[2]
    bm = _tile(m, TILE["up_rows"])
    bn = _tile(f, TILE["up_cols"])
    return pl.pallas_call(
        _up_kernel,
        grid=(f // bn, m // bm),
        in_specs=[pl.BlockSpec((bm, d), lambda j, i: (i, 0)),
                  pl.BlockSpec((None, d, bn), lambda j, i: (layer, 0, j))],
        out_specs=pl.BlockSpec((bm, bn), lambda j, i: (i, j)),
        out_shape=jax.ShapeDtypeStruct((m, f), BF16),
        scratch_shapes=[pltpu.VMEM((d, bn), BF16)],
        compiler_params=_params(2),
        name="up_proj",
    )(h, w_up)


def _down_kernel(a_ref, w_ref, x_ref, *rest, with_norm):
    if with_norm:
        g_ref, xo_ref, h_ref = rest
    else:
        (xo_ref,) = rest
    k = pl.program_id(1)
    last = pl.num_programs(1) - 1

    @pl.when(k == 0)
    def _():
        xo_ref[...] = x_ref[...]

    if not with_norm:
        xo_ref[...] += jnp.dot(a_ref[...], w_ref[...], preferred_element_type=F32)
        return

    @pl.when(k < last)
    def _():
        xo_ref[...] += jnp.dot(a_ref[...], w_ref[...], preferred_element_type=F32)

    @pl.when(k == last)
    def _():
        xn = xo_ref[...] + jnp.dot(a_ref[...], w_ref[...], preferred_element_type=F32)
        xo_ref[...] = xn
        ms = jnp.mean(xn * xn, axis=-1, keepdims=True)
        h_ref[...] = (xn * lax.rsqrt(ms + EPS) * g_ref[...]).astype(h_ref.dtype)


def _down_proj(a, w_down_bf, x, g_next):
    m, d = x.shape
    f = a.shape[1]
    bm = _tile(m, TILE["down_rows"])
    bk = _tile(f, TILE["down_k"])
    with_norm = g_next is not None
    row = pl.BlockSpec((bm, d), lambda i, k: (i, 0))
    in_specs = [pl.BlockSpec((bm, bk), lambda i, k: (i, k)),
                pl.BlockSpec((bk, d), lambda i, k: (k, 0)),
                pl.BlockSpec((bm, d), lambda i, k: (i, 0), pipeline_mode=pl.Buffered(1))]
    args = [a, w_down_bf, x]
    out_specs = [row]
    out_shape = [jax.ShapeDtypeStruct((m, d), F32)]
    if with_norm:
        in_specs.append(pl.BlockSpec((1, d), lambda i, k: (0, 0)))
        args.append(g_next.reshape(1, d))
        out_specs.append(row)
        out_shape.append(jax.ShapeDtypeStruct((m, d), BF16))
    outs = pl.pallas_call(
        functools.partial(_down_kernel, with_norm=with_norm),
        grid=(m // bm, f // bk),
        in_specs=in_specs,
        out_specs=out_specs,
        out_shape=out_shape,
        compiler_params=_params(2),
        name="down_proj",
    )(*args)
    return (outs[0], outs[1]) if with_norm else (outs[0], None)


def kernel(x, w_in, w_out, conv_w, q_norm_g, k_norm_g, lambda_q1, lambda_k1, lambda_q2, lambda_k2,
           subln_g, attn_norm_g, mlp_norm_g, w_up, w_down, rel_bias):
    batch, seq, d = x.shape
    depth = w_in.shape[0]
    aw = d // 2
    cwid = d - aw
    n_heads = aw // HEAD_DV
    assert w_in.shape[2] == 3 * aw + 3 * cwid and rel_bias.shape == (NUM_BUCKETS, 2 * n_heads)
    t = _tile(seq, TILE["attn"])

    tables = _bias_tables(rel_bias, n_heads, t)
    scale = HEAD_DK ** -0.5 * LOG2E
    xf = x.reshape(batch * seq, d)
    h = _rmsnorm(xf, attn_norm_g[0])
    for l in range(depth):
        lam_init = 0.8 - 0.6 * math.exp(-0.3 * l)
        qk_gains = jnp.stack([jnp.tile(q_norm_g[l] * scale, 2), jnp.tile(k_norm_g[l], 2)])
        lam_vecs = jnp.stack([lambda_q1[l], lambda_k1[l], lambda_q2[l], lambda_k2[l],
                              jnp.full((HEAD_DK,), lam_init, F32),
                              jnp.full((HEAD_DK,), 1.0 - lam_init, F32)])

        qk, w_down_bf = _qk_proj(h, w_in, l, qk_gains.reshape(2, 1, LANES), aw, w_down)
        conv, v = _conv_v_proj(h, w_in, conv_w, l, aw, cwid, seq)
        attn = _attention(qk, v, tables, lam_vecs, subln_g[l].reshape(1, HEAD_DV),
                          batch, seq, n_heads, t)
        xf, hm = _out_proj(attn, conv, w_out, l, xf, mlp_norm_g[l])
        act = _up_proj(hm, w_up, l)
        xf, h = _down_proj(act, w_down_bf, xf, attn_norm_g[l + 1] if l + 1 < depth else None)
    return xf.reshape(batch, seq, d)
```
